```python
import jax, jax.numpy as jnp
from jax import lax
import numpy as np

D_MODEL = 1024
BATCH = 2
SEQ = 16384
DEPTH = 4

ATT_WIDTH = D_MODEL // 2
ATT_DIM = 64
ATT_HEADS = ATT_WIDTH // ATT_DIM
ATT_KV_HEADS = 2
KV_WIDTH = ATT_KV_HEADS * ATT_DIM
ROPE_DIM = ATT_DIM // 4
ROPE_THETA = 500000.0
CMP_BLOCK = 32
CMP_STRIDE = 16
SEL_BLOCK = 64
SEL_TOPK = 16
WINDOW = 512
Q_BLOCK = 128
FORCE_BONUS = 1.0e4
M_WIDTH = D_MODEL - ATT_WIDTH
M_HEADS = 4
M_DIM = M_WIDTH // M_HEADS
M_CHUNK = 64
M_CONV = 4
D_FF = ((8 * D_MODEL // 3 + 127) // 128) * 128
FFN_CONV = 3
RMS_EPS = 1e-6
IN_SIZES = (ATT_WIDTH, KV_WIDTH, KV_WIDTH, KV_WIDTH, KV_WIDTH, KV_WIDTH, KV_WIDTH,
            3 * ATT_HEADS, 2 * M_WIDTH, M_WIDTH, 2 * M_HEADS, M_WIDTH)
IN_COLS = sum(IN_SIZES)

kernel_name = 'hybrid_nsa_mlstm_convffn_adaln'


def rms_norm(x, g):
    x32 = x.astype(jnp.float32)
    y = x32 * lax.rsqrt(jnp.mean(x32 * x32, axis=-1, keepdims=True) + RMS_EPS)
    return y.astype(x.dtype) * g


def head_rms_norm(x, g, n_heads):
    B, S, W = x.shape
    x32 = x.reshape(B, S, n_heads, W // n_heads).astype(jnp.float32)
    y = x32 * lax.rsqrt(jnp.mean(x32 * x32, axis=-1, keepdims=True) + RMS_EPS)
    return y.reshape(B, S, W).astype(x.dtype) * g


def rope_tables(positions):
    inv_freq = ROPE_THETA ** (-jnp.arange(0, ROPE_DIM, 2, dtype=jnp.float32) / ROPE_DIM)
    ang = positions.astype(jnp.float32)[..., None] * inv_freq
    return jnp.cos(ang)[:, :, None, :], jnp.sin(ang)[:, :, None, :]


def apply_rope(x, cos, sin):
    half = ROPE_DIM // 2
    x1 = x[..., :half].astype(jnp.float32)
    x2 = x[..., half:ROPE_DIM].astype(jnp.float32)
    rot = jnp.concatenate([x1 * cos - x2 * sin, x1 * sin + x2 * cos], axis=-1).astype(x.dtype)
    return jnp.concatenate([rot, x[..., ROPE_DIM:]], axis=-1)


def causal_dwconv(x, w, b):
    K = w.shape[0]
    S = x.shape[1]
    xp = jnp.pad(x, ((0, 0), (K - 1, 0), (0, 0)))
    y = b
    for k in range(K):
        y = y + xp[:, k:k + S] * w[k]
    return y


def masked_softmax(s, mask):
    s = jnp.where(mask, s, -jnp.inf)
    m = jnp.max(s, axis=-1, keepdims=True)
    m = jnp.where(jnp.isfinite(m), m, 0.0)
    p = jnp.exp(s - m)
    d = jnp.sum(p, axis=-1, keepdims=True)
    return p / jnp.where(d > 0, d, 1.0)


def compress(k, pos, w1, w2):
    B, S, G, D = k.shape
    n_cmp = (S - CMP_BLOCK) // CMP_STRIDE + 1
    idx = jnp.arange(n_cmp)[:, None] * CMP_STRIDE + jnp.arange(CMP_BLOCK)[None, :]
    blocks = k[:, idx] + pos[:, None, :]
    flat = blocks.transpose(0, 1, 3, 2, 4).reshape(B, n_cmp, G, CMP_BLOCK * D)
    return jax.nn.silu(flat @ w1) @ w2


def selection_overlap(seq_len):
    n_cmp = (seq_len - CMP_BLOCK) // CMP_STRIDE + 1
    n_sel = seq_len // SEL_BLOCK
    c_start = jnp.arange(n_cmp)[:, None] * CMP_STRIDE
    s_start = jnp.arange(n_sel)[None, :] * SEL_BLOCK
    ov = (c_start <= s_start + SEL_BLOCK - 1) & (c_start + CMP_BLOCK - 1 >= s_start)
    return ov.astype(jnp.float32)


def nsa_attention(q, k_cmp, v_cmp, k_slc, v_slc, k_win, v_win, gates, overlap):
    B, S, H, D = q.shape
    G = k_slc.shape[2]
    R = H // G
    n_cmp = k_cmp.shape[1]
    n_sel = S // SEL_BLOCK
    k_top = min(SEL_TOPK, n_sel)
    scale = D ** -0.5
    f32 = jnp.float32
    cmp_end = jnp.arange(n_cmp) * CMP_STRIDE + CMP_BLOCK - 1
    ks_blocks = k_slc.reshape(B, n_sel, SEL_BLOCK, G, D).transpose(0, 3, 1, 2, 4)
    vs_blocks = v_slc.reshape(B, n_sel, SEL_BLOCK, G, D).transpose(0, 3, 1, 2, 4)
    kw_pad = jnp.pad(k_win, ((0, 0), (WINDOW, 0), (0, 0), (0, 0)))
    vw_pad = jnp.pad(v_win, ((0, 0), (WINDOW, 0), (0, 0), (0, 0)))
    v_cmp32 = v_cmp.astype(f32)
    bi = jnp.arange(B)[:, None, None, None]
    gi = jnp.arange(G)[None, :, None, None]
    jsel = jnp.arange(n_sel)

    def block(qb):
        t0 = qb * Q_BLOCK
        tpos = t0 + jnp.arange(Q_BLOCK)
        qblk = lax.dynamic_slice_in_dim(q, t0, Q_BLOCK, axis=1).reshape(B, Q_BLOCK, G, R, D)
        gblk = lax.dynamic_slice_in_dim(gates, t0, Q_BLOCK, axis=1).reshape(B, Q_BLOCK, G, R, 3).astype(f32)
        s_c = jnp.einsum('bqgrd,bngd->bgrqn', qblk, k_cmp).astype(f32) * scale
        p_c = masked_softmax(s_c, cmp_end[None, :] <= tpos[:, None])
        o_cmp = jnp.einsum('bgrqn,bngd->bqgrd', p_c, v_cmp32)
        imp = jnp.einsum('bgrqn,nj->bgqj', p_c, overlap)
        j_cur = tpos // SEL_BLOCK
        valid = jsel[None, :] * SEL_BLOCK <= tpos[:, None]
        forced = (jsel[None, :] == 0) | (jsel[None, :] == j_cur[:, None]) | (jsel[None, :] == j_cur[:, None] - 1)
        score = jnp.where(valid, imp + jnp.where(forced, FORCE_BONUS, 0.0), -jnp.inf)
        top_vals, sel_idx = lax.top_k(score, k_top)
        k_sel = ks_blocks[bi, gi, sel_idx]
        v_sel = vs_blocks[bi, gi, sel_idx]
        s_s = jnp.einsum('bqgrd,bgqkld->bgrqkl', qblk, k_sel).astype(f32) * scale
        kpos = sel_idx[..., None] * SEL_BLOCK + jnp.arange(SEL_BLOCK)
        m_s = jnp.isfinite(top_vals)[..., None] & (kpos <= tpos[:, None, None])
        p_s = masked_softmax(s_s.reshape(B, G, R, Q_BLOCK, k_top * SEL_BLOCK),
                             m_s.reshape(B, G, 1, Q_BLOCK, k_top * SEL_BLOCK))
        o_slc = jnp.einsum('bgrqm,bgqmd->bqgrd', p_s,
                           v_sel.reshape(B, G, Q_BLOCK, k_top * SEL_BLOCK, D).astype(f32))
        k_w = lax.dynamic_slice_in_dim(kw_pad, t0, Q_BLOCK + WINDOW, axis=1)
        v_w = lax.dynamic_slice_in_dim(vw_pad, t0, Q_BLOCK + WINDOW, axis=1)
        wpos = t0 - WINDOW + jnp.arange(Q_BLOCK + WINDOW)
        m_w = (wpos[None, :] <= tpos[:, None]) & (wpos[None, :] > tpos[:, None] - WINDOW) & (wpos[None, :] >= 0)
        s_w = jnp.einsum('bqgrd,bkgd->bgrqk', qblk, k_w).astype(f32) * scale
        p_w = masked_softmax(s_w, m_w)
        o_win = jnp.einsum('bgrqk,bkgd->bqgrd', p_w, v_w.astype(f32))
        o = gblk[..., 0:1] * o_cmp + gblk[..., 1:2] * o_slc + gblk[..., 2:3] * o_win
        return o.reshape(B, Q_BLOCK, H * D).astype(q.dtype)

    out = lax.map(block, jnp.arange(S // Q_BLOCK))
    return out.transpose(1, 0, 2, 3).reshape(B, S, H * D)


def mlstm_chunkwise(q, k, v, i_pre, f_pre):
    B, S, H, D = q.shape
    L = M_CHUNK
    nc = S // L
    f32 = jnp.float32

    def chunks(a):
        return a.astype(f32).reshape(B, nc, L, H, -1).transpose(1, 0, 3, 2, 4)

    qc = chunks(q)
    kc = chunks(k) * (D ** -0.5)
    vc = chunks(v)
    ic = chunks(i_pre[..., None])[..., 0]
    lf = jax.nn.log_sigmoid(chunks(f_pre[..., None])[..., 0])
    causal = jnp.tril(jnp.ones((L, L), dtype=bool))

    def step(carry, inp):
        C, n, m = carry
        qt, kt, vt, it, ft = inp
        b = jnp.cumsum(ft, axis=-1)
        dmat = jnp.where(causal, b[..., :, None] - b[..., None, :] + it[..., None, :], -jnp.inf)
        inter = b + m[..., None]
        mt = jnp.maximum(inter, jnp.max(dmat, axis=-1))
        w_intra = jnp.exp(dmat - mt[..., None])
        w_inter = jnp.exp(inter - mt)
        qk = jnp.einsum('bhtd,bhsd->bhts', qt, kt) * w_intra
        num = w_inter[..., None] * jnp.einsum('bhtd,bhde->bhte', qt, C) + jnp.einsum('bhts,bhse->bhte', qk, vt)
        den = w_inter * jnp.einsum('bhtd,bhd->bht', qt, n) + jnp.sum(qk, axis=-1)
        h = num / jnp.maximum(jnp.abs(den), jnp.exp(-mt))[..., None]
        bL = b[..., -1]
        a_s = bL[..., None] - b + it
        m_new = jnp.maximum(bL + m, jnp.max(a_s, axis=-1))
        ws = jnp.exp(a_s - m_new[..., None])
        decay = jnp.exp(bL + m - m_new)
        C_new = decay[..., None, None] * C + jnp.einsum('bhsd,bhse->bhde', kt * ws[..., None], vt)
        n_new = decay[..., None] * n + jnp.einsum('bhs,bhsd->bhd', ws, kt)
        return (C_new, n_new, m_new), h

    init = (jnp.zeros((B, H, D, D), f32), jnp.zeros((B, H, D), f32), jnp.full((B, H), -1e30, f32))
    _, h = lax.scan(step, init, (qc, kc, vc, ic, lf))
    return h.transpose(1, 0, 3, 2, 4).reshape(B, S, H, D).astype(q.dtype)


def token_mixer(h, cos, sin, overlap, w_in, cmp_pos, cmp_w1, cmp_w2, att_norm_g,
                m_conv_w, m_conv_b, m_gate_b, m_norm_g, w_out):
    B, S, _ = h.shape
    splits = [int(i) for i in np.cumsum(IN_SIZES)[:-1]]
    q, kc, vc, ks, vs, kw, vw, g_att, m_qk, m_v, m_if, m_o = jnp.split(h @ w_in, splits, axis=-1)
    q = apply_rope(q.reshape(B, S, ATT_HEADS, ATT_DIM), cos, sin)
    kc = apply_rope(kc.reshape(B, S, ATT_KV_HEADS, ATT_DIM), cos, sin)
    ks = apply_rope(ks.reshape(B, S, ATT_KV_HEADS, ATT_DIM), cos, sin)
    kw = apply_rope(kw.reshape(B, S, ATT_KV_HEADS, ATT_DIM), cos, sin)
    vc = vc.reshape(B, S, ATT_KV_HEADS, ATT_DIM)
    vs = vs.reshape(B, S, ATT_KV_HEADS, ATT_DIM)
    vw = vw.reshape(B, S, ATT_KV_HEADS, ATT_DIM)
    k_cmp = compress(kc, cmp_pos[0], cmp_w1[0], cmp_w2[0])
    v_cmp = compress(vc, cmp_pos[1], cmp_w1[1], cmp_w2[1])
    gates = jax.nn.sigmoid(g_att).reshape(B, S, ATT_HEADS, 3)
    att = nsa_attention(q, k_cmp, v_cmp, ks, vs, kw, vw, gates, overlap)
    att = head_rms_norm(att, att_norm_g, ATT_HEADS)
    mq, mk = jnp.split(jax.nn.silu(causal_dwconv(m_qk, m_conv_w, m_conv_b)), 2, axis=-1)
    m_if = m_if + m_gate_b
    hm = mlstm_chunkwise(mq.reshape(B, S, M_HEADS, M_DIM), mk.reshape(B, S, M_HEADS, M_DIM),
                         m_v.reshape(B, S, M_HEADS, M_DIM), m_if[..., :M_HEADS], m_if[..., M_HEADS:])
    hm = jax.nn.sigmoid(m_o) * hm.reshape(B, S, M_WIDTH)
    hm = head_rms_norm(hm, m_norm_g, M_HEADS)
    return jnp.concatenate([att, hm], axis=-1) @ w_out


def conv_ffn(h, w_up, conv_w, conv_b, w_down):
    a = causal_dwconv(h @ w_up[:, :D_FF], conv_w, conv_b)
    v = h @ w_up[:, D_FF:]
    return (jax.nn.silu(a) * v) @ w_down


def setup_inputs(seed: int = 0) -> dict:
    key = jax.random.key(seed)
    ks = jax.random.split(key, 24)
    n = jax.random.normal
    L_ = DEPTH
    x = n(ks[0], (BATCH, SEQ, D_MODEL), jnp.float32)
    c = n(ks[1], (BATCH, D_MODEL), jnp.float32)
    offset = jax.random.randint(ks[2], (BATCH, 1), 0, 4096, dtype=jnp.int32)
    positions = offset + jnp.arange(SEQ, dtype=jnp.int32)[None, :]
    f_bias = jnp.linspace(3.0, 6.0, M_HEADS, dtype=jnp.float32)[None, :] + 0.1 * n(ks[3], (L_, M_HEADS))
    i_bias = 0.1 * n(ks[4], (L_, M_HEADS))
    return {
        'x': x,
        'c': c,
        'positions': positions,
        'norm1_g': 1.0 + 0.05 * n(ks[5], (L_, D_MODEL)),
        'norm2_g': 1.0 + 0.05 * n(ks[6], (L_, D_MODEL)),
        'ada_w': n(ks[7], (L_, D_MODEL, 6 * D_MODEL)) * (0.5 * D_MODEL ** -0.5),
        'ada_b': 0.02 * n(ks[8], (L_, 6 * D_MODEL)),
        'w_in': n(ks[9], (L_, D_MODEL, IN_COLS)) * D_MODEL ** -0.5,
        'cmp_pos': 0.1 * n(ks[10], (L_, 2, CMP_BLOCK, ATT_DIM)),
        'cmp_w1': n(ks[11], (L_, 2, CMP_BLOCK * ATT_DIM, ATT_DIM)) * (CMP_BLOCK * ATT_DIM) ** -0.5,
        'cmp_w2': n(ks[12], (L_, 2, ATT_DIM, ATT_DIM)) * ATT_DIM ** -0.5,
        'att_norm_g': 1.0 + 0.05 * n(ks[13], (L_, ATT_WIDTH)),
        'm_conv_w': n(ks[14], (L_, M_CONV, 2 * M_WIDTH)) * M_CONV ** -0.5,
        'm_conv_b': 0.02 * n(ks[15], (L_, 2 * M_WIDTH)),
        'm_gate_b': jnp.concatenate([i_bias, f_bias], axis=-1),
        'm_norm_g': 1.0 + 0.05 * n(ks[16], (L_, M_WIDTH)),
        'w_out': n(ks[17], (L_, D_MODEL, D_MODEL)) * D_MODEL ** -0.5,
        'ffn_up': n(ks[18], (L_, D_MODEL, 2 * D_FF)) * D_MODEL ** -0.5,
        'ffn_conv_w': n(ks[19], (L_, FFN_CONV, D_FF)) * FFN_CONV ** -0.5,
        'ffn_conv_b': 0.02 * n(ks[20], (L_, D_FF)),
        'ffn_down': n(ks[21], (L_, D_FF, D_MODEL)) * D_FF ** -0.5,
        'final_g': 1.0 + 0.05 * n(ks[22], (D_MODEL,)),
    }


def reference(x, c, positions, norm1_g, norm2_g, ada_w, ada_b, w_in, cmp_pos, cmp_w1, cmp_w2,
              att_norm_g, m_conv_w, m_conv_b, m_gate_b, m_norm_g, w_out, ffn_up, ffn_conv_w,
              ffn_conv_b, ffn_down, final_g):
    cos, sin = rope_tables(positions)
    overlap = selection_overlap(x.shape[1])
    c_act = jax.nn.silu(c)
    for l in range(DEPTH):
        mod = c_act @ ada_w[l] + ada_b[l]
        sh1, sc1, g1, sh2, sc2, g2 = [m[:, None, :] for m in jnp.split(mod, 6, axis=-1)]
        h = rms_norm(x, norm1_g[l]) * (1.0 + sc1) + sh1
        x = x + g1 * token_mixer(h, cos, sin, overlap, w_in[l], cmp_pos[l], cmp_w1[l], cmp_w2[l],
                                 att_norm_g[l], m_conv_w[l], m_conv_b[l], m_gate_b[l], m_norm_g[l], w_out[l])
        h = rms_norm(x, norm2_g[l]) * (1.0 + sc2) + sh2
        x = x + g2 * conv_ffn(h, ffn_up[l], ffn_conv_w[l], ffn_conv_b[l], ffn_down[l])
    return rms_norm(x, final_g)
```

```python
import functools

import jax
import jax.numpy as jnp
from jax import lax
from jax.experimental import pallas as pl
from jax.experimental.pallas import tpu as pltpu

F32 = jnp.float32
BF16 = jnp.bfloat16
MXU_DT = jnp.bfloat16

ATT_DIM = 64
ATT_HEADS = 8
ATT_KV_HEADS = 2
ATT_REP = ATT_HEADS // ATT_KV_HEADS
ATT_WIDTH = ATT_HEADS * ATT_DIM
KV_WIDTH = ATT_KV_HEADS * ATT_DIM
ROPE_DIM = ATT_DIM // 4
ROPE_HALF = ROPE_DIM // 2
ROPE_THETA = 500000.0
CMP_BLOCK = 32
CMP_STRIDE = 16
SEL_BLOCK = 64
SEL_TOPK = 16
WINDOW = 512
FORCE_BONUS = 1.0e4
M_HEADS = 4
M_DIM = 128
M_WIDTH = M_HEADS * M_DIM
M_CONV = 4
FFN_CONV = 3
RMS_EPS = 1e-6
NEG = -1e30

LANES = 128
SUBLANES = 8
VMEM_LIMIT = 56 * 1024 * 1024

COL_Q = 0
COL_KC = 512
COL_VC = 640
COL_KS = 768
COL_VS = 896
COL_KW = 1024
COL_VW = 1152
COL_MQK = 1280
COL_MV = 2304
COL_MO = 2816
COL_SMALL = 3328
IN_COLS_PACKED = 3456
SMALL_I = 3 * ATT_HEADS
SMALL_F = SMALL_I + M_HEADS

_NT = (((1,), (1,)), ((), ()))
_TN = (((0,), (0,)), ((), ()))


def _params(n_axes):
    return pltpu.CompilerParams(dimension_semantics=("arbitrary",) * n_axes,
                                vmem_limit_bytes=VMEM_LIMIT)


def _dot(a, b):
    return jnp.dot(a.astype(MXU_DT), b.astype(MXU_DT), preferred_element_type=F32)


def _dot_nt(a, b):
    return lax.dot_general(a.astype(MXU_DT), b.astype(MXU_DT), _NT, preferred_element_type=F32)


def _dot_tn(a, b):
    return lax.dot_general(a.astype(MXU_DT), b.astype(MXU_DT), _TN, preferred_element_type=F32)


def _sigmoid(x):
    return 1.0 / (1.0 + jnp.exp(-x))


def _silu(x):
    return x * _sigmoid(x)


def _shift_rows(x, prev8, k):
    rolled = pltpu.roll(x, k, 0)
    fix = pltpu.roll(prev8, k, 0)
    row = lax.broadcasted_iota(jnp.int32, fix.shape, 0)
    top = jnp.where(row < k, fix, rolled[:SUBLANES])
    return jnp.concatenate([top, rolled[SUBLANES:]], axis=0)


def _ada_body(c_ref, w_ref, b_ref, o_ref):
    c = c_ref[...]
    o_ref[0] = _dot(_silu(c), w_ref[0]) + b_ref[0]


def _ada_call(c8, ada_w, ada_b3, tn):
    depth, d, n = ada_w.shape
    return pl.pallas_call(
        _ada_body,
        grid=(depth, n // tn),
        in_specs=[pl.BlockSpec((SUBLANES, d), lambda l, j: (0, 0)),
                  pl.BlockSpec((1, d, tn), lambda l, j: (l, 0, j)),
                  pl.BlockSpec((1, 1, tn), lambda l, j: (l, 0, j))],
        out_specs=pl.BlockSpec((1, SUBLANES, tn), lambda l, j: (l, 0, j)),
        out_shape=jax.ShapeDtypeStruct((depth, SUBLANES, n), F32),
        compiler_params=_params(2),
        name="ada",
    )(c8, ada_w, ada_b3)


def _rope_body(pos_ref, freq_ref, rc_ref, rs1_ref, rs2_ref):
    pos = pos_ref[0].astype(F32)
    ang = pos * freq_ref[...]
    d = lax.broadcasted_iota(jnp.int32, ang.shape, 1) % ATT_DIM
    cos = jnp.cos(ang)
    sin = jnp.sin(ang)
    rc_ref[0] = jnp.where(d < ROPE_DIM, cos, 1.0)
    rs1_ref[0] = jnp.where(d < ROPE_HALF, -sin, 0.0)
    rs2_ref[0] = jnp.where((d >= ROPE_HALF) & (d < ROPE_DIM), sin, 0.0)


def _rope_call(pos3, freq, tm):
    b, s, _ = pos3.shape
    spec = pl.BlockSpec((1, tm, LANES), lambda i, j: (i, j, 0))
    shp = jax.ShapeDtypeStruct((b, s, LANES), F32)
    return pl.pallas_call(
        _rope_body,
        grid=(b, s // tm),
        in_specs=[pl.BlockSpec((1, tm, 1), lambda i, j: (i, j, 0)),
                  pl.BlockSpec((1, LANES), lambda i, j: (0, 0))],
        out_specs=[spec, spec, spec],
        out_shape=[shp, shp, shp],
        compiler_params=_params(2),
        name="rope",
    )(pos3, freq)


def _proj_body(x_ref, sc_ref, sh_ref, g_ref, w_ref, rc_ref, rs1_ref, rs2_ref,
               q_ref, kc_ref, vc_ref, ks_ref, vst_ref, kw_ref, vwt_ref,
               mqk_ref, mv_ref, mo_ref, sm_ref, smt_ref):
    x = x_ref[0]
    var = jnp.mean(x * x, axis=-1, keepdims=True)
    h = (x * lax.rsqrt(var + RMS_EPS)) * g_ref[...]
    h = h * (1.0 + sc_ref[0]) + sh_ref[0]
    hb = h.astype(MXU_DT)

    def mm(c0, n):
        return jnp.dot(hb, w_ref[:, c0:c0 + n], preferred_element_type=F32)

    rc, rs1, rs2 = rc_ref[0], rs1_ref[0], rs2_ref[0]

    def rope(y):
        return (y * rc + pltpu.roll(y, LANES - ROPE_HALF, 1) * rs1
                + pltpu.roll(y, ROPE_HALF, 1) * rs2)

    scale = ATT_DIM ** -0.5
    for j in range(ATT_HEADS // 2):
        y = rope(mm(COL_Q + LANES * j, LANES)) * scale
        q_ref[0, 2 * j] = y[:, :ATT_DIM].astype(q_ref.dtype)
        q_ref[0, 2 * j + 1] = y[:, ATT_DIM:].astype(q_ref.dtype)
    kc_ref[0] = rope(mm(COL_KC, KV_WIDTH))
    vc_ref[0] = mm(COL_VC, KV_WIDTH)
    y = rope(mm(COL_KS, KV_WIDTH))
    ks_ref[0, 0] = y[:, :ATT_DIM].astype(ks_ref.dtype)
    ks_ref[0, 1] = y[:, ATT_DIM:].astype(ks_ref.dtype)
    vst_ref[0] = mm(COL_VS, KV_WIDTH).T.astype(vst_ref.dtype)
    y = rope(mm(COL_KW, KV_WIDTH))
    kw_ref[0, 0] = y[:, :ATT_DIM].astype(kw_ref.dtype)
    kw_ref[0, 1] = y[:, ATT_DIM:].astype(kw_ref.dtype)
    vwt_ref[0] = mm(COL_VW, KV_WIDTH).T.astype(vwt_ref.dtype)
    for j in range(2 * M_WIDTH // 256):
        mqk_ref[0, :, 256 * j:256 * (j + 1)] = mm(COL_MQK + 256 * j, 256).astype(mqk_ref.dtype)
    for j in range(M_WIDTH // 256):
        mv_ref[0, :, 256 * j:256 * (j + 1)] = mm(COL_MV + 256 * j, 256).astype(mv_ref.dtype)
        mo_ref[0, :, 256 * j:256 * (j + 1)] = mm(COL_MO + 256 * j, 256).astype(mo_ref.dtype)
    sm = mm(COL_SMALL, LANES)
    sm_ref[0] = sm
    smt_ref[0] = sm.T


def _proj_call(x, sc, sh, g, w, rc, rs1, rs2, tm):
    b, s, d = x.shape
    row = lambda n: pl.BlockSpec((1, tm, n), lambda i, j: (i, j, 0))
    vec = pl.BlockSpec((1, 1, d), lambda i, j: (i, 0, 0))
    headed = lambda n: pl.BlockSpec((1, n, tm, ATT_DIM), lambda i, j: (i, 0, j, 0))
    tposed = pl.BlockSpec((1, LANES, tm), lambda i, j: (i, 0, j))
    sds = jax.ShapeDtypeStruct
    return pl.pallas_call(
        _proj_body,
        grid=(b, s // tm),
        in_specs=[row(d), vec, vec,
                  pl.BlockSpec((1, d), lambda i, j: (0, 0)),
                  pl.BlockSpec((d, IN_COLS_PACKED), lambda i, j: (0, 0)),
                  row(LANES), row(LANES), row(LANES)],
        out_specs=[headed(ATT_HEADS), row(KV_WIDTH), row(KV_WIDTH),
                   headed(ATT_KV_HEADS), tposed, headed(ATT_KV_HEADS), tposed,
                   row(2 * M_WIDTH), row(M_WIDTH), row(M_WIDTH), row(LANES), tposed],
        out_shape=[sds((b, ATT_HEADS, s, ATT_DIM), MXU_DT),
                   sds((b, s, KV_WIDTH), F32), sds((b, s, KV_WIDTH), F32),
                   sds((b, ATT_KV_HEADS, s, ATT_DIM), MXU_DT), sds((b, KV_WIDTH, s), MXU_DT),
                   sds((b, ATT_KV_HEADS, s, ATT_DIM), MXU_DT), sds((b, KV_WIDTH, s), MXU_DT),
                   sds((b, s, 2 * M_WIDTH), MXU_DT), sds((b, s, M_WIDTH), MXU_DT),
                   sds((b, s, M_WIDTH), MXU_DT), sds((b, s, LANES), F32), sds((b, LANES, s), F32)],
        compiler_params=_params(2),
        name="proj",
    )(x, sc, sh, g, w, rc, rs1, rs2)


def _compress_body(kr_ref, pa_ref, pb_ref, w1a_ref, w1b_ref, w2_ref, out_ref, outt_ref):
    kr = kr_ref[0]
    nc = kr.shape[0]
    ya = _dot(kr + pa_ref[...], w1a_ref[...])
    yb = _dot(kr + pb_ref[...], w1b_ref[...])
    pre = ya + pltpu.roll(yb, nc - 1, 0)
    out = _dot(_silu(pre), w2_ref[...])
    row = lax.broadcasted_iota(jnp.int32, out.shape, 0)
    out = jnp.where(row < nc - 1, out, 0.0)
    out_ref[0, 0] = out[:, :ATT_DIM].astype(out_ref.dtype)
    out_ref[0, 1] = out[:, ATT_DIM:].astype(out_ref.dtype)
    outt_ref[0] = out.T.astype(outt_ref.dtype)


def _compress_call(kr, pa, pb, w1a, w1b, w2e):
    b, nc, width = kr.shape
    full = lambda a: pl.BlockSpec(a.shape, lambda i: (0,) * a.ndim)
    return pl.pallas_call(
        _compress_body,
        grid=(b,),
        in_specs=[pl.BlockSpec((1, nc, width), lambda i: (i, 0, 0)),
                  full(pa), full(pb), full(w1a), full(w1b), full(w2e)],
        out_specs=[pl.BlockSpec((1, ATT_KV_HEADS, nc, ATT_DIM), lambda i: (i, 0, 0, 0)),
                   pl.BlockSpec((1, KV_WIDTH, nc), lambda i: (i, 0, 0))],
        out_shape=[jax.ShapeDtypeStruct((b, ATT_KV_HEADS, nc, ATT_DIM), MXU_DT),
                   jax.ShapeDtypeStruct((b, KV_WIDTH, nc), MXU_DT)],
        compiler_params=_params(1),
        name="compress",
    )(kr, pa, pb, w1a, w1b, w2e)


def _cmp_body(q_ref, kc_ref, vct_ref, ovt_ref, ocmp_ref, bias_ref, *, tq, k_top):
    t0 = pl.program_id(2) * tq
    kc = kc_ref[0, 0]
    vct = vct_ref[0]
    nc = kc.shape[0]
    row = lax.broadcasted_iota(jnp.int32, (nc, tq), 0)
    tpos = t0 + lax.broadcasted_iota(jnp.int32, (nc, tq), 1)
    mask = row * CMP_STRIDE + (CMP_BLOCK - 1) <= tpos
    psum = jnp.zeros((nc, tq), F32)
    for r in range(ATT_REP):
        s = jnp.where(mask, _dot_nt(kc, q_ref[0, r]), NEG)
        m = jnp.max(s, axis=0, keepdims=True)
        m = jnp.where(m > 0.5 * NEG, m, 0.0)
        p = jnp.exp(s - m)
        d = jnp.sum(p, axis=0, keepdims=True)
        pn = p / jnp.where(d > 0.0, d, 1.0)
        ocmp_ref[0, r * ATT_DIM:(r + 1) * ATT_DIM, :] = _dot(vct, pn)
        psum = psum + pn
    hi = psum.astype(MXU_DT)
    lo = psum - hi.astype(F32)
    imp = _dot(ovt_ref[...], hi) + _dot(ovt_ref[...], lo)
    nsel = imp.shape[0]
    jidx = lax.broadcasted_iota(jnp.int32, (nsel, tq), 0)
    tq_pos = t0 + lax.broadcasted_iota(jnp.int32, (nsel, tq), 1)
    jcur = tq_pos // SEL_BLOCK
    forced = (jidx == 0) | (jidx == jcur) | (jidx == jcur - 1)
    score = jnp.where(jidx * SEL_BLOCK <= tq_pos, imp + jnp.where(forced, FORCE_BONUS, 0.0), NEG)
    bias = jnp.full((nsel, tq), NEG, F32)
    for _ in range(k_top):
        cur = jnp.max(score, axis=0, keepdims=True)
        first = jnp.min(jnp.where(score == cur, jidx, nsel), axis=0, keepdims=True)
        pick = (jidx == first) & (cur > 0.5 * NEG)
        bias = jnp.where(pick, 0.0, bias)
        score = jnp.where(pick, NEG, score)
    bias_ref[0, 0, 0] = bias


def _cmp_call(q, kcmp, vcmpt, ovt, tq, k_top):
    b, _, s, _ = q.shape
    nc = kcmp.shape[2]
    nsel = ovt.shape[0]
    nq = s // tq
    return pl.pallas_call(
        functools.partial(_cmp_body, tq=tq, k_top=k_top),
        grid=(b, ATT_KV_HEADS, nq),
        in_specs=[pl.BlockSpec((1, ATT_REP, tq, ATT_DIM), lambda i, g, j: (i, g, j, 0)),
                  pl.BlockSpec((1, 1, nc, ATT_DIM), lambda i, g, j: (i, g, 0, 0)),
                  pl.BlockSpec((1, ATT_DIM, nc), lambda i, g, j: (i, g, 0)),
                  pl.BlockSpec((nsel, nc), lambda i, g, j: (0, 0))],
        out_specs=[pl.BlockSpec((1, ATT_REP * ATT_DIM, tq), lambda i, g, j: (i, g, j)),
                   pl.BlockSpec((1, 1, 1, nsel, tq), lambda i, g, j: (i, g, j, 0, 0))],
        out_shape=[jax.ShapeDtypeStruct((b, ATT_WIDTH, s), F32),
                   jax.ShapeDtypeStruct((b, ATT_KV_HEADS, nq, nsel, tq), F32)],
        compiler_params=_params(3),
        name="cmp",
    )(q, kcmp, vcmpt, ovt)


def _slc_body(q_ref, ks_ref, vst_ref, kw_ref, vwt_ref, bias_ref, ocmp_ref, gt_ref, gn_ref,
              o_ref, m_scr, l_scr, acc_scr, *, tq):
    g = pl.program_id(1)
    qi = pl.program_id(2)
    t0 = qi * tq
    nb = tq // SEL_BLOCK
    m_scr[...] = jnp.full(m_scr.shape, NEG, F32)
    l_scr[...] = jnp.zeros(l_scr.shape, F32)
    acc_scr[...] = jnp.zeros(acc_scr.shape, F32)
    row = lax.broadcasted_iota(jnp.int32, (tq, tq), 0)
    lane = lax.broadcasted_iota(jnp.int32, (tq, tq), 1)

    def tile(kv, diag):
        k0 = pl.multiple_of(kv * tq, tq)
        kt = ks_ref[0, 0, pl.ds(k0, tq), :]
        vt = vst_ref[0, :, pl.ds(k0, tq)]
        rows = [jnp.broadcast_to(bias_ref[0, 0, 0, pl.ds(kv * nb + jj, 1), :], (SEL_BLOCK, tq))
                for jj in range(nb)]
        btile = jnp.concatenate(rows, axis=0)
        if diag:
            btile = jnp.where(row <= lane, btile, NEG)
        for r in range(ATT_REP):
            s = _dot_nt(kt, q_ref[0, r]) + btile
            m_old = m_scr[r]
            m_new = jnp.maximum(m_old, jnp.max(s, axis=0, keepdims=True))
            alpha = jnp.exp(m_old - m_new)
            p = jnp.exp(s - m_new)
            l_scr[r] = alpha * l_scr[r] + jnp.sum(p, axis=0, keepdims=True)
            acc_scr[r] = alpha * acc_scr[r] + _dot(vt, p)
            m_scr[r] = m_new

    def loop_body(kv, carry):
        tile(kv, False)
        return carry

    lax.fori_loop(0, qi, loop_body, 0)
    tile(qi, True)

    wk = WINDOW + tq
    s0 = pl.multiple_of(jnp.maximum(t0 - WINDOW, 0), tq)
    kwt = kw_ref[0, 0, pl.ds(s0, wk), :]
    vw = vwt_ref[0, :, pl.ds(s0, wk)]
    kpos = s0 + lax.broadcasted_iota(jnp.int32, (wk, tq), 0)
    tpos = t0 + lax.broadcasted_iota(jnp.int32, (wk, tq), 1)
    wmask = (kpos <= tpos) & (kpos > tpos - WINDOW)
    for r in range(ATT_REP):
        s = jnp.where(wmask, _dot_nt(kwt, q_ref[0, r]), NEG)
        p = jnp.exp(s - jnp.max(s, axis=0, keepdims=True))
        o_win = _dot(vw, p) / jnp.sum(p, axis=0, keepdims=True)
        o_slc = acc_scr[r] / l_scr[r]
        o_cmp = ocmp_ref[0, r * ATT_DIM:(r + 1) * ATT_DIM, :]
        gbase = (g * ATT_REP + r) * 3
        g_cmp = _sigmoid(gt_ref[0, pl.ds(gbase, 1), :])
        g_slc = _sigmoid(gt_ref[0, pl.ds(gbase + 1, 1), :])
        g_win = _sigmoid(gt_ref[0, pl.ds(gbase + 2, 1), :])
        o = g_cmp * o_cmp + g_slc * o_slc + g_win * o_win
        o = o * lax.rsqrt(jnp.mean(o * o, axis=0, keepdims=True) + RMS_EPS)
        o = o * gn_ref[r * ATT_DIM:(r + 1) * ATT_DIM, :]
        o_ref[0, r * ATT_DIM:(r + 1) * ATT_DIM, :] = o.astype(o_ref.dtype)


def _slc_call(q, ks, vst, kw, vwt, bias, ocmp, smallt, gn, tq):
    b, _, s, _ = q.shape
    nsel = bias.shape[3]
    gw = ATT_REP * ATT_DIM
    kres = pl.BlockSpec((1, 1, s, ATT_DIM), lambda i, g, j: (i, g, 0, 0))
    vres = pl.BlockSpec((1, ATT_DIM, s), lambda i, g, j: (i, g, 0))
    return pl.pallas_call(
        functools.partial(_slc_body, tq=tq),
        grid=(b, ATT_KV_HEADS, s // tq),
        in_specs=[pl.BlockSpec((1, ATT_REP, tq, ATT_DIM), lambda i, g, j: (i, g, j, 0)),
                  kres, vres, kres, vres,
                  pl.BlockSpec((1, 1, 1, nsel, tq), lambda i, g, j: (i, g, j, 0, 0)),
                  pl.BlockSpec((1, gw, tq), lambda i, g, j: (i, g, j)),
                  pl.BlockSpec((1, LANES, tq), lambda i, g, j: (i, 0, j)),
                  pl.BlockSpec((gw, tq), lambda i, g, j: (g, 0))],
        out_specs=pl.BlockSpec((1, gw, tq), lambda i, g, j: (i, g, j)),
        out_shape=jax.ShapeDtypeStruct((b, ATT_WIDTH, s), MXU_DT),
        scratch_shapes=[pltpu.VMEM((ATT_REP, 1, tq), F32),
                        pltpu.VMEM((ATT_REP, 1, tq), F32),
                        pltpu.VMEM((ATT_REP, ATT_DIM, tq), F32)],
        compiler_params=_params(3),
        name="slc",
    )(q, ks, vst, kw, vwt, bias, ocmp, smallt, gn)


def _mlstm_body(mqk_ref, mv_ref, mo_ref, sm_ref, smt_ref, cw_ref, cb_ref, gbr_ref, gbc_ref, ng_ref,
                o_ref, c_scr, m_scr, tail_scr, *, chunk):
    @pl.when(pl.program_id(1) == 0)
    def _():
        c_scr[...] = jnp.zeros(c_scr.shape, F32)
        m_scr[...] = jnp.full(m_scr.shape, NEG, F32)
        tail_scr[...] = jnp.zeros(tail_scr.shape, F32)

    x = mqk_ref[0].astype(F32)
    tail = tail_scr[...]
    y = cb_ref[...] + x * cw_ref[M_CONV - 1:M_CONV, :]
    for k in range(1, M_CONV):
        y = y + _shift_rows(x, tail, k) * cw_ref[M_CONV - 1 - k:M_CONV - k, :]
    tail_scr[...] = x[chunk - SUBLANES:, :]
    qk = _silu(y)
    sm = sm_ref[0] + gbr_ref[...]
    smt = smt_ref[0] + gbc_ref[...]
    row = lax.broadcasted_iota(jnp.int32, (chunk, chunk), 0)
    col = lax.broadcasted_iota(jnp.int32, (chunk, chunk), 1)
    causal = row >= col
    ones = jnp.ones((chunk, M_DIM), F32)

    def log_sigmoid(z):
        return jnp.minimum(z, 0.0) - jnp.log(1.0 + jnp.exp(-jnp.abs(z)))

    for h in range(M_HEADS):
        q = qk[:, h * M_DIM:(h + 1) * M_DIM]
        k = qk[:, M_WIDTH + h * M_DIM:M_WIDTH + (h + 1) * M_DIM] * (M_DIM ** -0.5)
        v = mv_ref[0, :, h * M_DIM:(h + 1) * M_DIM]
        i_col = sm[:, SMALL_I + h:SMALL_I + h + 1]
        lf_col = log_sigmoid(sm[:, SMALL_F + h:SMALL_F + h + 1])
        i_row = smt[SMALL_I + h:SMALL_I + h + 1, :]
        lf_row = log_sigmoid(smt[SMALL_F + h:SMALL_F + h + 1, :])
        b_col = jnp.sum(jnp.where(causal, lf_row, 0.0), axis=1, keepdims=True)
        b_row = jnp.sum(jnp.where(row <= col, lf_col, 0.0), axis=0, keepdims=True)
        m_prev = m_scr[h]
        dmat = jnp.where(causal, b_col - b_row + i_row, NEG)
        inter = b_col + m_prev
        mt = jnp.maximum(inter, jnp.max(dmat, axis=1, keepdims=True))
        w_intra = jnp.exp(dmat - mt)
        w_inter = jnp.exp(inter - mt)
        qkw = _dot_nt(q, k) * w_intra
        c_aug = c_scr[h]
        r1 = _dot(q, c_aug)
        num = w_inter * r1[:, :M_DIM] + _dot(qkw, v)
        den = w_inter * r1[:, M_DIM:M_DIM + 1] + jnp.sum(qkw, axis=1, keepdims=True)
        hh = num / jnp.maximum(jnp.abs(den), jnp.exp(-mt))
        b_last = b_col[chunk - 1:chunk, :]
        a_col = b_last - b_col + i_col
        m_new = jnp.maximum(b_last + m_prev, jnp.max(a_col, axis=0, keepdims=True))
        ws = jnp.exp(a_col - m_new)
        decay = jnp.exp(b_last + m_prev - m_new)
        v_aug = jnp.concatenate([v.astype(F32), ones], axis=1)
        c_scr[h] = decay * c_aug + _dot_tn(k * ws, v_aug)
        m_scr[h] = m_new
        hm = _sigmoid(mo_ref[0, :, h * M_DIM:(h + 1) * M_DIM].astype(F32)) * hh
        hm = hm * lax.rsqrt(jnp.mean(hm * hm, axis=1, keepdims=True) + RMS_EPS)
        o_ref[0, :, h * M_DIM:(h + 1) * M_DIM] = (hm * ng_ref[:, h * M_DIM:(h + 1) * M_DIM]).astype(o_ref.dtype)


def _mlstm_call(mqk, mv, mo, small, smallt, cw, cb, gbr, gbc, ng, chunk):
    b, s, _ = mv.shape
    row = lambda n: pl.BlockSpec((1, chunk, n), lambda i, j: (i, j, 0))
    full = lambda a: pl.BlockSpec(a.shape, lambda i, j: (0,) * a.ndim)
    return pl.pallas_call(
        functools.partial(_mlstm_body, chunk=chunk),
        grid=(b, s // chunk),
        in_specs=[row(2 * M_WIDTH), row(M_WIDTH), row(M_WIDTH), row(LANES),
                  pl.BlockSpec((1, LANES, chunk), lambda i, j: (i, 0, j)),
                  full(cw), full(cb), full(gbr), full(gbc), full(ng)],
        out_specs=row(M_WIDTH),
        out_shape=jax.ShapeDtypeStruct((b, s, M_WIDTH), MXU_DT),
        scratch_shapes=[pltpu.VMEM((M_HEADS, M_DIM, 2 * M_DIM), F32),
                        pltpu.VMEM((M_HEADS, 1, 1), F32),
                        pltpu.VMEM((SUBLANES, 2 * M_WIDTH), F32)],
        compiler_params=_params(2),
        name="mlstm",
    )(mqk, mv, mo, small, smallt, cw, cb, gbr, gbc, ng)


def _mix_ffn_body(x_ref, attt_ref, hm_ref, wo_ref, g1_ref, sc_ref, sh_ref, g2_ref, ng_ref,
                  wup_ref, cw_ref, cb_ref, wdn_ref, fg_ref, o_ref, tail_scr, *, d_ff, tf, tiles_per_seq, final):
    @pl.when(pl.program_id(0) % tiles_per_seq == 0)
    def _():
        tail_scr[...] = jnp.zeros(tail_scr.shape, F32)

    d = x_ref.shape[-1]
    mix = _dot_tn(attt_ref[0], wo_ref[:ATT_WIDTH, :]) + _dot(hm_ref[0], wo_ref[ATT_WIDTH:, :])
    x = x_ref[0] + g1_ref[0] * mix
    var = jnp.mean(x * x, axis=-1, keepdims=True)
    h = (x * lax.rsqrt(var + RMS_EPS)) * ng_ref[...]
    hb = (h * (1.0 + sc_ref[0]) + sh_ref[0]).astype(MXU_DT)
    acc = jnp.zeros((x.shape[0], d), F32)
    for c in range(d_ff // tf):
        a = jnp.dot(hb, wup_ref[:, c * tf:(c + 1) * tf], preferred_element_type=F32)
        v = jnp.dot(hb, wup_ref[:, d_ff + c * tf:d_ff + (c + 1) * tf], preferred_element_type=F32)
        tail = tail_scr[c]
        y = cb_ref[:, c * tf:(c + 1) * tf] + a * cw_ref[FFN_CONV - 1:FFN_CONV, c * tf:(c + 1) * tf]
        for k in range(1, FFN_CONV):
            y = y + _shift_rows(a, tail, k) * cw_ref[FFN_CONV - 1 - k:FFN_CONV - k, c * tf:(c + 1) * tf]
        tail_scr[c] = a[a.shape[0] - SUBLANES:, :]
        acc = acc + _dot(_silu(y) * v, wdn_ref[c * tf:(c + 1) * tf, :])
    out = x + g2_ref[0] * acc
    if final:
        var = jnp.mean(out * out, axis=-1, keepdims=True)
        out = (out * lax.rsqrt(var + RMS_EPS)) * fg_ref[...]
    o_ref[0] = out


def _mix_ffn_call(x, attt, hm, wo, g1, sc, sh, g2, ng, wup, cw, cb, wdn, fg, tm, tf, final):
    b, s, d = x.shape
    d_ff = wdn.shape[0]
    tiles_per_seq = s // tm
    row = lambda n: pl.BlockSpec((1, tm, n), lambda i: (i // tiles_per_seq, i % tiles_per_seq, 0))
    vec = pl.BlockSpec((1, 1, d), lambda i: (i // tiles_per_seq, 0, 0))
    full = lambda a: pl.BlockSpec(a.shape, lambda i: (0,) * a.ndim)
    return pl.pallas_call(
        functools.partial(_mix_ffn_body, d_ff=d_ff, tf=tf, tiles_per_seq=tiles_per_seq, final=final),
        grid=(b * tiles_per_seq,),
        in_specs=[row(d),
                  pl.BlockSpec((1, ATT_WIDTH, tm), lambda i: (i // tiles_per_seq, 0, i % tiles_per_seq)),
                  row(M_WIDTH), full(wo), vec, vec, vec, vec, full(ng),
                  full(wup), full(cw), full(cb), full(wdn), full(fg)],
        out_specs=row(d),
        out_shape=jax.ShapeDtypeStruct((b, s, d), F32),
        scratch_shapes=[pltpu.VMEM((d_ff // tf, SUBLANES, tf), F32)],
        compiler_params=_params(1),
        name="mix_ffn",
    )(x, attt, hm, wo, g1, sc, sh, g2, ng, wup, cw, cb, wdn, fg)


def _pick_tile(n, pref):
    t = min(n, pref)
    assert n % t == 0, (n, t)
    return t


def _pack_w_in(w_in):
    att_cols = ATT_WIDTH + 6 * KV_WIDTH
    n_gate = 3 * ATT_HEADS
    o = att_cols
    gates = w_in[:, o:o + n_gate]; o += n_gate
    mqk = w_in[:, o:o + 2 * M_WIDTH]; o += 2 * M_WIDTH
    mv = w_in[:, o:o + M_WIDTH]; o += M_WIDTH
    mif = w_in[:, o:o + 2 * M_HEADS]; o += 2 * M_HEADS
    mo = w_in[:, o:o + M_WIDTH]
    pad = jnp.zeros((w_in.shape[0], LANES - n_gate - 2 * M_HEADS), w_in.dtype)
    return jnp.concatenate([w_in[:, :att_cols], mqk, mv, mo, gates, mif, pad], axis=1).astype(MXU_DT)


def _expand_cmp_weights(pos, w1, w2):
    half = CMP_BLOCK // 2
    w1r = w1.reshape(CMP_BLOCK, ATT_DIM, ATT_DIM)
    eye = jnp.eye(ATT_KV_HEADS, dtype=w1.dtype)

    def expand(wl):
        z = wl[:, None, :, None, :] * eye[None, :, None, :, None]
        return z.reshape(half * KV_WIDTH, KV_WIDTH).astype(MXU_DT)

    def expand_pos(pl_):
        return jnp.broadcast_to(pl_[:, None, :], (half, ATT_KV_HEADS, ATT_DIM)).reshape(1, half * KV_WIDTH)

    w2e = (w2[None, :, None, :] * eye[:, None, :, None]).reshape(KV_WIDTH, KV_WIDTH).astype(MXU_DT)
    return expand_pos(pos[:half]), expand_pos(pos[half:]), expand(w1r[:half]), expand(w1r[half:]), w2e


def kernel(x, c, positions, norm1_g, norm2_g, ada_w, ada_b, w_in, cmp_pos, cmp_w1, cmp_w2, att_norm_g,
           m_conv_w, m_conv_b, m_gate_b, m_norm_g, w_out, ffn_up, ffn_conv_w, ffn_conv_b, ffn_down, final_g):
    b, s, d = x.shape
    depth = ada_w.shape[0]
    d_ff = ffn_down.shape[1]
    assert s % (CMP_STRIDE * SUBLANES) == 0 and d % LANES == 0 and b <= SUBLANES
    tm_proj = _pick_tile(s, 512)
    tm_ffn = _pick_tile(s, 512)
    tq = _pick_tile(s, 256)
    chunk = _pick_tile(s, 256)
    tf = 256
    assert d_ff % tf == 0 and WINDOW % tq == 0 and s >= WINDOW + tq
    nc = s // CMP_STRIDE
    nsel = s // SEL_BLOCK
    k_top = min(SEL_TOPK, nsel)

    c8 = jnp.zeros((SUBLANES, d), F32).at[:b].set(c)
    mod = _ada_call(c8, ada_w, ada_b.reshape(depth, 1, 6 * d), _pick_tile(6 * d, 1536))

    def mod_vec(l, k):
        return mod[l, :b, k * d:(k + 1) * d].reshape(b, 1, d)

    lane_d = jnp.arange(LANES) % ATT_DIM
    inv_freq = ROPE_THETA ** (-(2 * (lane_d % ROPE_HALF)).astype(F32) / ROPE_DIM)
    rc, rs1, rs2 = _rope_call(positions.reshape(b, s, 1), inv_freq.reshape(1, LANES), _pick_tile(s, 1024))

    c_start = jnp.arange(nc)[None, :] * CMP_STRIDE
    s_start = jnp.arange(nsel)[:, None] * SEL_BLOCK
    ovt = ((c_start <= s_start + SEL_BLOCK - 1) & (c_start + CMP_BLOCK - 1 >= s_start)
           & (jnp.arange(nc)[None, :] < nc - 1)).astype(MXU_DT)

    for l in range(depth):
        outs = _proj_call(x, mod_vec(l, 1), mod_vec(l, 0), norm1_g[l].reshape(1, d), _pack_w_in(w_in[l]),
                          rc, rs1, rs2, tm_proj)
        q, kc, vc, ks, vst, kw, vwt, mqk, mv, mo, small, smallt = outs
        kcmp, _ = _compress_call(kc.reshape(b, nc, CMP_STRIDE * KV_WIDTH),
                                 *_expand_cmp_weights(cmp_pos[l, 0], cmp_w1[l, 0], cmp_w2[l, 0]))
        _, vcmpt = _compress_call(vc.reshape(b, nc, CMP_STRIDE * KV_WIDTH),
                                  *_expand_cmp_weights(cmp_pos[l, 1], cmp_w1[l, 1], cmp_w2[l, 1]))
        ocmp, bias = _cmp_call(q, kcmp, vcmpt, ovt, tq, k_top)
        gn = jnp.broadcast_to(att_norm_g[l][:, None], (ATT_WIDTH, tq))
        attt = _slc_call(q, ks, vst, kw, vwt, bias, ocmp, smallt, gn, tq)
        gbr = jnp.zeros((1, LANES), F32).at[0, SMALL_I:SMALL_I + 2 * M_HEADS].set(m_gate_b[l])
        hm = _mlstm_call(mqk, mv, mo, small, smallt, m_conv_w[l], m_conv_b[l].reshape(1, -1),
                         gbr, gbr.reshape(LANES, 1), m_norm_g[l].reshape(1, -1), chunk)
        x = _mix_ffn_call(x, attt, hm, w_out[l].astype(MXU_DT), mod_vec(l, 2), mod_vec(l, 4), mod_vec(l, 3),
                          mod_vec(l, 5), norm2_g[l].reshape(1, d), ffn_up[l].astype(MXU_DT), ffn_conv_w[l],
                          ffn_conv_b[l].reshape(1, -1), ffn_down[l].astype(MXU_DT), final_g.reshape(1, d),
                          tm_ffn, tf, l == depth - 1)
    return x
```

```python
import functools

import jax
import jax.numpy as jnp
from jax import lax
from jax.experimental import pallas as pl
from jax.experimental.pallas import tpu as pltpu

F32 = jnp.float32
BF16 = jnp.bfloat16
MXU_DT = jnp.bfloat16

ATT_DIM = 64
ATT_HEADS = 8
ATT_KV_HEADS = 2
ATT_REP = ATT_HEADS // ATT_KV_HEADS
ATT_WIDTH = ATT_HEADS * ATT_DIM
KV_WIDTH = ATT_KV_HEADS * ATT_DIM
ROPE_DIM = ATT_DIM // 4
ROPE_HALF = ROPE_DIM // 2
ROPE_THETA = 500000.0
CMP_BLOCK = 32
CMP_STRIDE = 16
SEL_BLOCK = 64
SEL_TOPK = 16
WINDOW = 512
FORCE_BONUS = 1.0e4
M_HEADS = 4
M_DIM = 128
M_WIDTH = M_HEADS * M_DIM
M_CONV = 4
FFN_CONV = 3
RMS_EPS = 1e-6
NEG = -1e30
Q_SCALE = ATT_DIM ** -0.5 * 1.4426950408889634
BIAS_GROUP = 8

LANES = 128
SUBLANES = 8
VMEM_LIMIT = 56 * 1024 * 1024

COL_Q = 0
COL_KC = 512
COL_VC = 640
COL_KS = 768
COL_VS = 896
COL_KW = 1024
COL_VW = 1152
COL_MQK = 1280
COL_MV = 2304
COL_MO = 2816
COL_SMALL = 3328
IN_COLS_PACKED = 3456
SMALL_I = 3 * ATT_HEADS
SMALL_F = SMALL_I + M_HEADS

_NT = (((1,), (1,)), ((), ()))
_TN = (((0,), (0,)), ((), ()))


def _params(n_axes):
    return pltpu.CompilerParams(dimension_semantics=("arbitrary",) * n_axes,
                                vmem_limit_bytes=VMEM_LIMIT)


def _dot(a, b):
    return jnp.dot(a.astype(MXU_DT), b.astype(MXU_DT), preferred_element_type=F32)


def _dot_nt(a, b):
    return lax.dot_general(a.astype(MXU_DT), b.astype(MXU_DT), _NT, preferred_element_type=F32)


def _dot_tn(a, b):
    return lax.dot_general(a.astype(MXU_DT), b.astype(MXU_DT), _TN, preferred_element_type=F32)


def _sigmoid(x):
    return 1.0 / (1.0 + jnp.exp(-x))


def _silu(x):
    return x * _sigmoid(x)


def _shift_rows(x, prev8, k):
    rolled = pltpu.roll(x, k, 0)
    fix = pltpu.roll(prev8, k, 0)
    row = lax.broadcasted_iota(jnp.int32, fix.shape, 0)
    top = jnp.where(row < k, fix, rolled[:SUBLANES])
    return jnp.concatenate([top, rolled[SUBLANES:]], axis=0)


def _ada_body(c_ref, w_ref, b_ref, o_ref):
    c = c_ref[...]
    o_ref[0] = _dot(_silu(c), w_ref[0]) + b_ref[0]


def _ada_call(c8, ada_w, ada_b3, tn):
    depth, d, n = ada_w.shape
    return pl.pallas_call(
        _ada_body,
        grid=(depth, n // tn),
        in_specs=[pl.BlockSpec((SUBLANES, d), lambda l, j: (0, 0)),
                  pl.BlockSpec((1, d, tn), lambda l, j: (l, 0, j)),
                  pl.BlockSpec((1, 1, tn), lambda l, j: (l, 0, j))],
        out_specs=pl.BlockSpec((1, SUBLANES, tn), lambda l, j: (l, 0, j)),
        out_shape=jax.ShapeDtypeStruct((depth, SUBLANES, n), F32),
        compiler_params=_params(2),
        name="ada",
    )(c8, ada_w, ada_b3)


def _rope_body(pos_ref, freq_ref, rc_ref, rs1_ref, rs2_ref):
    pos = pos_ref[0].astype(F32)
    ang = pos * freq_ref[...]
    d = lax.broadcasted_iota(jnp.int32, ang.shape, 1) % ATT_DIM
    cos = jnp.cos(ang)
    sin = jnp.sin(ang)
    rc_ref[0] = jnp.where(d < ROPE_DIM, cos, 1.0)
    rs1_ref[0] = jnp.where(d < ROPE_HALF, -sin, 0.0)
    rs2_ref[0] = jnp.where((d >= ROPE_HALF) & (d < ROPE_DIM), sin, 0.0)


def _rope_call(pos3, freq, tm):
    b, s, _ = pos3.shape
    spec = pl.BlockSpec((1, tm, LANES), lambda i, j: (i, j, 0))
    shp = jax.ShapeDtypeStruct((b, s, LANES), F32)
    return pl.pallas_call(
        _rope_body,
        grid=(b, s // tm),
        in_specs=[pl.BlockSpec((1, tm, 1), lambda i, j: (i, j, 0)),
                  pl.BlockSpec((1, LANES), lambda i, j: (0, 0))],
        out_specs=[spec, spec, spec],
        out_shape=[shp, shp, shp],
        compiler_params=_params(2),
        name="rope",
    )(pos3, freq)


def _proj_body(x_ref, sc_ref, sh_ref, g_ref, w_ref, rc_ref, rs1_ref, rs2_ref,
               q_ref, kc_ref, vc_ref, ks_ref, vst_ref, kw_ref, vwt_ref,
               mqk_ref, mv_ref, mo_ref, sm_ref, smt_ref):
    x = x_ref[0]
    var = jnp.mean(x * x, axis=-1, keepdims=True)
    h = (x * lax.rsqrt(var + RMS_EPS)) * g_ref[...]
    h = h * (1.0 + sc_ref[0]) + sh_ref[0]
    hb = h.astype(MXU_DT)

    def mm(c0, n):
        return jnp.dot(hb, w_ref[:, c0:c0 + n], preferred_element_type=F32)

    rc, rs1, rs2 = rc_ref[0], rs1_ref[0], rs2_ref[0]

    def rope(y):
        return (y * rc + pltpu.roll(y, LANES - ROPE_HALF, 1) * rs1
                + pltpu.roll(y, ROPE_HALF, 1) * rs2)

    for j in range(ATT_HEADS // 2):
        yt = (rope(mm(COL_Q + LANES * j, LANES)) * Q_SCALE).T
        q_ref[0, 2 * j] = yt[:ATT_DIM].astype(q_ref.dtype)
        q_ref[0, 2 * j + 1] = yt[ATT_DIM:].astype(q_ref.dtype)
    kc_ref[0] = rope(mm(COL_KC, KV_WIDTH))
    vc_ref[0] = mm(COL_VC, KV_WIDTH)
    y = rope(mm(COL_KS, KV_WIDTH))
    lane = lax.broadcasted_iota(jnp.int32, y.shape, 1)
    t = pl.program_id(1) * y.shape[0] + lax.broadcasted_iota(jnp.int32, y.shape, 0)
    onehot = jnp.where(lane - ATT_DIM == (t // SEL_BLOCK) % BIAS_GROUP, 1.0, 0.0)
    ks_ref[0, 0] = jnp.where(lane < ATT_DIM, y, onehot).astype(ks_ref.dtype)
    ks_ref[0, 1] = jnp.where(lane < ATT_DIM, pltpu.roll(y, ATT_DIM, 1), onehot).astype(ks_ref.dtype)
    vst_ref[0] = mm(COL_VS, KV_WIDTH).T.astype(vst_ref.dtype)
    y = rope(mm(COL_KW, KV_WIDTH))
    kw_ref[0, 0] = y[:, :ATT_DIM].astype(kw_ref.dtype)
    kw_ref[0, 1] = y[:, ATT_DIM:].astype(kw_ref.dtype)
    vwt_ref[0] = mm(COL_VW, KV_WIDTH).T.astype(vwt_ref.dtype)
    for j in range(2 * M_WIDTH // 256):
        mqk_ref[0, :, 256 * j:256 * (j + 1)] = mm(COL_MQK + 256 * j, 256).astype(mqk_ref.dtype)
    for j in range(M_WIDTH // 256):
        mv_ref[0, :, 256 * j:256 * (j + 1)] = mm(COL_MV + 256 * j, 256).astype(mv_ref.dtype)
        mo_ref[0, :, 256 * j:256 * (j + 1)] = mm(COL_MO + 256 * j, 256).astype(mo_ref.dtype)
    sm = mm(COL_SMALL, LANES)
    sm_ref[0] = sm
    smt_ref[0] = sm.T


def _proj_call(x, sc, sh, g, w, rc, rs1, rs2, tm):
    b, s, d = x.shape
    row = lambda n: pl.BlockSpec((1, tm, n), lambda i, j: (i, j, 0))
    vec = pl.BlockSpec((1, 1, d), lambda i, j: (i, 0, 0))
    headed = lambda n, w: pl.BlockSpec((1, n, tm, w), lambda i, j: (i, 0, j, 0))
    tposed = pl.BlockSpec((1, LANES, tm), lambda i, j: (i, 0, j))
    sds = jax.ShapeDtypeStruct
    return pl.pallas_call(
        _proj_body,
        grid=(b, s // tm),
        in_specs=[row(d), vec, vec,
                  pl.BlockSpec((1, d), lambda i, j: (0, 0)),
                  pl.BlockSpec((d, IN_COLS_PACKED), lambda i, j: (0, 0)),
                  row(LANES), row(LANES), row(LANES)],
        out_specs=[pl.BlockSpec((1, ATT_HEADS, ATT_DIM, tm), lambda i, j: (i, 0, 0, j)),
                   row(KV_WIDTH), row(KV_WIDTH),
                   headed(ATT_KV_HEADS, LANES), tposed, headed(ATT_KV_HEADS, ATT_DIM), tposed,
                   row(2 * M_WIDTH), row(M_WIDTH), row(M_WIDTH), row(LANES), tposed],
        out_shape=[sds((b, ATT_HEADS, ATT_DIM, s), MXU_DT),
                   sds((b, s, KV_WIDTH), F32), sds((b, s, KV_WIDTH), F32),
                   sds((b, ATT_KV_HEADS, s, LANES), MXU_DT), sds((b, KV_WIDTH, s), MXU_DT),
                   sds((b, ATT_KV_HEADS, s, ATT_DIM), MXU_DT), sds((b, KV_WIDTH, s), MXU_DT),
                   sds((b, s, 2 * M_WIDTH), MXU_DT), sds((b, s, M_WIDTH), MXU_DT),
                   sds((b, s, M_WIDTH), MXU_DT), sds((b, s, LANES), F32), sds((b, LANES, s), F32)],
        compiler_params=_params(2),
        name="proj",
    )(x, sc, sh, g, w, rc, rs1, rs2)


def _compress_body(kr_ref, pa_ref, pb_ref, w1a_ref, w1b_ref, w2_ref, out_ref, outt_ref):
    kr = kr_ref[0]
    nc = kr.shape[0]
    ya = _dot(kr + pa_ref[...], w1a_ref[...])
    yb = _dot(kr + pb_ref[...], w1b_ref[...])
    pre = ya + pltpu.roll(yb, nc - 1, 0)
    out = _dot(_silu(pre), w2_ref[...])
    row = lax.broadcasted_iota(jnp.int32, out.shape, 0)
    out = jnp.where(row < nc - 1, out, 0.0)
    out_ref[0, 0] = out[:, :ATT_DIM].astype(out_ref.dtype)
    out_ref[0, 1] = out[:, ATT_DIM:].astype(out_ref.dtype)
    outt_ref[0] = out.T.astype(outt_ref.dtype)


def _compress_call(kr, pa, pb, w1a, w1b, w2e):
    b, nc, width = kr.shape
    full = lambda a: pl.BlockSpec(a.shape, lambda i: (0,) * a.ndim)
    return pl.pallas_call(
        _compress_body,
        grid=(b,),
        in_specs=[pl.BlockSpec((1, nc, width), lambda i: (i, 0, 0)),
                  full(pa), full(pb), full(w1a), full(w1b), full(w2e)],
        out_specs=[pl.BlockSpec((1, ATT_KV_HEADS, nc, ATT_DIM), lambda i: (i, 0, 0, 0)),
                   pl.BlockSpec((1, KV_WIDTH, nc), lambda i: (i, 0, 0))],
        out_shape=[jax.ShapeDtypeStruct((b, ATT_KV_HEADS, nc, ATT_DIM), MXU_DT),
                   jax.ShapeDtypeStruct((b, KV_WIDTH, nc), MXU_DT)],
        compiler_params=_params(1),
        name="compress",
    )(kr, pa, pb, w1a, w1b, w2e)


def _heads_on_lanes(qt_ref):
    return jnp.concatenate([qt_ref[0, r] for r in range(ATT_REP)], axis=1)


def _tile_heads(a):
    return jnp.concatenate([a] * ATT_REP, axis=1)


def _with_ones_row(vt):
    n = vt.shape[1]
    extra = jnp.where(lax.broadcasted_iota(jnp.int32, (2 * SUBLANES, n), 0) == 0, 1.0, 0.0)
    return jnp.concatenate([vt, extra.astype(vt.dtype)], axis=0)


def _cmp_body(qt_ref, kc_ref, vct_ref, ovt_ref, ocmp_ref, bias_ref, *, tq, k_top):
    t0 = pl.program_id(2) * tq
    kc = kc_ref[0, 0]
    nc = kc.shape[0]
    row = lax.broadcasted_iota(jnp.int32, (nc, tq), 0)
    tpos = t0 + lax.broadcasted_iota(jnp.int32, (nc, tq), 1)
    mbias = jnp.where(row * CMP_STRIDE + (CMP_BLOCK - 1) <= tpos, 0.0, NEG)
    s = jnp.dot(kc, _heads_on_lanes(qt_ref), preferred_element_type=F32) + _tile_heads(mbias)
    m = jnp.max(s, axis=0, keepdims=True)
    m = jnp.where(m > 0.5 * NEG, m, 0.0)
    p = jnp.exp2(s - m).astype(MXU_DT)
    oa = jnp.dot(_with_ones_row(vct_ref[0]), p, preferred_element_type=F32)
    d = oa[ATT_DIM:ATT_DIM + 1]
    inv = 1.0 / jnp.where(d > 0.0, d, 1.0)
    o = oa[:ATT_DIM] * inv
    impa = jnp.dot(ovt_ref[...], p, preferred_element_type=F32) * inv
    imp = impa[:, :tq]
    for r in range(ATT_REP):
        ocmp_ref[0, r * ATT_DIM:(r + 1) * ATT_DIM, :] = o[:, r * tq:(r + 1) * tq]
        if r:
            imp = imp + impa[:, r * tq:(r + 1) * tq]
    nsel = imp.shape[0]
    jidx = lax.broadcasted_iota(jnp.int32, (nsel, tq), 0)
    tq_pos = t0 + lax.broadcasted_iota(jnp.int32, (nsel, tq), 1)
    jcur = tq_pos // SEL_BLOCK
    forced = (jidx == 0) | (jidx == jcur) | (jidx == jcur - 1)
    score = jnp.where(jidx * SEL_BLOCK <= tq_pos, imp + jnp.where(forced, FORCE_BONUS, 0.0), NEG)
    valid = score > 0.5 * NEG
    for _ in range(k_top):
        cur = jnp.max(score, axis=0, keepdims=True)
        first = jnp.min(jnp.where(score == cur, jidx, nsel), axis=0, keepdims=True)
        score = jnp.where(jidx == first, NEG, score)
    bias_ref[0, 0, 0] = jnp.where(valid & (score < 0.5 * NEG), 0.0, NEG)


def _cmp_call(qt, kcmp, vcmpt, ovt, tq, k_top):
    b, _, _, s = qt.shape
    nc = kcmp.shape[2]
    nsel = ovt.shape[0]
    nq = s // tq
    return pl.pallas_call(
        functools.partial(_cmp_body, tq=tq, k_top=k_top),
        grid=(b, ATT_KV_HEADS, nq),
        in_specs=[pl.BlockSpec((1, ATT_REP, ATT_DIM, tq), lambda i, g, j: (i, g, 0, j)),
                  pl.BlockSpec((1, 1, nc, ATT_DIM), lambda i, g, j: (i, g, 0, 0)),
                  pl.BlockSpec((1, ATT_DIM, nc), lambda i, g, j: (i, g, 0)),
                  pl.BlockSpec((nsel, nc), lambda i, g, j: (0, 0))],
        out_specs=[pl.BlockSpec((1, ATT_REP * ATT_DIM, tq), lambda i, g, j: (i, g, j)),
                   pl.BlockSpec((1, 1, 1, nsel, tq), lambda i, g, j: (i, g, j, 0, 0))],
        out_shape=[jax.ShapeDtypeStruct((b, ATT_WIDTH, s), F32),
                   jax.ShapeDtypeStruct((b, ATT_KV_HEADS, nq, nsel, tq), F32)],
        compiler_params=_params(3),
        name="cmp",
    )(qt, kcmp, vcmpt, ovt)


def _slc_body(qt_ref, ks_ref, vst_ref, kw_ref, vwt_ref, bias_ref, ocmp_ref, gt_ref, gn_ref,
              o_ref, qa_scr, m_scr, acc_scr, *, tq):
    g = pl.program_id(1)
    qi = pl.program_id(2)
    t0 = qi * tq
    blocks_per_tile = tq // SEL_BLOCK
    qa_scr[:ATT_DIM, :] = _heads_on_lanes(qt_ref)
    qa_scr[ATT_DIM:, :] = jnp.zeros((LANES - ATT_DIM, qa_scr.shape[1]), qa_scr.dtype)
    m_scr[...] = jnp.full(m_scr.shape, NEG, F32)
    acc_scr[...] = jnp.zeros(acc_scr.shape, F32)

    def tile(kv, diag):
        k0 = pl.multiple_of(kv * tq, tq)
        grp = pl.multiple_of((kv * blocks_per_tile) // BIAS_GROUP * BIAS_GROUP, BIAS_GROUP)
        b8 = bias_ref[0, 0, 0, pl.ds(grp, BIAS_GROUP), :]
        b16 = jnp.concatenate([b8, jnp.zeros_like(b8)], axis=0)
        qa_scr[ATT_DIM:ATT_DIM + 2 * SUBLANES, :] = _tile_heads(b16).astype(qa_scr.dtype)
        s = jnp.dot(ks_ref[0, 0, pl.ds(k0, tq), :], qa_scr[...], preferred_element_type=F32)
        if diag:
            row = lax.broadcasted_iota(jnp.int32, (tq, tq), 0)
            lane = lax.broadcasted_iota(jnp.int32, (tq, tq), 1)
            s = s + _tile_heads(jnp.where(row <= lane, 0.0, NEG))
        m_old = m_scr[...]
        m_new = jnp.maximum(m_old, jnp.max(s, axis=0, keepdims=True))
        alpha = jnp.exp2(m_old - m_new)
        p = jnp.exp2(s - m_new).astype(MXU_DT)
        vt = _with_ones_row(vst_ref[0, :, pl.ds(k0, tq)])
        acc_scr[...] = alpha * acc_scr[...] + jnp.dot(vt, p, preferred_element_type=F32)
        m_scr[...] = m_new

    def loop_body(kv, carry):
        tile(kv, False)
        return carry

    lax.fori_loop(0, qi, loop_body, 0)
    tile(qi, True)

    wk = WINDOW + tq
    s0 = pl.multiple_of(jnp.maximum(t0 - WINDOW, 0), tq)
    kpos = s0 + lax.broadcasted_iota(jnp.int32, (wk, tq), 0)
    tpos = t0 + lax.broadcasted_iota(jnp.int32, (wk, tq), 1)
    wbias = jnp.where((kpos <= tpos) & (kpos > tpos - WINDOW), 0.0, NEG)
    sw = jnp.dot(kw_ref[0, 0, pl.ds(s0, wk), :], qa_scr[:ATT_DIM, :], preferred_element_type=F32)
    sw = sw + _tile_heads(wbias)
    pw = jnp.exp2(sw - jnp.max(sw, axis=0, keepdims=True)).astype(MXU_DT)
    ow = jnp.dot(_with_ones_row(vwt_ref[0, :, pl.ds(s0, wk)]), pw, preferred_element_type=F32)
    o_win_all = ow[:ATT_DIM] / ow[ATT_DIM:ATT_DIM + 1]
    o_slc_all = acc_scr[:ATT_DIM, :] / acc_scr[ATT_DIM:ATT_DIM + 1, :]
    for r in range(ATT_REP):
        o_win = o_win_all[:, r * tq:(r + 1) * tq]
        o_slc = o_slc_all[:, r * tq:(r + 1) * tq]
        o_cmp = ocmp_ref[0, r * ATT_DIM:(r + 1) * ATT_DIM, :]
        gbase = (g * ATT_REP + r) * 3
        g_cmp = _sigmoid(gt_ref[0, pl.ds(gbase, 1), :])
        g_slc = _sigmoid(gt_ref[0, pl.ds(gbase + 1, 1), :])
        g_win = _sigmoid(gt_ref[0, pl.ds(gbase + 2, 1), :])
        o = g_cmp * o_cmp + g_slc * o_slc + g_win * o_win
        o = o * lax.rsqrt(jnp.mean(o * o, axis=0, keepdims=True) + RMS_EPS)
        o = o * gn_ref[r * ATT_DIM:(r + 1) * ATT_DIM, :]
        o_ref[0, r * ATT_DIM:(r + 1) * ATT_DIM, :] = o.astype(o_ref.dtype)


def _slc_call(qt, ks, vst, kw, vwt, bias, ocmp, smallt, gn, tq):
    b, _, _, s = qt.shape
    nsel = bias.shape[3]
    gw = ATT_REP * ATT_DIM
    vres = pl.BlockSpec((1, ATT_DIM, s), lambda i, g, j: (i, g, 0))
    return pl.pallas_call(
        functools.partial(_slc_body, tq=tq),
        grid=(b, ATT_KV_HEADS, s // tq),
        in_specs=[pl.BlockSpec((1, ATT_REP, ATT_DIM, tq), lambda i, g, j: (i, g, 0, j)),
                  pl.BlockSpec((1, 1, s, LANES), lambda i, g, j: (i, g, 0, 0)), vres,
                  pl.BlockSpec((1, 1, s, ATT_DIM), lambda i, g, j: (i, g, 0, 0)), vres,
                  pl.BlockSpec((1, 1, 1, nsel, tq), lambda i, g, j: (i, g, j, 0, 0)),
                  pl.BlockSpec((1, gw, tq), lambda i, g, j: (i, g, j)),
                  pl.BlockSpec((1, LANES, tq), lambda i, g, j: (i, 0, j)),
                  pl.BlockSpec((gw, tq), lambda i, g, j: (g, 0))],
        out_specs=pl.BlockSpec((1, gw, tq), lambda i, g, j: (i, g, j)),
        out_shape=jax.ShapeDtypeStruct((b, ATT_WIDTH, s), MXU_DT),
        scratch_shapes=[pltpu.VMEM((LANES, ATT_REP * tq), MXU_DT),
                        pltpu.VMEM((1, ATT_REP * tq), F32),
                        pltpu.VMEM((ATT_DIM + 2 * SUBLANES, ATT_REP * tq), F32)],
        compiler_params=_params(3),
        name="slc",
    )(qt, ks, vst, kw, vwt, bias, ocmp, smallt, gn)


def _mlstm_body(mqk_ref, mv_ref, mo_ref, sm_ref, smt_ref, cw_ref, cb_ref, gbr_ref, gbc_ref, ng_ref,
                o_ref, c_scr, m_scr, tail_scr, *, chunk):
    @pl.when(pl.program_id(1) == 0)
    def _():
        c_scr[...] = jnp.zeros(c_scr.shape, F32)
        m_scr[...] = jnp.full(m_scr.shape, NEG, F32)
        tail_scr[...] = jnp.zeros(tail_scr.shape, F32)

    x = mqk_ref[0].astype(F32)
    tail = tail_scr[...]
    y = cb_ref[...] + x * cw_ref[M_CONV - 1:M_CONV, :]
    for k in range(1, M_CONV):
        y = y + _shift_rows(x, tail, k) * cw_ref[M_CONV - 1 - k:M_CONV - k, :]
    tail_scr[...] = x[chunk - SUBLANES:, :]
    qk = _silu(y)
    sm = sm_ref[0] + gbr_ref[...]
    smt = smt_ref[0] + gbc_ref[...]
    row = lax.broadcasted_iota(jnp.int32, (chunk, chunk), 0)
    col = lax.broadcasted_iota(jnp.int32, (chunk, chunk), 1)
    causal = row >= col
    ones = jnp.ones((chunk, M_DIM), F32)

    def log_sigmoid(z):
        return jnp.minimum(z, 0.0) - jnp.log(1.0 + jnp.exp(-jnp.abs(z)))

    for h in range(M_HEADS):
        q = qk[:, h * M_DIM:(h + 1) * M_DIM]
        k = qk[:, M_WIDTH + h * M_DIM:M_WIDTH + (h + 1) * M_DIM] * (M_DIM ** -0.5)
        v = mv_ref[0, :, h * M_DIM:(h + 1) * M_DIM]
        i_col = sm[:, SMALL_I + h:SMALL_I + h + 1]
        lf_col = log_sigmoid(sm[:, SMALL_F + h:SMALL_F + h + 1])
        i_row = smt[SMALL_I + h:SMALL_I + h + 1, :]
        lf_row = log_sigmoid(smt[SMALL_F + h:SMALL_F + h + 1, :])
        b_col = jnp.sum(jnp.where(causal, lf_row, 0.0), axis=1, keepdims=True)
        b_row = jnp.sum(jnp.where(row <= col, lf_col, 0.0), axis=0, keepdims=True)
        m_prev = m_scr[h]
        dmat = jnp.where(causal, b_col - b_row + i_row, NEG)
        inter = b_col + m_prev
        mt = jnp.maximum(inter, jnp.max(dmat, axis=1, keepdims=True))
        w_intra = jnp.exp(dmat - mt)
        w_inter = jnp.exp(inter - mt)
        qkw = _dot_nt(q, k) * w_intra
        c_aug = c_scr[h]
        r1 = _dot(q, c_aug)
        num = w_inter * r1[:, :M_DIM] + _dot(qkw, v)
        den = w_inter * r1[:, M_DIM:M_DIM + 1] + jnp.sum(qkw, axis=1, keepdims=True)
        hh = num / jnp.maximum(jnp.abs(den), jnp.exp(-mt))
        b_last = b_col[chunk - 1:chunk, :]
        a_col = b_last - b_col + i_col
        m_new = jnp.maximum(b_last + m_prev, jnp.max(a_col, axis=0, keepdims=True))
        ws = jnp.exp(a_col - m_new)
        decay = jnp.exp(b_last + m_prev - m_new)
        v_aug = jnp.concatenate([v.astype(F32), ones], axis=1)
        c_scr[h] = decay * c_aug + _dot_tn(k * ws, v_aug)
        m_scr[h] = m_new
        hm = _sigmoid(mo_ref[0, :, h * M_DIM:(h + 1) * M_DIM].astype(F32)) * hh
        hm = hm * lax.rsqrt(jnp.mean(hm * hm, axis=1, keepdims=True) + RMS_EPS)
        o_ref[0, :, h * M_DIM:(h + 1) * M_DIM] = (hm * ng_ref[:, h * M_DIM:(h + 1) * M_DIM]).astype(o_ref.dtype)


def _mlstm_call(mqk, mv, mo, small, smallt, cw, cb, gbr, gbc, ng, chunk):
    b, s, _ = mv.shape
    row = lambda n: pl.BlockSpec((1, chunk, n), lambda i, j: (i, j, 0))
    full = lambda a: pl.BlockSpec(a.shape, lambda i, j: (0,) * a.ndim)
    return pl.pallas_call(
        functools.partial(_mlstm_body, chunk=chunk),
        grid=(b, s // chunk),
        in_specs=[row(2 * M_WIDTH), row(M_WIDTH), row(M_WIDTH), row(LANES),
                  pl.BlockSpec((1, LANES, chunk), lambda i, j: (i, 0, j)),
                  full(cw), full(cb), full(gbr), full(gbc), full(ng)],
        out_specs=row(M_WIDTH),
        out_shape=jax.ShapeDtypeStruct((b, s, M_WIDTH), MXU_DT),
        scratch_shapes=[pltpu.VMEM((M_HEADS, M_DIM, 2 * M_DIM), F32),
                        pltpu.VMEM((M_HEADS, 1, 1), F32),
                        pltpu.VMEM((SUBLANES, 2 * M_WIDTH), F32)],
        compiler_params=_params(2),
        name="mlstm",
    )(mqk, mv, mo, small, smallt, cw, cb, gbr, gbc, ng)


def _mix_ffn_body(x_ref, attt_ref, hm_ref, wo_ref, g1_ref, sc_ref, sh_ref, g2_ref, ng_ref,
                  wup_ref, cw_ref, cb_ref, wdn_ref, fg_ref, o_ref, tail_scr, *, d_ff, tf, tiles_per_seq, final):
    @pl.when(pl.program_id(0) % tiles_per_seq == 0)
    def _():
        tail_scr[...] = jnp.zeros(tail_scr.shape, F32)

    d = x_ref.shape[-1]
    mix = _dot_tn(attt_ref[0], wo_ref[:ATT_WIDTH, :]) + _dot(hm_ref[0], wo_ref[ATT_WIDTH:, :])
    x = x_ref[0] + g1_ref[0] * mix
    var = jnp.mean(x * x, axis=-1, keepdims=True)
    h = (x * lax.rsqrt(var + RMS_EPS)) * ng_ref[...]
    hb = (h * (1.0 + sc_ref[0]) + sh_ref[0]).astype(MXU_DT)
    acc = jnp.zeros((x.shape[0], d), F32)
    for c in range(d_ff // tf):
        a = jnp.dot(hb, wup_ref[:, c * tf:(c + 1) * tf], preferred_element_type=F32)
        v = jnp.dot(hb, wup_ref[:, d_ff + c * tf:d_ff + (c + 1) * tf], preferred_element_type=F32)
        tail = tail_scr[c]
        y = cb_ref[:, c * tf:(c + 1) * tf] + a * cw_ref[FFN_CONV - 1:FFN_CONV, c * tf:(c + 1) * tf]
        for k in range(1, FFN_CONV):
            y = y + _shift_rows(a, tail, k) * cw_ref[FFN_CONV - 1 - k:FFN_CONV - k, c * tf:(c + 1) * tf]
        tail_scr[c] = a[a.shape[0] - SUBLANES:, :]
        acc = acc + _dot(_silu(y) * v, wdn_ref[c * tf:(c + 1) * tf, :])
    out = x + g2_ref[0] * acc
    if final:
        var = jnp.mean(out * out, axis=-1, keepdims=True)
        out = (out * lax.rsqrt(var + RMS_EPS)) * fg_ref[...]
    o_ref[0] = out


def _mix_ffn_call(x, attt, hm, wo, g1, sc, sh, g2, ng, wup, cw, cb, wdn, fg, tm, tf, final):
    b, s, d = x.shape
    d_ff = wdn.shape[0]
    tiles_per_seq = s // tm
    row = lambda n: pl.BlockSpec((1, tm, n), lambda i: (i // tiles_per_seq, i % tiles_per_seq, 0))
    vec = pl.BlockSpec((1, 1, d), lambda i: (i // tiles_per_seq, 0, 0))
    full = lambda a: pl.BlockSpec(a.shape, lambda i: (0,) * a.ndim)
    return pl.pallas_call(
        functools.partial(_mix_ffn_body, d_ff=d_ff, tf=tf, tiles_per_seq=tiles_per_seq, final=final),
        grid=(b * tiles_per_seq,),
        in_specs=[row(d),
                  pl.BlockSpec((1, ATT_WIDTH, tm), lambda i: (i // tiles_per_seq, 0, i % tiles_per_seq)),
                  row(M_WIDTH), full(wo), vec, vec, vec, vec, full(ng),
                  full(wup), full(cw), full(cb), full(wdn), full(fg)],
        out_specs=row(d),
        out_shape=jax.ShapeDtypeStruct((b, s, d), F32),
        scratch_shapes=[pltpu.VMEM((d_ff // tf, SUBLANES, tf), F32)],
        compiler_params=_params(1),
        name="mix_ffn",
    )(x, attt, hm, wo, g1, sc, sh, g2, ng, wup, cw, cb, wdn, fg)


def _pick_tile(n, pref):
    t = min(n, pref)
    assert n % t == 0, (n, t)
    return t


def _pack_w_in(w_in):
    att_cols = ATT_WIDTH + 6 * KV_WIDTH
    n_gate = 3 * ATT_HEADS
    o = att_cols
    gates = w_in[:, o:o + n_gate]; o += n_gate
    mqk = w_in[:, o:o + 2 * M_WIDTH]; o += 2 * M_WIDTH
    mv = w_in[:, o:o + M_WIDTH]; o += M_WIDTH
    mif = w_in[:, o:o + 2 * M_HEADS]; o += 2 * M_HEADS
    mo = w_in[:, o:o + M_WIDTH]
    pad = jnp.zeros((w_in.shape[0], LANES - n_gate - 2 * M_HEADS), w_in.dtype)
    return jnp.concatenate([w_in[:, :att_cols], mqk, mv, mo, gates, mif, pad], axis=1).astype(MXU_DT)


def _expand_cmp_weights(pos, w1, w2):
    half = CMP_BLOCK // 2
    w1r = w1.reshape(CMP_BLOCK, ATT_DIM, ATT_DIM)
    eye = jnp.eye(ATT_KV_HEADS, dtype=w1.dtype)

    def expand(wl):
        z = wl[:, None, :, None, :] * eye[None, :, None, :, None]
        return z.reshape(half * KV_WIDTH, KV_WIDTH).astype(MXU_DT)

    def expand_pos(pl_):
        return jnp.broadcast_to(pl_[:, None, :], (half, ATT_KV_HEADS, ATT_DIM)).reshape(1, half * KV_WIDTH)

    w2e = (w2[None, :, None, :] * eye[:, None, :, None]).reshape(KV_WIDTH, KV_WIDTH).astype(MXU_DT)
    return expand_pos(pos[:half]), expand_pos(pos[half:]), expand(w1r[:half]), expand(w1r[half:]), w2e


def kernel(x, c, positions, norm1_g, norm2_g, ada_w, ada_b, w_in, cmp_pos, cmp_w1, cmp_w2, att_norm_g,
           m_conv_w, m_conv_b, m_gate_b, m_norm_g, w_out, ffn_up, ffn_conv_w, ffn_conv_b, ffn_down, final_g):
    b, s, d = x.shape
    depth = ada_w.shape[0]
    d_ff = ffn_down.shape[1]
    assert s % (CMP_STRIDE * SUBLANES) == 0 and d % LANES == 0 and b <= SUBLANES
    tm_proj = _pick_tile(s, 512)
    tm_ffn = _pick_tile(s, 512)
    tq = _pick_tile(s, 256)
    chunk = _pick_tile(s, 256)
    tf = 256
    assert d_ff % tf == 0 and WINDOW % tq == 0 and s >= WINDOW + tq
    nc = s // CMP_STRIDE
    nsel = s // SEL_BLOCK
    k_top = min(SEL_TOPK, nsel)

    c8 = jnp.zeros((SUBLANES, d), F32).at[:b].set(c)
    mod = _ada_call(c8, ada_w, ada_b.reshape(depth, 1, 6 * d), _pick_tile(6 * d, 1536))

    def mod_vec(l, k):
        return mod[l, :b, k * d:(k + 1) * d].reshape(b, 1, d)

    lane_d = jnp.arange(LANES) % ATT_DIM
    inv_freq = ROPE_THETA ** (-(2 * (lane_d % ROPE_HALF)).astype(F32) / ROPE_DIM)
    rc, rs1, rs2 = _rope_call(positions.reshape(b, s, 1), inv_freq.reshape(1, LANES), _pick_tile(s, 1024))

    c_start = jnp.arange(nc)[None, :] * CMP_STRIDE
    s_start = jnp.arange(nsel)[:, None] * SEL_BLOCK
    ovt = ((c_start <= s_start + SEL_BLOCK - 1) & (c_start + CMP_BLOCK - 1 >= s_start)
           & (jnp.arange(nc)[None, :] < nc - 1)).astype(MXU_DT)

    for l in range(depth):
        outs = _proj_call(x, mod_vec(l, 1), mod_vec(l, 0), norm1_g[l].reshape(1, d), _pack_w_in(w_in[l]),
                          rc, rs1, rs2, tm_proj)
        q, kc, vc, ks, vst, kw, vwt, mqk, mv, mo, small, smallt = outs
        kcmp, _ = _compress_call(kc.reshape(b, nc, CMP_STRIDE * KV_WIDTH),
                                 *_expand_cmp_weights(cmp_pos[l, 0], cmp_w1[l, 0], cmp_w2[l, 0]))
        _, vcmpt = _compress_call(vc.reshape(b, nc, CMP_STRIDE * KV_WIDTH),
                                  *_expand_cmp_weights(cmp_pos[l, 1], cmp_w1[l, 1], cmp_w2[l, 1]))
        ocmp, bias = _cmp_call(q, kcmp, vcmpt, ovt, tq, k_top)
        gn = jnp.broadcast_to(att_norm_g[l][:, None], (ATT_WIDTH, tq))
        attt = _slc_call(q, ks, vst, kw, vwt, bias, ocmp, smallt, gn, tq)
        gbr = jnp.zeros((1, LANES), F32).at[0, SMALL_I:SMALL_I + 2 * M_HEADS].set(m_gate_b[l])
        hm = _mlstm_call(mqk, mv, mo, small, smallt, m_conv_w[l], m_conv_b[l].reshape(1, -1),
                         gbr, gbr.reshape(LANES, 1), m_norm_g[l].reshape(1, -1), chunk)
        x = _mix_ffn_call(x, attt, hm, w_out[l].astype(MXU_DT), mod_vec(l, 2), mod_vec(l, 4), mod_vec(l, 3),
                          mod_vec(l, 5), norm2_g[l].reshape(1, d), ffn_up[l].astype(MXU_DT), ffn_conv_w[l],
                          ffn_conv_b[l].reshape(1, -1), ffn_down[l].astype(MXU_DT), final_g.reshape(1, d),
                          tm_ffn, tf, l == depth - 1)
    return x
```

```python
import functools

import jax
import jax.numpy as jnp
from jax import lax
from jax.experimental import pallas as pl
from jax.experimental.pallas import tpu as pltpu

F32 = jnp.float32
BF16 = jnp.bfloat16
MXU_DT = jnp.bfloat16

ATT_DIM = 64
ATT_HEADS = 8
ATT_KV_HEADS = 2
ATT_REP = ATT_HEADS // ATT_KV_HEADS
ATT_WIDTH = ATT_HEADS * ATT_DIM
KV_WIDTH = ATT_KV_HEADS * ATT_DIM
ROPE_DIM = ATT_DIM // 4
ROPE_HALF = ROPE_DIM // 2
ROPE_THETA = 500000.0
CMP_BLOCK = 32
CMP_STRIDE = 16
SEL_BLOCK = 64
SEL_TOPK = 16
WINDOW = 512
FORCE_BONUS = 1.0e4
M_HEADS = 4
M_DIM = 128
M_WIDTH = M_HEADS * M_DIM
M_CONV = 4
FFN_CONV = 3
RMS_EPS = 1e-6
NEG = -1e30
Q_SCALE = ATT_DIM ** -0.5 * 1.4426950408889634
BIAS_GROUP = 8

LANES = 128
SUBLANES = 8
VMEM_LIMIT = 56 * 1024 * 1024

COL_Q = 0
COL_KC = 512
COL_VC = 640
COL_KS = 768
COL_VS = 896
COL_KW = 1024
COL_VW = 1152
COL_MQK = 1280
COL_MV = 2304
COL_MO = 2816
COL_SMALL = 3328
IN_COLS_PACKED = 3456
SMALL_I = 3 * ATT_HEADS
SMALL_F = SMALL_I + M_HEADS

_NT = (((1,), (1,)), ((), ()))
_TN = (((0,), (0,)), ((), ()))


def _params(n_axes):
    return pltpu.CompilerParams(dimension_semantics=("arbitrary",) * n_axes,
                                vmem_limit_bytes=VMEM_LIMIT)


def _dot(a, b):
    return jnp.dot(a.astype(MXU_DT), b.astype(MXU_DT), preferred_element_type=F32)


def _dot_nt(a, b):
    return lax.dot_general(a.astype(MXU_DT), b.astype(MXU_DT), _NT, preferred_element_type=F32)


def _dot_tn(a, b):
    return lax.dot_general(a.astype(MXU_DT), b.astype(MXU_DT), _TN, preferred_element_type=F32)


def _sigmoid(x):
    return 1.0 / (1.0 + jnp.exp(-x))


def _silu(x):
    return x * _sigmoid(x)


def _shift_rows(x, prev8, k):
    rolled = pltpu.roll(x, k, 0)
    fix = pltpu.roll(prev8, k, 0)
    row = lax.broadcasted_iota(jnp.int32, fix.shape, 0)
    top = jnp.where(row < k, fix, rolled[:SUBLANES])
    return jnp.concatenate([top, rolled[SUBLANES:]], axis=0)


def _ada_body(c_ref, w_ref, b_ref, o_ref):
    c = c_ref[...]
    o_ref[0] = _dot(_silu(c), w_ref[0]) + b_ref[0]


def _ada_call(c8, ada_w, ada_b3, tn):
    depth, d, n = ada_w.shape
    return pl.pallas_call(
        _ada_body,
        grid=(depth, n // tn),
        in_specs=[pl.BlockSpec((SUBLANES, d), lambda l, j: (0, 0)),
                  pl.BlockSpec((1, d, tn), lambda l, j: (l, 0, j)),
                  pl.BlockSpec((1, 1, tn), lambda l, j: (l, 0, j))],
        out_specs=pl.BlockSpec((1, SUBLANES, tn), lambda l, j: (l, 0, j)),
        out_shape=jax.ShapeDtypeStruct((depth, SUBLANES, n), F32),
        compiler_params=_params(2),
        name="ada",
    )(c8, ada_w, ada_b3)


def _rope_body(pos_ref, freq_ref, rc_ref, rs1_ref, rs2_ref):
    pos = pos_ref[0].astype(F32)
    ang = pos * freq_ref[...]
    d = lax.broadcasted_iota(jnp.int32, ang.shape, 1) % ATT_DIM
    cos = jnp.cos(ang)
    sin = jnp.sin(ang)
    rc_ref[0] = jnp.where(d < ROPE_DIM, cos, 1.0)
    rs1_ref[0] = jnp.where(d < ROPE_HALF, -sin, 0.0)
    rs2_ref[0] = jnp.where((d >= ROPE_HALF) & (d < ROPE_DIM), sin, 0.0)


def _rope_call(pos3, freq, tm):
    b, s, _ = pos3.shape
    spec = pl.BlockSpec((1, tm, LANES), lambda i, j: (i, j, 0))
    shp = jax.ShapeDtypeStruct((b, s, LANES), F32)
    return pl.pallas_call(
        _rope_body,
        grid=(b, s // tm),
        in_specs=[pl.BlockSpec((1, tm, 1), lambda i, j: (i, j, 0)),
                  pl.BlockSpec((1, LANES), lambda i, j: (0, 0))],
        out_specs=[spec, spec, spec],
        out_shape=[shp, shp, shp],
        compiler_params=_params(2),
        name="rope",
    )(pos3, freq)


def _proj_body(x_ref, sc_ref, sh_ref, g_ref, w_ref, rc_ref, rs1_ref, rs2_ref,
               q_ref, kc_ref, vc_ref, ks_ref, vst_ref, kw_ref, vwt_ref,
               mqk_ref, mv_ref, mo_ref, sm_ref, smt_ref):
    x = x_ref[0]
    var = jnp.mean(x * x, axis=-1, keepdims=True)
    h = (x * lax.rsqrt(var + RMS_EPS)) * g_ref[...]
    h = h * (1.0 + sc_ref[0]) + sh_ref[0]
    hb = h.astype(MXU_DT)

    def mm(c0, n):
        return jnp.dot(hb, w_ref[:, c0:c0 + n], preferred_element_type=F32)

    rc, rs1, rs2 = rc_ref[0], rs1_ref[0], rs2_ref[0]

    def rope(y):
        return (y * rc + pltpu.roll(y, LANES - ROPE_HALF, 1) * rs1
                + pltpu.roll(y, ROPE_HALF, 1) * rs2)

    for j in range(ATT_HEADS // 2):
        yt = (rope(mm(COL_Q + LANES * j, LANES)) * Q_SCALE).T
        q_ref[0, 2 * j] = yt[:ATT_DIM].astype(q_ref.dtype)
        q_ref[0, 2 * j + 1] = yt[ATT_DIM:].astype(q_ref.dtype)
    kc_ref[0] = rope(mm(COL_KC, KV_WIDTH))
    vc_ref[0] = mm(COL_VC, KV_WIDTH)
    y = rope(mm(COL_KS, KV_WIDTH))
    lane = lax.broadcasted_iota(jnp.int32, y.shape, 1)
    t = pl.program_id(1) * y.shape[0] + lax.broadcasted_iota(jnp.int32, y.shape, 0)
    onehot = jnp.where(lane - ATT_DIM == (t // SEL_BLOCK) % BIAS_GROUP, 1.0, 0.0)
    ks_ref[0, 0] = jnp.where(lane < ATT_DIM, y, onehot).astype(ks_ref.dtype)
    ks_ref[0, 1] = jnp.where(lane < ATT_DIM, pltpu.roll(y, ATT_DIM, 1), onehot).astype(ks_ref.dtype)
    vst_ref[0] = mm(COL_VS, KV_WIDTH).T.astype(vst_ref.dtype)
    y = rope(mm(COL_KW, KV_WIDTH))
    kw_ref[0, 0] = y[:, :ATT_DIM].astype(kw_ref.dtype)
    kw_ref[0, 1] = y[:, ATT_DIM:].astype(kw_ref.dtype)
    vwt_ref[0] = mm(COL_VW, KV_WIDTH).T.astype(vwt_ref.dtype)
    for j in range(2 * M_WIDTH // 256):
        mqk_ref[0, :, 256 * j:256 * (j + 1)] = mm(COL_MQK + 256 * j, 256).astype(mqk_ref.dtype)
    for j in range(M_WIDTH // 256):
        mv_ref[0, :, 256 * j:256 * (j + 1)] = mm(COL_MV + 256 * j, 256).astype(mv_ref.dtype)
        mo_ref[0, :, 256 * j:256 * (j + 1)] = mm(COL_MO + 256 * j, 256).astype(mo_ref.dtype)
    sm = mm(COL_SMALL, LANES)
    sm_ref[0] = sm
    smt_ref[0] = sm.T


def _proj_call(x, sc, sh, g, w, rc, rs1, rs2, tm):
    b, s, d = x.shape
    row = lambda n: pl.BlockSpec((1, tm, n), lambda i, j: (i, j, 0))
    vec = pl.BlockSpec((1, 1, d), lambda i, j: (i, 0, 0))
    headed = lambda n, w: pl.BlockSpec((1, n, tm, w), lambda i, j: (i, 0, j, 0))
    tposed = pl.BlockSpec((1, LANES, tm), lambda i, j: (i, 0, j))
    sds = jax.ShapeDtypeStruct
    return pl.pallas_call(
        _proj_body,
        grid=(b, s // tm),
        in_specs=[row(d), vec, vec,
                  pl.BlockSpec((1, d), lambda i, j: (0, 0)),
                  pl.BlockSpec((d, IN_COLS_PACKED), lambda i, j: (0, 0)),
                  row(LANES), row(LANES), row(LANES)],
        out_specs=[pl.BlockSpec((1, ATT_HEADS, ATT_DIM, tm), lambda i, j: (i, 0, 0, j)),
                   row(KV_WIDTH), row(KV_WIDTH),
                   headed(ATT_KV_HEADS, LANES), tposed, headed(ATT_KV_HEADS, ATT_DIM), tposed,
                   row(2 * M_WIDTH), row(M_WIDTH), row(M_WIDTH), row(LANES), tposed],
        out_shape=[sds((b, ATT_HEADS, ATT_DIM, s), MXU_DT),
                   sds((b, s, KV_WIDTH), F32), sds((b, s, KV_WIDTH), F32),
                   sds((b, ATT_KV_HEADS, s, LANES), MXU_DT), sds((b, KV_WIDTH, s), MXU_DT),
                   sds((b, ATT_KV_HEADS, s, ATT_DIM), MXU_DT), sds((b, KV_WIDTH, s), MXU_DT),
                   sds((b, s, 2 * M_WIDTH), MXU_DT), sds((b, s, M_WIDTH), MXU_DT),
                   sds((b, s, M_WIDTH), MXU_DT), sds((b, s, LANES), F32), sds((b, LANES, s), F32)],
        compiler_params=_params(2),
        name="proj",
    )(x, sc, sh, g, w, rc, rs1, rs2)


def _compress_body(kr_ref, pa_ref, pb_ref, w1a_ref, w1b_ref, w2_ref, out_ref, outt_ref):
    kr = kr_ref[0]
    nc = kr.shape[0]
    ya = _dot(kr + pa_ref[...], w1a_ref[...])
    yb = _dot(kr + pb_ref[...], w1b_ref[...])
    pre = ya + pltpu.roll(yb, nc - 1, 0)
    out = _dot(_silu(pre), w2_ref[...])
    row = lax.broadcasted_iota(jnp.int32, out.shape, 0)
    out = jnp.where(row < nc - 1, out, 0.0)
    out_ref[0, 0] = out[:, :ATT_DIM].astype(out_ref.dtype)
    out_ref[0, 1] = out[:, ATT_DIM:].astype(out_ref.dtype)
    outt_ref[0] = out.T.astype(outt_ref.dtype)


def _compress_call(kr, pa, pb, w1a, w1b, w2e):
    b, nc, width = kr.shape
    full = lambda a: pl.BlockSpec(a.shape, lambda i: (0,) * a.ndim)
    return pl.pallas_call(
        _compress_body,
        grid=(b,),
        in_specs=[pl.BlockSpec((1, nc, width), lambda i: (i, 0, 0)),
                  full(pa), full(pb), full(w1a), full(w1b), full(w2e)],
        out_specs=[pl.BlockSpec((1, ATT_KV_HEADS, nc, ATT_DIM), lambda i: (i, 0, 0, 0)),
                   pl.BlockSpec((1, KV_WIDTH, nc), lambda i: (i, 0, 0))],
        out_shape=[jax.ShapeDtypeStruct((b, ATT_KV_HEADS, nc, ATT_DIM), MXU_DT),
                   jax.ShapeDtypeStruct((b, KV_WIDTH, nc), MXU_DT)],
        compiler_params=_params(1),
        name="compress",
    )(kr, pa, pb, w1a, w1b, w2e)


def _heads_on_lanes(qt_ref):
    return jnp.concatenate([qt_ref[0, r] for r in range(ATT_REP)], axis=1)


def _tile_heads(a):
    return jnp.concatenate([a] * ATT_REP, axis=1)


def _with_ones_row(vt):
    n = vt.shape[1]
    extra = jnp.where(lax.broadcasted_iota(jnp.int32, (2 * SUBLANES, n), 0) == 0, 1.0, 0.0)
    return jnp.concatenate([vt, extra.astype(vt.dtype)], axis=0)


def _cmp_body(qt_ref, kc_ref, vct_ref, ovt_ref, ocmp_ref, bias_ref, *, tq, k_top):
    t0 = pl.program_id(2) * tq
    kc = kc_ref[0, 0]
    nc = kc.shape[0]
    row = lax.broadcasted_iota(jnp.int32, (nc, tq), 0)
    tpos = t0 + lax.broadcasted_iota(jnp.int32, (nc, tq), 1)
    mbias = jnp.where(row * CMP_STRIDE + (CMP_BLOCK - 1) <= tpos, 0.0, NEG)
    s = jnp.dot(kc, _heads_on_lanes(qt_ref), preferred_element_type=F32) + _tile_heads(mbias)
    m = jnp.max(s, axis=0, keepdims=True)
    m = jnp.where(m > 0.5 * NEG, m, 0.0)
    p = jnp.exp2(s - m).astype(MXU_DT)
    oa = jnp.dot(_with_ones_row(vct_ref[0]), p, preferred_element_type=F32)
    d = oa[ATT_DIM:ATT_DIM + 1]
    inv = 1.0 / jnp.where(d > 0.0, d, 1.0)
    o = oa[:ATT_DIM] * inv
    impa = jnp.dot(ovt_ref[...], p, preferred_element_type=F32) * inv
    imp = impa[:, :tq]
    for r in range(ATT_REP):
        ocmp_ref[0, r * ATT_DIM:(r + 1) * ATT_DIM, :] = o[:, r * tq:(r + 1) * tq]
        if r:
            imp = imp + impa[:, r * tq:(r + 1) * tq]
    nsel = imp.shape[0]
    jidx = lax.broadcasted_iota(jnp.int32, (nsel, tq), 0)
    tq_pos = t0 + lax.broadcasted_iota(jnp.int32, (nsel, tq), 1)
    jcur = tq_pos // SEL_BLOCK
    forced = (jidx == 0) | (jidx == jcur) | (jidx == jcur - 1)
    score = jnp.where(jidx * SEL_BLOCK <= tq_pos, imp + jnp.where(forced, FORCE_BONUS, 0.0), NEG)
    valid = score > 0.5 * NEG
    for _ in range(k_top):
        cur = jnp.max(score, axis=0, keepdims=True)
        first = jnp.min(jnp.where(score == cur, jidx, nsel), axis=0, keepdims=True)
        score = jnp.where(jidx == first, NEG, score)
    bias_ref[0, 0, 0] = jnp.where(valid & (score < 0.5 * NEG), 0.0, NEG)


def _cmp_call(qt, kcmp, vcmpt, ovt, tq, k_top):
    b, _, _, s = qt.shape
    nc = kcmp.shape[2]
    nsel = ovt.shape[0]
    nq = s // tq
    return pl.pallas_call(
        functools.partial(_cmp_body, tq=tq, k_top=k_top),
        grid=(b, ATT_KV_HEADS, nq),
        in_specs=[pl.BlockSpec((1, ATT_REP, ATT_DIM, tq), lambda i, g, j: (i, g, 0, j)),
                  pl.BlockSpec((1, 1, nc, ATT_DIM), lambda i, g, j: (i, g, 0, 0)),
                  pl.BlockSpec((1, ATT_DIM, nc), lambda i, g, j: (i, g, 0)),
                  pl.BlockSpec((nsel, nc), lambda i, g, j: (0, 0))],
        out_specs=[pl.BlockSpec((1, ATT_REP * ATT_DIM, tq), lambda i, g, j: (i, g, j)),
                   pl.BlockSpec((1, 1, 1, nsel, tq), lambda i, g, j: (i, g, j, 0, 0))],
        out_shape=[jax.ShapeDtypeStruct((b, ATT_WIDTH, s), F32),
                   jax.ShapeDtypeStruct((b, ATT_KV_HEADS, nq, nsel, tq), F32)],
        compiler_params=_params(3),
        name="cmp",
    )(qt, kcmp, vcmpt, ovt)


def _slc_body(qt_ref, ks_ref, vst_ref, kw_ref, vwt_ref, bias_ref, ocmp_ref, gt_ref, gn_ref,
              o_ref, qa0_scr, qa1_scr, s0_scr, s1_scr, mx0_scr, mx1_scr, m_scr, acc_scr, *, tq):
    qa_scr = (qa0_scr, qa1_scr)
    s_scr = (s0_scr, s1_scr)
    mx_scr = (mx0_scr, mx1_scr)
    g = pl.program_id(1)
    qi = pl.program_id(2)
    t0 = qi * tq
    blocks_per_tile = tq // SEL_BLOCK
    qt = _heads_on_lanes(qt_ref)
    for qa in qa_scr:
        qa[:ATT_DIM, :] = qt
        qa[ATT_DIM:, :] = jnp.zeros((LANES - ATT_DIM, qa.shape[1]), qa.dtype)
    m_scr[...] = jnp.full(m_scr.shape, NEG, F32)
    acc_scr[...] = jnp.zeros(acc_scr.shape, F32)

    def stage_a(kv, slot, dummy=None, diag=False):
        kvc = jnp.maximum(kv, 0)
        k0 = pl.multiple_of(kvc * tq, tq)
        grp = pl.multiple_of((kvc * blocks_per_tile) // BIAS_GROUP * BIAS_GROUP, BIAS_GROUP)
        b8 = bias_ref[0, 0, 0, pl.ds(grp, BIAS_GROUP), :]
        if dummy is not None:
            b8 = jnp.where(dummy, NEG, b8)
        b16 = jnp.concatenate([b8, jnp.zeros_like(b8)], axis=0)
        qa = qa_scr[slot]
        qa[ATT_DIM:ATT_DIM + 2 * SUBLANES, :] = _tile_heads(b16).astype(qa.dtype)
        s = jnp.dot(ks_ref[0, 0, pl.ds(k0, tq), :], qa[...], preferred_element_type=F32)
        if diag:
            row = lax.broadcasted_iota(jnp.int32, (tq, tq), 0)
            lane = lax.broadcasted_iota(jnp.int32, (tq, tq), 1)
            s = s + _tile_heads(jnp.where(row <= lane, 0.0, NEG))
        s_scr[slot][...] = s
        mx_scr[slot][...] = jnp.max(s, axis=0, keepdims=True)

    def stage_b(kv, slot):
        k0 = pl.multiple_of(jnp.maximum(kv, 0) * tq, tq)
        m_old = m_scr[...]
        m_new = jnp.maximum(m_old, mx_scr[slot][...])
        alpha = jnp.exp2(m_old - m_new)
        p = jnp.exp2(s_scr[slot][...] - m_new).astype(MXU_DT)
        vt = _with_ones_row(vst_ref[0, :, pl.ds(k0, tq)])
        acc_scr[...] = alpha * acc_scr[...] + jnp.dot(vt, p, preferred_element_type=F32)
        m_scr[...] = m_new

    off = 1 - qi % 2
    n_pairs = (qi + 2) // 2
    stage_a(-off, 0, dummy=off == 1)

    def pair_body(pr, carry):
        first = 2 * pr - off
        stage_a(first + 1, 1)
        stage_b(first, 0)
        stage_a(first + 2, 0)
        stage_b(first + 1, 1)
        return carry

    lax.fori_loop(0, n_pairs - 1, pair_body, 0)
    stage_a(qi, 1, diag=True)
    stage_b(qi - 1, 0)
    stage_b(qi, 1)

    wk = WINDOW + tq
    s0 = pl.multiple_of(jnp.maximum(t0 - WINDOW, 0), tq)
    kpos = s0 + lax.broadcasted_iota(jnp.int32, (wk, tq), 0)
    tpos = t0 + lax.broadcasted_iota(jnp.int32, (wk, tq), 1)
    wbias = jnp.where((kpos <= tpos) & (kpos > tpos - WINDOW), 0.0, NEG)
    sw = jnp.dot(kw_ref[0, 0, pl.ds(s0, wk), :], qt, preferred_element_type=F32)
    sw = sw + _tile_heads(wbias)
    pw = jnp.exp2(sw - jnp.max(sw, axis=0, keepdims=True)).astype(MXU_DT)
    ow = jnp.dot(_with_ones_row(vwt_ref[0, :, pl.ds(s0, wk)]), pw, preferred_element_type=F32)
    o_win_all = ow[:ATT_DIM] / ow[ATT_DIM:ATT_DIM + 1]
    o_slc_all = acc_scr[:ATT_DIM, :] / acc_scr[ATT_DIM:ATT_DIM + 1, :]
    for r in range(ATT_REP):
        o_win = o_win_all[:, r * tq:(r + 1) * tq]
        o_slc = o_slc_all[:, r * tq:(r + 1) * tq]
        o_cmp = ocmp_ref[0, r * ATT_DIM:(r + 1) * ATT_DIM, :]
        gbase = (g * ATT_REP + r) * 3
        g_cmp = _sigmoid(gt_ref[0, pl.ds(gbase, 1), :])
        g_slc = _sigmoid(gt_ref[0, pl.ds(gbase + 1, 1), :])
        g_win = _sigmoid(gt_ref[0, pl.ds(gbase + 2, 1), :])
        o = g_cmp * o_cmp + g_slc * o_slc + g_win * o_win
        o = o * lax.rsqrt(jnp.mean(o * o, axis=0, keepdims=True) + RMS_EPS)
        o = o * gn_ref[r * ATT_DIM:(r + 1) * ATT_DIM, :]
        o_ref[0, r * ATT_DIM:(r + 1) * ATT_DIM, :] = o.astype(o_ref.dtype)


def _slc_call(qt, ks, vst, kw, vwt, bias, ocmp, smallt, gn, tq):
    b, _, _, s = qt.shape
    nsel = bias.shape[3]
    gw = ATT_REP * ATT_DIM
    vres = pl.BlockSpec((1, ATT_DIM, s), lambda i, g, j: (i, g, 0))
    return pl.pallas_call(
        functools.partial(_slc_body, tq=tq),
        grid=(b, ATT_KV_HEADS, s // tq),
        in_specs=[pl.BlockSpec((1, ATT_REP, ATT_DIM, tq), lambda i, g, j: (i, g, 0, j)),
                  pl.BlockSpec((1, 1, s, LANES), lambda i, g, j: (i, g, 0, 0)), vres,
                  pl.BlockSpec((1, 1, s, ATT_DIM), lambda i, g, j: (i, g, 0, 0)), vres,
                  pl.BlockSpec((1, 1, 1, nsel, tq), lambda i, g, j: (i, g, j, 0, 0)),
                  pl.BlockSpec((1, gw, tq), lambda i, g, j: (i, g, j)),
                  pl.BlockSpec((1, LANES, tq), lambda i, g, j: (i, 0, j)),
                  pl.BlockSpec((gw, tq), lambda i, g, j: (g, 0))],
        out_specs=pl.BlockSpec((1, gw, tq), lambda i, g, j: (i, g, j)),
        out_shape=jax.ShapeDtypeStruct((b, ATT_WIDTH, s), MXU_DT),
        scratch_shapes=[pltpu.VMEM((LANES, ATT_REP * tq), MXU_DT),
                        pltpu.VMEM((LANES, ATT_REP * tq), MXU_DT),
                        pltpu.VMEM((tq, ATT_REP * tq), F32),
                        pltpu.VMEM((tq, ATT_REP * tq), F32),
                        pltpu.VMEM((1, ATT_REP * tq), F32),
                        pltpu.VMEM((1, ATT_REP * tq), F32),
                        pltpu.VMEM((1, ATT_REP * tq), F32),
                        pltpu.VMEM((ATT_DIM + 2 * SUBLANES, ATT_REP * tq), F32)],
        compiler_params=_params(3),
        name="slc",
    )(qt, ks, vst, kw, vwt, bias, ocmp, smallt, gn)


def _mlstm_body(mqk_ref, mv_ref, mo_ref, sm_ref, smt_ref, cw_ref, cb_ref, gbr_ref, gbc_ref, ng_ref,
                o_ref, c_scr, m_scr, tail_scr, *, chunk):
    @pl.when(pl.program_id(1) == 0)
    def _():
        c_scr[...] = jnp.zeros(c_scr.shape, F32)
        m_scr[...] = jnp.full(m_scr.shape, NEG, F32)
        tail_scr[...] = jnp.zeros(tail_scr.shape, F32)

    x = mqk_ref[0].astype(F32)
    tail = tail_scr[...]
    y = cb_ref[...] + x * cw_ref[M_CONV - 1:M_CONV, :]
    for k in range(1, M_CONV):
        y = y + _shift_rows(x, tail, k) * cw_ref[M_CONV - 1 - k:M_CONV - k, :]
    tail_scr[...] = x[chunk - SUBLANES:, :]
    qk = _silu(y)
    sm = sm_ref[0] + gbr_ref[...]
    smt = smt_ref[0] + gbc_ref[...]
    row = lax.broadcasted_iota(jnp.int32, (chunk, chunk), 0)
    col = lax.broadcasted_iota(jnp.int32, (chunk, chunk), 1)
    causal = row >= col
    ones = jnp.ones((chunk, M_DIM), F32)

    def log_sigmoid(z):
        return jnp.minimum(z, 0.0) - jnp.log(1.0 + jnp.exp(-jnp.abs(z)))

    for h in range(M_HEADS):
        q = qk[:, h * M_DIM:(h + 1) * M_DIM]
        k = qk[:, M_WIDTH + h * M_DIM:M_WIDTH + (h + 1) * M_DIM] * (M_DIM ** -0.5)
        v = mv_ref[0, :, h * M_DIM:(h + 1) * M_DIM]
        i_col = sm[:, SMALL_I + h:SMALL_I + h + 1]
        lf_col = log_sigmoid(sm[:, SMALL_F + h:SMALL_F + h + 1])
        i_row = smt[SMALL_I + h:SMALL_I + h + 1, :]
        lf_row = log_sigmoid(smt[SMALL_F + h:SMALL_F + h + 1, :])
        b_col = jnp.sum(jnp.where(causal, lf_row, 0.0), axis=1, keepdims=True)
        b_row = jnp.sum(jnp.where(row <= col, lf_col, 0.0), axis=0, keepdims=True)
        m_prev = m_scr[h]
        dmat = jnp.where(causal, b_col - b_row + i_row, NEG)
        inter = b_col + m_prev
        mt = jnp.maximum(inter, jnp.max(dmat, axis=1, keepdims=True))
        w_intra = jnp.exp(dmat - mt)
        w_inter = jnp.exp(inter - mt)
        qkw = _dot_nt(q, k) * w_intra
        c_aug = c_scr[h]
        r1 = _dot(q, c_aug)
        num = w_inter * r1[:, :M_DIM] + _dot(qkw, v)
        den = w_inter * r1[:, M_DIM:M_DIM + 1] + jnp.sum(qkw, axis=1, keepdims=True)
        hh = num / jnp.maximum(jnp.abs(den), jnp.exp(-mt))
        b_last = b_col[chunk - 1:chunk, :]
        a_col = b_last - b_col + i_col
        m_new = jnp.maximum(b_last + m_prev, jnp.max(a_col, axis=0, keepdims=True))
        ws = jnp.exp(a_col - m_new)
        decay = jnp.exp(b_last + m_prev - m_new)
        v_aug = jnp.concatenate([v.astype(F32), ones], axis=1)
        c_scr[h] = decay * c_aug + _dot_tn(k * ws, v_aug)
        m_scr[h] = m_new
        hm = _sigmoid(mo_ref[0, :, h * M_DIM:(h + 1) * M_DIM].astype(F32)) * hh
        hm = hm * lax.rsqrt(jnp.mean(hm * hm, axis=1, keepdims=True) + RMS_EPS)
        o_ref[0, :, h * M_DIM:(h + 1) * M_DIM] = (hm * ng_ref[:, h * M_DIM:(h + 1) * M_DIM]).astype(o_ref.dtype)


def _mlstm_call(mqk, mv, mo, small, smallt, cw, cb, gbr, gbc, ng, chunk):
    b, s, _ = mv.shape
    row = lambda n: pl.BlockSpec((1, chunk, n), lambda i, j: (i, j, 0))
    full = lambda a: pl.BlockSpec(a.shape, lambda i, j: (0,) * a.ndim)
    return pl.pallas_call(
        functools.partial(_mlstm_body, chunk=chunk),
        grid=(b, s // chunk),
        in_specs=[row(2 * M_WIDTH), row(M_WIDTH), row(M_WIDTH), row(LANES),
                  pl.BlockSpec((1, LANES, chunk), lambda i, j: (i, 0, j)),
                  full(cw), full(cb), full(gbr), full(gbc), full(ng)],
        out_specs=row(M_WIDTH),
        out_shape=jax.ShapeDtypeStruct((b, s, M_WIDTH), MXU_DT),
        scratch_shapes=[pltpu.VMEM((M_HEADS, M_DIM, 2 * M_DIM), F32),
                        pltpu.VMEM((M_HEADS, 1, 1), F32),
                        pltpu.VMEM((SUBLANES, 2 * M_WIDTH), F32)],
        compiler_params=_params(2),
        name="mlstm",
    )(mqk, mv, mo, small, smallt, cw, cb, gbr, gbc, ng)


def _mix_ffn_body(x_ref, attt_ref, hm_ref, wo_ref, g1_ref, sc_ref, sh_ref, g2_ref, ng_ref,
                  wup_ref, cw_ref, cb_ref, wdn_ref, fg_ref, o_ref, tail_scr, *, d_ff, tf, tiles_per_seq, final):
    @pl.when(pl.program_id(0) % tiles_per_seq == 0)
    def _():
        tail_scr[...] = jnp.zeros(tail_scr.shape, F32)

    d = x_ref.shape[-1]
    mix = _dot_tn(attt_ref[0], wo_ref[:ATT_WIDTH, :]) + _dot(hm_ref[0], wo_ref[ATT_WIDTH:, :])
    x = x_ref[0] + g1_ref[0] * mix
    var = jnp.mean(x * x, axis=-1, keepdims=True)
    h = (x * lax.rsqrt(var + RMS_EPS)) * ng_ref[...]
    hb = (h * (1.0 + sc_ref[0]) + sh_ref[0]).astype(MXU_DT)
    acc = jnp.zeros((x.shape[0], d), F32)
    for c in range(d_ff // tf):
        a = jnp.dot(hb, wup_ref[:, c * tf:(c + 1) * tf], preferred_element_type=F32)
        v = jnp.dot(hb, wup_ref[:, d_ff + c * tf:d_ff + (c + 1) * tf], preferred_element_type=F32)
        tail = tail_scr[c]
        y = cb_ref[:, c * tf:(c + 1) * tf] + a * cw_ref[FFN_CONV - 1:FFN_CONV, c * tf:(c + 1) * tf]
        for k in range(1, FFN_CONV):
            y = y + _shift_rows(a, tail, k) * cw_ref[FFN_CONV - 1 - k:FFN_CONV - k, c * tf:(c + 1) * tf]
        tail_scr[c] = a[a.shape[0] - SUBLANES:, :]
        acc = acc + _dot(_silu(y) * v, wdn_ref[c * tf:(c + 1) * tf, :])
    out = x + g2_ref[0] * acc
    if final:
        var = jnp.mean(out * out, axis=-1, keepdims=True)
        out = (out * lax.rsqrt(var + RMS_EPS)) * fg_ref[...]
    o_ref[0] = out


def _mix_ffn_call(x, attt, hm, wo, g1, sc, sh, g2, ng, wup, cw, cb, wdn, fg, tm, tf, final):
    b, s, d = x.shape
    d_ff = wdn.shape[0]
    tiles_per_seq = s // tm
    row = lambda n: pl.BlockSpec((1, tm, n), lambda i: (i // tiles_per_seq, i % tiles_per_seq, 0))
    vec = pl.BlockSpec((1, 1, d), lambda i: (i // tiles_per_seq, 0, 0))
    full = lambda a: pl.BlockSpec(a.shape, lambda i: (0,) * a.ndim)
    return pl.pallas_call(
        functools.partial(_mix_ffn_body, d_ff=d_ff, tf=tf, tiles_per_seq=tiles_per_seq, final=final),
        grid=(b * tiles_per_seq,),
        in_specs=[row(d),
                  pl.BlockSpec((1, ATT_WIDTH, tm), lambda i: (i // tiles_per_seq, 0, i % tiles_per_seq)),
                  row(M_WIDTH), full(wo), vec, vec, vec, vec, full(ng),
                  full(wup), full(cw), full(cb), full(wdn), full(fg)],
        out_specs=row(d),
        out_shape=jax.ShapeDtypeStruct((b, s, d), F32),
        scratch_shapes=[pltpu.VMEM((d_ff // tf, SUBLANES, tf), F32)],
        compiler_params=_params(1),
        name="mix_ffn",
    )(x, attt, hm, wo, g1, sc, sh, g2, ng, wup, cw, cb, wdn, fg)


def _pick_tile(n, pref):
    t = min(n, pref)
    assert n % t == 0, (n, t)
    return t


def _pack_w_in(w_in):
    att_cols = ATT_WIDTH + 6 * KV_WIDTH
    n_gate = 3 * ATT_HEADS
    o = att_cols
    gates = w_in[:, o:o + n_gate]; o += n_gate
    mqk = w_in[:, o:o + 2 * M_WIDTH]; o += 2 * M_WIDTH
    mv = w_in[:, o:o + M_WIDTH]; o += M_WIDTH
    mif = w_in[:, o:o + 2 * M_HEADS]; o += 2 * M_HEADS
    mo = w_in[:, o:o + M_WIDTH]
    pad = jnp.zeros((w_in.shape[0], LANES - n_gate - 2 * M_HEADS), w_in.dtype)
    return jnp.concatenate([w_in[:, :att_cols], mqk, mv, mo, gates, mif, pad], axis=1).astype(MXU_DT)


def _expand_cmp_weights(pos, w1, w2):
    half = CMP_BLOCK // 2
    w1r = w1.reshape(CMP_BLOCK, ATT_DIM, ATT_DIM)
    eye = jnp.eye(ATT_KV_HEADS, dtype=w1.dtype)

    def expand(wl):
        z = wl[:, None, :, None, :] * eye[None, :, None, :, None]
        return z.reshape(half * KV_WIDTH, KV_WIDTH).astype(MXU_DT)

    def expand_pos(pl_):
        return jnp.broadcast_to(pl_[:, None, :], (half, ATT_KV_HEADS, ATT_DIM)).reshape(1, half * KV_WIDTH)

    w2e = (w2[None, :, None, :] * eye[:, None, :, None]).reshape(KV_WIDTH, KV_WIDTH).astype(MXU_DT)
    return expand_pos(pos[:half]), expand_pos(pos[half:]), expand(w1r[:half]), expand(w1r[half:]), w2e


def kernel(x, c, positions, norm1_g, norm2_g, ada_w, ada_b, w_in, cmp_pos, cmp_w1, cmp_w2, att_norm_g,
           m_conv_w, m_conv_b, m_gate_b, m_norm_g, w_out, ffn_up, ffn_conv_w, ffn_conv_b, ffn_down, final_g):
    b, s, d = x.shape
    depth = ada_w.shape[0]
    d_ff = ffn_down.shape[1]
    assert s % (CMP_STRIDE * SUBLANES) == 0 and d % LANES == 0 and b <= SUBLANES
    tm_proj = _pick_tile(s, 512)
    tm_ffn = _pick_tile(s, 512)
    tq = _pick_tile(s, 256)
    chunk = _pick_tile(s, 256)
    tf = 256
    assert d_ff % tf == 0 and WINDOW % tq == 0 and s >= WINDOW + tq
    nc = s // CMP_STRIDE
    nsel = s // SEL_BLOCK
    k_top = min(SEL_TOPK, nsel)

    c8 = jnp.zeros((SUBLANES, d), F32).at[:b].set(c)
    mod = _ada_call(c8, ada_w, ada_b.reshape(depth, 1, 6 * d), _pick_tile(6 * d, 1536))

    def mod_vec(l, k):
        return mod[l, :b, k * d:(k + 1) * d].reshape(b, 1, d)

    lane_d = jnp.arange(LANES) % ATT_DIM
    inv_freq = ROPE_THETA ** (-(2 * (lane_d % ROPE_HALF)).astype(F32) / ROPE_DIM)
    rc, rs1, rs2 = _rope_call(positions.reshape(b, s, 1), inv_freq.reshape(1, LANES), _pick_tile(s, 1024))

    c_start = jnp.arange(nc)[None, :] * CMP_STRIDE
    s_start = jnp.arange(nsel)[:, None] * SEL_BLOCK
    ovt = ((c_start <= s_start + SEL_BLOCK - 1) & (c_start + CMP_BLOCK - 1 >= s_start)
           & (jnp.arange(nc)[None, :] < nc - 1)).astype(MXU_DT)

    for l in range(depth):
        outs = _proj_call(x, mod_vec(l, 1), mod_vec(l, 0), norm1_g[l].reshape(1, d), _pack_w_in(w_in[l]),
                          rc, rs1, rs2, tm_proj)
        q, kc, vc, ks, vst, kw, vwt, mqk, mv, mo, small, smallt = outs
        kcmp, _ = _compress_call(kc.reshape(b, nc, CMP_STRIDE * KV_WIDTH),
                                 *_expand_cmp_weights(cmp_pos[l, 0], cmp_w1[l, 0], cmp_w2[l, 0]))
        _, vcmpt = _compress_call(vc.reshape(b, nc, CMP_STRIDE * KV_WIDTH),
                                  *_expand_cmp_weights(cmp_pos[l, 1], cmp_w1[l, 1], cmp_w2[l, 1]))
        ocmp, bias = _cmp_call(q, kcmp, vcmpt, ovt, tq, k_top)
        gn = jnp.broadcast_to(att_norm_g[l][:, None], (ATT_WIDTH, tq))
        attt = _slc_call(q, ks, vst, kw, vwt, bias, ocmp, smallt, gn, tq)
        gbr = jnp.zeros((1, LANES), F32).at[0, SMALL_I:SMALL_I + 2 * M_HEADS].set(m_gate_b[l])
        hm = _mlstm_call(mqk, mv, mo, small, smallt, m_conv_w[l], m_conv_b[l].reshape(1, -1),
                         gbr, gbr.reshape(LANES, 1), m_norm_g[l].reshape(1, -1), chunk)
        x = _mix_ffn_call(x, attt, hm, w_out[l].astype(MXU_DT), mod_vec(l, 2), mod_vec(l, 4), mod_vec(l, 3),
                          mod_vec(l, 5), norm2_g[l].reshape(1, d), ffn_up[l].astype(MXU_DT), ffn_conv_w[l],
                          ffn_conv_b[l].reshape(1, -1), ffn_down[l].astype(MXU_DT), final_g.reshape(1, d),
                          tm_ffn, tf, l == depth - 1)
    return x
```

```python
import functools

import jax
import jax.numpy as jnp
from jax import lax
from jax.experimental import pallas as pl
from jax.experimental.pallas import tpu as pltpu

F32 = jnp.float32
BF16 = jnp.bfloat16
MXU_DT = jnp.bfloat16

ATT_DIM = 64
ATT_HEADS = 8
ATT_KV_HEADS = 2
ATT_REP = ATT_HEADS // ATT_KV_HEADS
ATT_WIDTH = ATT_HEADS * ATT_DIM
KV_WIDTH = ATT_KV_HEADS * ATT_DIM
ROPE_DIM = ATT_DIM // 4
ROPE_HALF = ROPE_DIM // 2
ROPE_THETA = 500000.0
CMP_BLOCK = 32
CMP_STRIDE = 16
SEL_BLOCK = 64
SEL_TOPK = 16
WINDOW = 512
FORCE_BONUS = 1.0e4
M_HEADS = 4
M_DIM = 128
M_WIDTH = M_HEADS * M_DIM
M_CONV = 4
FFN_CONV = 3
RMS_EPS = 1e-6
NEG = -1e30
Q_SCALE = ATT_DIM ** -0.5 * 1.4426950408889634
BIAS_GROUP = 8

LANES = 128
SUBLANES = 8
VMEM_LIMIT = 56 * 1024 * 1024

COL_Q = 0
COL_KC = 512
COL_VC = 640
COL_KS = 768
COL_VS = 896
COL_KW = 1024
COL_VW = 1152
COL_MQK = 1280
COL_MV = 2304
COL_MO = 2816
COL_SMALL = 3328
IN_COLS_PACKED = 3456
SMALL_I = 3 * ATT_HEADS
SMALL_F = SMALL_I + M_HEADS

_NT = (((1,), (1,)), ((), ()))
_TN = (((0,), (0,)), ((), ()))


def _params(n_axes):
    return pltpu.CompilerParams(dimension_semantics=("arbitrary",) * n_axes,
                                vmem_limit_bytes=VMEM_LIMIT)


def _dot(a, b):
    return jnp.dot(a.astype(MXU_DT), b.astype(MXU_DT), preferred_element_type=F32)


def _dot_nt(a, b):
    return lax.dot_general(a.astype(MXU_DT), b.astype(MXU_DT), _NT, preferred_element_type=F32)


def _dot_tn(a, b):
    return lax.dot_general(a.astype(MXU_DT), b.astype(MXU_DT), _TN, preferred_element_type=F32)


def _sigmoid(x):
    return 1.0 / (1.0 + jnp.exp(-x))


def _silu(x):
    return x * _sigmoid(x)


def _shift_rows(x, prev8, k):
    rolled = pltpu.roll(x, k, 0)
    fix = pltpu.roll(prev8, k, 0)
    row = lax.broadcasted_iota(jnp.int32, fix.shape, 0)
    top = jnp.where(row < k, fix, rolled[:SUBLANES])
    return jnp.concatenate([top, rolled[SUBLANES:]], axis=0)


def _ada_body(c_ref, w_ref, b_ref, o_ref):
    c = c_ref[...]
    o_ref[0] = _dot(_silu(c), w_ref[0]) + b_ref[0]


def _ada_call(c8, ada_w, ada_b3, tn):
    depth, d, n = ada_w.shape
    return pl.pallas_call(
        _ada_body,
        grid=(depth, n // tn),
        in_specs=[pl.BlockSpec((SUBLANES, d), lambda l, j: (0, 0)),
                  pl.BlockSpec((1, d, tn), lambda l, j: (l, 0, j)),
                  pl.BlockSpec((1, 1, tn), lambda l, j: (l, 0, j))],
        out_specs=pl.BlockSpec((1, SUBLANES, tn), lambda l, j: (l, 0, j)),
        out_shape=jax.ShapeDtypeStruct((depth, SUBLANES, n), F32),
        compiler_params=_params(2),
        name="ada",
    )(c8, ada_w, ada_b3)


def _rope_body(pos_ref, freq_ref, rc_ref, rs1_ref, rs2_ref):
    pos = pos_ref[0].astype(F32)
    ang = pos * freq_ref[...]
    d = lax.broadcasted_iota(jnp.int32, ang.shape, 1) % ATT_DIM
    cos = jnp.cos(ang)
    sin = jnp.sin(ang)
    rc_ref[0] = jnp.where(d < ROPE_DIM, cos, 1.0)
    rs1_ref[0] = jnp.where(d < ROPE_HALF, -sin, 0.0)
    rs2_ref[0] = jnp.where((d >= ROPE_HALF) & (d < ROPE_DIM), sin, 0.0)


def _rope_call(pos3, freq, tm):
    b, s, _ = pos3.shape
    spec = pl.BlockSpec((1, tm, LANES), lambda i, j: (i, j, 0))
    shp = jax.ShapeDtypeStruct((b, s, LANES), F32)
    return pl.pallas_call(
        _rope_body,
        grid=(b, s // tm),
        in_specs=[pl.BlockSpec((1, tm, 1), lambda i, j: (i, j, 0)),
                  pl.BlockSpec((1, LANES), lambda i, j: (0, 0))],
        out_specs=[spec, spec, spec],
        out_shape=[shp, shp, shp],
        compiler_params=_params(2),
        name="rope",
    )(pos3, freq)


def _proj_body(x_ref, sc_ref, sh_ref, g_ref, w_ref, rc_ref, rs1_ref, rs2_ref,
               q_ref, kc_ref, vc_ref, ks_ref, vst_ref, kw_ref, vwt_ref,
               mqk_ref, mv_ref, mo_ref, sm_ref, smt_ref):
    x = x_ref[0]
    var = jnp.mean(x * x, axis=-1, keepdims=True)
    h = (x * lax.rsqrt(var + RMS_EPS)) * g_ref[...]
    h = h * (1.0 + sc_ref[0]) + sh_ref[0]
    hb = h.astype(MXU_DT)

    def mm(c0, n):
        return jnp.dot(hb, w_ref[:, c0:c0 + n], preferred_element_type=F32)

    rc, rs1, rs2 = rc_ref[0], rs1_ref[0], rs2_ref[0]

    def rope(y):
        return (y * rc + pltpu.roll(y, LANES - ROPE_HALF, 1) * rs1
                + pltpu.roll(y, ROPE_HALF, 1) * rs2)

    for j in range(ATT_HEADS // 4):
        y2 = mm(COL_Q + 2 * LANES * j, 2 * LANES)
        for i in range(2):
            yt = (rope(y2[:, LANES * i:LANES * (i + 1)]) * Q_SCALE).T
            q_ref[0, 4 * j + 2 * i] = yt[:ATT_DIM].astype(q_ref.dtype)
            q_ref[0, 4 * j + 2 * i + 1] = yt[ATT_DIM:].astype(q_ref.dtype)
    y2 = mm(COL_KC, 2 * KV_WIDTH)
    kc_ref[0] = rope(y2[:, :KV_WIDTH])
    vc_ref[0] = y2[:, KV_WIDTH:]
    y2 = mm(COL_KS, 2 * KV_WIDTH)
    y = rope(y2[:, :KV_WIDTH])
    lane = lax.broadcasted_iota(jnp.int32, y.shape, 1)
    t = pl.program_id(1) * y.shape[0] + lax.broadcasted_iota(jnp.int32, y.shape, 0)
    onehot = jnp.where(lane - ATT_DIM == (t // SEL_BLOCK) % BIAS_GROUP, 1.0, 0.0)
    ks_ref[0, 0] = jnp.where(lane < ATT_DIM, y, onehot).astype(ks_ref.dtype)
    ks_ref[0, 1] = jnp.where(lane < ATT_DIM, pltpu.roll(y, ATT_DIM, 1), onehot).astype(ks_ref.dtype)
    vst_ref[0] = y2[:, KV_WIDTH:].T.astype(vst_ref.dtype)
    y2 = mm(COL_KW, 2 * KV_WIDTH)
    y = rope(y2[:, :KV_WIDTH])
    kw_ref[0, 0] = y[:, :ATT_DIM].astype(kw_ref.dtype)
    kw_ref[0, 1] = y[:, ATT_DIM:].astype(kw_ref.dtype)
    vwt_ref[0] = y2[:, KV_WIDTH:].T.astype(vwt_ref.dtype)
    for j in range(2 * M_WIDTH // 256):
        mqk_ref[0, :, 256 * j:256 * (j + 1)] = mm(COL_MQK + 256 * j, 256).astype(mqk_ref.dtype)
    for j in range(M_WIDTH // 256):
        mv_ref[0, :, 256 * j:256 * (j + 1)] = mm(COL_MV + 256 * j, 256).astype(mv_ref.dtype)
        mo_ref[0, :, 256 * j:256 * (j + 1)] = mm(COL_MO + 256 * j, 256).astype(mo_ref.dtype)
    sm = mm(COL_SMALL, LANES)
    sm_ref[0] = sm
    smt_ref[0] = sm.T


def _proj_call(x, sc, sh, g, w, rc, rs1, rs2, tm):
    b, s, d = x.shape
    row = lambda n: pl.BlockSpec((1, tm, n), lambda i, j: (i, j, 0))
    vec = pl.BlockSpec((1, 1, d), lambda i, j: (i, 0, 0))
    headed = lambda n, w: pl.BlockSpec((1, n, tm, w), lambda i, j: (i, 0, j, 0))
    tposed = pl.BlockSpec((1, LANES, tm), lambda i, j: (i, 0, j))
    sds = jax.ShapeDtypeStruct
    return pl.pallas_call(
        _proj_body,
        grid=(b, s // tm),
        in_specs=[row(d), vec, vec,
                  pl.BlockSpec((1, d), lambda i, j: (0, 0)),
                  pl.BlockSpec((d, IN_COLS_PACKED), lambda i, j: (0, 0)),
                  row(LANES), row(LANES), row(LANES)],
        out_specs=[pl.BlockSpec((1, ATT_HEADS, ATT_DIM, tm), lambda i, j: (i, 0, 0, j)),
                   row(KV_WIDTH), row(KV_WIDTH),
                   headed(ATT_KV_HEADS, LANES), tposed, headed(ATT_KV_HEADS, ATT_DIM), tposed,
                   row(2 * M_WIDTH), row(M_WIDTH), row(M_WIDTH), row(LANES), tposed],
        out_shape=[sds((b, ATT_HEADS, ATT_DIM, s), MXU_DT),
                   sds((b, s, KV_WIDTH), F32), sds((b, s, KV_WIDTH), F32),
                   sds((b, ATT_KV_HEADS, s, LANES), MXU_DT), sds((b, KV_WIDTH, s), MXU_DT),
                   sds((b, ATT_KV_HEADS, s, ATT_DIM), MXU_DT), sds((b, KV_WIDTH, s), MXU_DT),
                   sds((b, s, 2 * M_WIDTH), MXU_DT), sds((b, s, M_WIDTH), MXU_DT),
                   sds((b, s, M_WIDTH), MXU_DT), sds((b, s, LANES), F32), sds((b, LANES, s), F32)],
        compiler_params=_params(2),
        name="proj",
    )(x, sc, sh, g, w, rc, rs1, rs2)


def _compress_body(kr_ref, pa_ref, pb_ref, w1a_ref, w1b_ref, w2_ref, out_ref, outt_ref):
    kr = kr_ref[0]
    nc = kr.shape[0]
    ya = _dot(kr + pa_ref[...], w1a_ref[...])
    yb = _dot(kr + pb_ref[...], w1b_ref[...])
    pre = ya + pltpu.roll(yb, nc - 1, 0)
    out = _dot(_silu(pre), w2_ref[...])
    row = lax.broadcasted_iota(jnp.int32, out.shape, 0)
    out = jnp.where(row < nc - 1, out, 0.0)
    out_ref[0, 0] = out[:, :ATT_DIM].astype(out_ref.dtype)
    out_ref[0, 1] = out[:, ATT_DIM:].astype(out_ref.dtype)
    outt_ref[0] = out.T.astype(outt_ref.dtype)


def _compress_call(kr, pa, pb, w1a, w1b, w2e):
    b, nc, width = kr.shape
    full = lambda a: pl.BlockSpec(a.shape, lambda i: (0,) * a.ndim)
    return pl.pallas_call(
        _compress_body,
        grid=(b,),
        in_specs=[pl.BlockSpec((1, nc, width), lambda i: (i, 0, 0)),
                  full(pa), full(pb), full(w1a), full(w1b), full(w2e)],
        out_specs=[pl.BlockSpec((1, ATT_KV_HEADS, nc, ATT_DIM), lambda i: (i, 0, 0, 0)),
                   pl.BlockSpec((1, KV_WIDTH, nc), lambda i: (i, 0, 0))],
        out_shape=[jax.ShapeDtypeStruct((b, ATT_KV_HEADS, nc, ATT_DIM), MXU_DT),
                   jax.ShapeDtypeStruct((b, KV_WIDTH, nc), MXU_DT)],
        compiler_params=_params(1),
        name="compress",
    )(kr, pa, pb, w1a, w1b, w2e)


def _heads_on_lanes(qt_ref):
    return jnp.concatenate([qt_ref[0, r] for r in range(ATT_REP)], axis=1)


def _tile_heads(a):
    return jnp.concatenate([a] * ATT_REP, axis=1)


def _with_ones_row(vt):
    n = vt.shape[1]
    extra = jnp.where(lax.broadcasted_iota(jnp.int32, (2 * SUBLANES, n), 0) == 0, 1.0, 0.0)
    return jnp.concatenate([vt, extra.astype(vt.dtype)], axis=0)


def _cmp_body(qt_ref, kc_ref, vct_ref, ovt_ref, ocmp_ref, bias_ref, *, tq, k_top, n_split):
    t0 = pl.program_id(2) * tq
    nc_all = kc_ref.shape[2]
    nsel_all = ovt_ref.shape[0]
    rows_per_sel = SEL_BLOCK // CMP_STRIDE

    def compute(nc):
        nsel = nc // rows_per_sel
        row = lax.broadcasted_iota(jnp.int32, (nc, tq), 0)
        tpos = t0 + lax.broadcasted_iota(jnp.int32, (nc, tq), 1)
        mbias = jnp.where(row * CMP_STRIDE + (CMP_BLOCK - 1) <= tpos, 0.0, NEG)
        s = jnp.dot(kc_ref[0, 0, :nc, :], _heads_on_lanes(qt_ref), preferred_element_type=F32)
        s = s + _tile_heads(mbias)
        m = jnp.max(s, axis=0, keepdims=True)
        m = jnp.where(m > 0.5 * NEG, m, 0.0)
        p = jnp.exp2(s - m).astype(MXU_DT)
        oa = jnp.dot(_with_ones_row(vct_ref[0, :, :nc]), p, preferred_element_type=F32)
        d = oa[ATT_DIM:ATT_DIM + 1]
        inv = 1.0 / jnp.where(d > 0.0, d, 1.0)
        o = oa[:ATT_DIM] * inv
        impa = jnp.dot(ovt_ref[:nsel, :nc], p, preferred_element_type=F32) * inv
        imp = impa[:, :tq]
        for r in range(ATT_REP):
            ocmp_ref[0, r * ATT_DIM:(r + 1) * ATT_DIM, :] = o[:, r * tq:(r + 1) * tq]
            if r:
                imp = imp + impa[:, r * tq:(r + 1) * tq]
        jidx = lax.broadcasted_iota(jnp.int32, (nsel, tq), 0)
        tq_pos = t0 + lax.broadcasted_iota(jnp.int32, (nsel, tq), 1)
        jcur = tq_pos // SEL_BLOCK
        forced = (jidx == 0) | (jidx == jcur) | (jidx == jcur - 1)
        score = jnp.where(jidx * SEL_BLOCK <= tq_pos, imp + jnp.where(forced, FORCE_BONUS, 0.0), NEG)
        valid = score > 0.5 * NEG
        for _ in range(min(k_top, nsel)):
            cur = jnp.max(score, axis=0, keepdims=True)
            first = jnp.min(jnp.where(score == cur, jidx, nsel), axis=0, keepdims=True)
            score = jnp.where(jidx == first, NEG, score)
        bias_ref[0, 0, 0, :nsel, :] = jnp.where(valid & (score < 0.5 * NEG), 0.0, NEG)
        if nsel < nsel_all:
            bias_ref[0, 0, 0, nsel:, :] = jnp.full((nsel_all - nsel, tq), NEG, F32)

    chunk_pos = nc_all // n_split * CMP_STRIDE
    need = jnp.minimum((t0 + tq - 1) // chunk_pos + 1, n_split)
    for k in range(1, n_split + 1):
        pl.when(need == k)(functools.partial(compute, nc_all // n_split * k))


def _cmp_call(qt, kcmp, vcmpt, ovt, tq, k_top):
    b, _, _, s = qt.shape
    nc = kcmp.shape[2]
    nsel = ovt.shape[0]
    nq = s // tq
    n_split = 4 if nc % (4 * LANES) == 0 and (nc // 4 * CMP_STRIDE) % tq == 0 else 1
    return pl.pallas_call(
        functools.partial(_cmp_body, tq=tq, k_top=k_top, n_split=n_split),
        grid=(b, ATT_KV_HEADS, nq),
        in_specs=[pl.BlockSpec((1, ATT_REP, ATT_DIM, tq), lambda i, g, j: (i, g, 0, j)),
                  pl.BlockSpec((1, 1, nc, ATT_DIM), lambda i, g, j: (i, g, 0, 0)),
                  pl.BlockSpec((1, ATT_DIM, nc), lambda i, g, j: (i, g, 0)),
                  pl.BlockSpec((nsel, nc), lambda i, g, j: (0, 0))],
        out_specs=[pl.BlockSpec((1, ATT_REP * ATT_DIM, tq), lambda i, g, j: (i, g, j)),
                   pl.BlockSpec((1, 1, 1, nsel, tq), lambda i, g, j: (i, g, j, 0, 0))],
        out_shape=[jax.ShapeDtypeStruct((b, ATT_WIDTH, s), F32),
                   jax.ShapeDtypeStruct((b, ATT_KV_HEADS, nq, nsel, tq), F32)],
        compiler_params=_params(3),
        name="cmp",
    )(qt, kcmp, vcmpt, ovt)


def _slc_body(qt_ref, ks_ref, vst_ref, kw_ref, vwt_ref, bias_ref, ocmp_ref, gt_ref, gn_ref,
              o_ref, qa0_scr, qa1_scr, s0_scr, s1_scr, mx0_scr, mx1_scr, m_scr, acc_scr, *, tq):
    qa_scr = (qa0_scr, qa1_scr)
    s_scr = (s0_scr, s1_scr)
    mx_scr = (mx0_scr, mx1_scr)
    g = pl.program_id(1)
    qi = pl.program_id(2)
    t0 = qi * tq
    blocks_per_tile = tq // SEL_BLOCK
    qt = _heads_on_lanes(qt_ref)
    for qa in qa_scr:
        qa[:ATT_DIM, :] = qt
        qa[ATT_DIM:, :] = jnp.zeros((LANES - ATT_DIM, qa.shape[1]), qa.dtype)
    m_scr[...] = jnp.full(m_scr.shape, NEG, F32)
    acc_scr[...] = jnp.zeros(acc_scr.shape, F32)

    def stage_a(kv, slot, dummy=None, diag=False):
        kvc = jnp.maximum(kv, 0)
        k0 = pl.multiple_of(kvc * tq, tq)
        grp = pl.multiple_of((kvc * blocks_per_tile) // BIAS_GROUP * BIAS_GROUP, BIAS_GROUP)
        b8 = bias_ref[0, 0, 0, pl.ds(grp, BIAS_GROUP), :]
        if dummy is not None:
            b8 = jnp.where(dummy, NEG, b8)
        b16 = jnp.concatenate([b8, jnp.zeros_like(b8)], axis=0)
        qa = qa_scr[slot]
        qa[ATT_DIM:ATT_DIM + 2 * SUBLANES, :] = _tile_heads(b16).astype(qa.dtype)
        s = jnp.dot(ks_ref[0, 0, pl.ds(k0, tq), :], qa[...], preferred_element_type=F32)
        if diag:
            row = lax.broadcasted_iota(jnp.int32, (tq, tq), 0)
            lane = lax.broadcasted_iota(jnp.int32, (tq, tq), 1)
            s = s + _tile_heads(jnp.where(row <= lane, 0.0, NEG))
        s_scr[slot][...] = s
        mx_scr[slot][...] = jnp.max(s, axis=0, keepdims=True)

    def stage_b(kv, slot):
        k0 = pl.multiple_of(jnp.maximum(kv, 0) * tq, tq)
        m_old = m_scr[...]
        m_new = jnp.maximum(m_old, mx_scr[slot][...])
        alpha = jnp.exp2(m_old - m_new)
        p = jnp.exp2(s_scr[slot][...] - m_new).astype(MXU_DT)
        vt = _with_ones_row(vst_ref[0, :, pl.ds(k0, tq)])
        acc_scr[...] = alpha * acc_scr[...] + jnp.dot(vt, p, preferred_element_type=F32)
        m_scr[...] = m_new

    off = 1 - qi % 2
    n_pairs = (qi + 2) // 2
    stage_a(-off, 0, dummy=off == 1)

    def pair(first):
        stage_a(first + 1, 1)
        stage_b(first, 0)
        stage_a(first + 2, 0)
        stage_b(first + 1, 1)

    def quad_body(i, carry):
        pair(4 * i - off)
        pair(4 * i + 2 - off)
        return carry

    def pair_body(pr, carry):
        pair(2 * pr - off)
        return carry

    n_quads = (n_pairs - 1) // 2
    lax.fori_loop(0, n_quads, quad_body, 0)
    lax.fori_loop(2 * n_quads, n_pairs - 1, pair_body, 0)
    stage_a(qi, 1, diag=True)
    stage_b(qi - 1, 0)
    stage_b(qi, 1)

    wk = WINDOW + tq
    s0 = pl.multiple_of(jnp.maximum(t0 - WINDOW, 0), tq)
    kpos = s0 + lax.broadcasted_iota(jnp.int32, (wk, tq), 0)
    tpos = t0 + lax.broadcasted_iota(jnp.int32, (wk, tq), 1)
    wbias = jnp.where((kpos <= tpos) & (kpos > tpos - WINDOW), 0.0, NEG)
    sw = jnp.dot(kw_ref[0, 0, pl.ds(s0, wk), :], qt, preferred_element_type=F32)
    sw = sw + _tile_heads(wbias)
    pw = jnp.exp2(sw - jnp.max(sw, axis=0, keepdims=True)).astype(MXU_DT)
    ow = jnp.dot(_with_ones_row(vwt_ref[0, :, pl.ds(s0, wk)]), pw, preferred_element_type=F32)
    o_win_all = ow[:ATT_DIM] / ow[ATT_DIM:ATT_DIM + 1]
    o_slc_all = acc_scr[:ATT_DIM, :] / acc_scr[ATT_DIM:ATT_DIM + 1, :]
    for r in range(ATT_REP):
        o_win = o_win_all[:, r * tq:(r + 1) * tq]
        o_slc = o_slc_all[:, r * tq:(r + 1) * tq]
        o_cmp = ocmp_ref[0, r * ATT_DIM:(r + 1) * ATT_DIM, :]
        gbase = (g * ATT_REP + r) * 3
        g_cmp = _sigmoid(gt_ref[0, pl.ds(gbase, 1), :])
        g_slc = _sigmoid(gt_ref[0, pl.ds(gbase + 1, 1), :])
        g_win = _sigmoid(gt_ref[0, pl.ds(gbase + 2, 1), :])
        o = g_cmp * o_cmp + g_slc * o_slc + g_win * o_win
        o = o * lax.rsqrt(jnp.mean(o * o, axis=0, keepdims=True) + RMS_EPS)
        o = o * gn_ref[r * ATT_DIM:(r + 1) * ATT_DIM, :]
        o_ref[0, r * ATT_DIM:(r + 1) * ATT_DIM, :] = o.astype(o_ref.dtype)


def _slc_call(qt, ks, vst, kw, vwt, bias, ocmp, smallt, gn, tq):
    b, _, _, s = qt.shape
    nsel = bias.shape[3]
    gw = ATT_REP * ATT_DIM
    vres = pl.BlockSpec((1, ATT_DIM, s), lambda i, g, j: (i, g, 0))
    return pl.pallas_call(
        functools.partial(_slc_body, tq=tq),
        grid=(b, ATT_KV_HEADS, s // tq),
        in_specs=[pl.BlockSpec((1, ATT_REP, ATT_DIM, tq), lambda i, g, j: (i, g, 0, j)),
                  pl.BlockSpec((1, 1, s, LANES), lambda i, g, j: (i, g, 0, 0)), vres,
                  pl.BlockSpec((1, 1, s, ATT_DIM), lambda i, g, j: (i, g, 0, 0)), vres,
                  pl.BlockSpec((1, 1, 1, nsel, tq), lambda i, g, j: (i, g, j, 0, 0)),
                  pl.BlockSpec((1, gw, tq), lambda i, g, j: (i, g, j)),
                  pl.BlockSpec((1, LANES, tq), lambda i, g, j: (i, 0, j)),
                  pl.BlockSpec((gw, tq), lambda i, g, j: (g, 0))],
        out_specs=pl.BlockSpec((1, gw, tq), lambda i, g, j: (i, g, j)),
        out_shape=jax.ShapeDtypeStruct((b, ATT_WIDTH, s), MXU_DT),
        scratch_shapes=[pltpu.VMEM((LANES, ATT_REP * tq), MXU_DT),
                        pltpu.VMEM((LANES, ATT_REP * tq), MXU_DT),
                        pltpu.VMEM((tq, ATT_REP * tq), F32),
                        pltpu.VMEM((tq, ATT_REP * tq), F32),
                        pltpu.VMEM((1, ATT_REP * tq), F32),
                        pltpu.VMEM((1, ATT_REP * tq), F32),
                        pltpu.VMEM((1, ATT_REP * tq), F32),
                        pltpu.VMEM((ATT_DIM + 2 * SUBLANES, ATT_REP * tq), F32)],
        compiler_params=_params(3),
        name="slc",
    )(qt, ks, vst, kw, vwt, bias, ocmp, smallt, gn)


def _mlstm_body(mqk_ref, mv_ref, mo_ref, sm_ref, smt_ref, cw_ref, cb_ref, gbr_ref, gbc_ref, ng_ref,
                o_ref, c_scr, m_scr, tail_scr, *, chunk):
    @pl.when(pl.program_id(1) == 0)
    def _():
        c_scr[...] = jnp.zeros(c_scr.shape, F32)
        m_scr[...] = jnp.full(m_scr.shape, NEG, F32)
        tail_scr[...] = jnp.zeros(tail_scr.shape, F32)

    x = mqk_ref[0].astype(F32)
    tail = tail_scr[...]
    y = cb_ref[...] + x * cw_ref[M_CONV - 1:M_CONV, :]
    for k in range(1, M_CONV):
        y = y + _shift_rows(x, tail, k) * cw_ref[M_CONV - 1 - k:M_CONV - k, :]
    tail_scr[...] = x[chunk - SUBLANES:, :]
    qk = _silu(y)
    sm = sm_ref[0] + gbr_ref[...]
    smt = smt_ref[0] + gbc_ref[...]
    row = lax.broadcasted_iota(jnp.int32, (chunk, chunk), 0)
    col = lax.broadcasted_iota(jnp.int32, (chunk, chunk), 1)
    causal = row >= col
    ones = jnp.ones((chunk, M_DIM), F32)

    def log_sigmoid(z):
        return jnp.minimum(z, 0.0) - jnp.log(1.0 + jnp.exp(-jnp.abs(z)))

    for h in range(M_HEADS):
        q = qk[:, h * M_DIM:(h + 1) * M_DIM]
        k = qk[:, M_WIDTH + h * M_DIM:M_WIDTH + (h + 1) * M_DIM] * (M_DIM ** -0.5)
        v = mv_ref[0, :, h * M_DIM:(h + 1) * M_DIM]
        i_col = sm[:, SMALL_I + h:SMALL_I + h + 1]
        lf_col = log_sigmoid(sm[:, SMALL_F + h:SMALL_F + h + 1])
        i_row = smt[SMALL_I + h:SMALL_I + h + 1, :]
        lf_row = log_sigmoid(smt[SMALL_F + h:SMALL_F + h + 1, :])
        b_col = jnp.sum(jnp.where(causal, lf_row, 0.0), axis=1, keepdims=True)
        b_row = jnp.sum(jnp.where(row <= col, lf_col, 0.0), axis=0, keepdims=True)
        m_prev = m_scr[h]
        dmat = jnp.where(causal, b_col - b_row + i_row, NEG)
        inter = b_col + m_prev
        mt = jnp.maximum(inter, jnp.max(dmat, axis=1, keepdims=True))
        w_intra = jnp.exp(dmat - mt)
        w_inter = jnp.exp(inter - mt)
        qkw = _dot_nt(q, k) * w_intra
        c_aug = c_scr[h]
        r1 = _dot(q, c_aug)
        num = w_inter * r1[:, :M_DIM] + _dot(qkw, v)
        den = w_inter * r1[:, M_DIM:M_DIM + 1] + jnp.sum(qkw, axis=1, keepdims=True)
        hh = num / jnp.maximum(jnp.abs(den), jnp.exp(-mt))
        b_last = b_col[chunk - 1:chunk, :]
        a_col = b_last - b_col + i_col
        m_new = jnp.maximum(b_last + m_prev, jnp.max(a_col, axis=0, keepdims=True))
        ws = jnp.exp(a_col - m_new)
        decay = jnp.exp(b_last + m_prev - m_new)
        v_aug = jnp.concatenate([v.astype(F32), ones], axis=1)
        c_scr[h] = decay * c_aug + _dot_tn(k * ws, v_aug)
        m_scr[h] = m_new
        hm = _sigmoid(mo_ref[0, :, h * M_DIM:(h + 1) * M_DIM].astype(F32)) * hh
        hm = hm * lax.rsqrt(jnp.mean(hm * hm, axis=1, keepdims=True) + RMS_EPS)
        o_ref[0, :, h * M_DIM:(h + 1) * M_DIM] = (hm * ng_ref[:, h * M_DIM:(h + 1) * M_DIM]).astype(o_ref.dtype)


def _mlstm_call(mqk, mv, mo, small, smallt, cw, cb, gbr, gbc, ng, chunk):
    b, s, _ = mv.shape
    row = lambda n: pl.BlockSpec((1, chunk, n), lambda i, j: (i, j, 0))
    full = lambda a: pl.BlockSpec(a.shape, lambda i, j: (0,) * a.ndim)
    return pl.pallas_call(
        functools.partial(_mlstm_body, chunk=chunk),
        grid=(b, s // chunk),
        in_specs=[row(2 * M_WIDTH), row(M_WIDTH), row(M_WIDTH), row(LANES),
                  pl.BlockSpec((1, LANES, chunk), lambda i, j: (i, 0, j)),
                  full(cw), full(cb), full(gbr), full(gbc), full(ng)],
        out_specs=row(M_WIDTH),
        out_shape=jax.ShapeDtypeStruct((b, s, M_WIDTH), MXU_DT),
        scratch_shapes=[pltpu.VMEM((M_HEADS, M_DIM, 2 * M_DIM), F32),
                        pltpu.VMEM((M_HEADS, 1, 1), F32),
                        pltpu.VMEM((SUBLANES, 2 * M_WIDTH), F32)],
        compiler_params=_params(2),
        name="mlstm",
    )(mqk, mv, mo, small, smallt, cw, cb, gbr, gbc, ng)


def _mix_ffn_body(x_ref, attt_ref, hm_ref, wo_ref, g1_ref, sc_ref, sh_ref, g2_ref, ng_ref,
                  wup_ref, cw_ref, cb_ref, wdn_ref, fg_ref, o_ref, tail_scr, *, d_ff, tf, tiles_per_seq, final):
    @pl.when(pl.program_id(0) % tiles_per_seq == 0)
    def _():
        tail_scr[...] = jnp.zeros(tail_scr.shape, F32)

    d = x_ref.shape[-1]
    mix = _dot_tn(attt_ref[0], wo_ref[:ATT_WIDTH, :]) + _dot(hm_ref[0], wo_ref[ATT_WIDTH:, :])
    x = x_ref[0] + g1_ref[0] * mix
    var = jnp.mean(x * x, axis=-1, keepdims=True)
    h = (x * lax.rsqrt(var + RMS_EPS)) * ng_ref[...]
    hb = (h * (1.0 + sc_ref[0]) + sh_ref[0]).astype(MXU_DT)
    acc = jnp.zeros((x.shape[0], d), F32)
    for c in range(d_ff // tf):
        a = jnp.dot(hb, wup_ref[:, c * tf:(c + 1) * tf], preferred_element_type=F32)
        v = jnp.dot(hb, wup_ref[:, d_ff + c * tf:d_ff + (c + 1) * tf], preferred_element_type=F32)
        tail = tail_scr[c]
        y = cb_ref[:, c * tf:(c + 1) * tf] + a * cw_ref[FFN_CONV - 1:FFN_CONV, c * tf:(c + 1) * tf]
        for k in range(1, FFN_CONV):
            y = y + _shift_rows(a, tail, k) * cw_ref[FFN_CONV - 1 - k:FFN_CONV - k, c * tf:(c + 1) * tf]
        tail_scr[c] = a[a.shape[0] - SUBLANES:, :]
        acc = acc + _dot(_silu(y) * v, wdn_ref[c * tf:(c + 1) * tf, :])
    out = x + g2_ref[0] * acc
    if final:
        var = jnp.mean(out * out, axis=-1, keepdims=True)
        out = (out * lax.rsqrt(var + RMS_EPS)) * fg_ref[...]
    o_ref[0] = out


def _mix_ffn_call(x, attt, hm, wo, g1, sc, sh, g2, ng, wup, cw, cb, wdn, fg, tm, tf, final):
    b, s, d = x.shape
    d_ff = wdn.shape[0]
    tiles_per_seq = s // tm
    row = lambda n: pl.BlockSpec((1, tm, n), lambda i: (i // tiles_per_seq, i % tiles_per_seq, 0))
    vec = pl.BlockSpec((1, 1, d), lambda i: (i // tiles_per_seq, 0, 0))
    full = lambda a: pl.BlockSpec(a.shape, lambda i: (0,) * a.ndim)
    return pl.pallas_call(
        functools.partial(_mix_ffn_body, d_ff=d_ff, tf=tf, tiles_per_seq=tiles_per_seq, final=final),
        grid=(b * tiles_per_seq,),
        in_specs=[row(d),
                  pl.BlockSpec((1, ATT_WIDTH, tm), lambda i: (i // tiles_per_seq, 0, i % tiles_per_seq)),
                  row(M_WIDTH), full(wo), vec, vec, vec, vec, full(ng),
                  full(wup), full(cw), full(cb), full(wdn), full(fg)],
        out_specs=row(d),
        out_shape=jax.ShapeDtypeStruct((b, s, d), F32),
        scratch_shapes=[pltpu.VMEM((d_ff // tf, SUBLANES, tf), F32)],
        compiler_params=_params(1),
        name="mix_ffn",
    )(x, attt, hm, wo, g1, sc, sh, g2, ng, wup, cw, cb, wdn, fg)


def _pick_tile(n, pref):
    t = min(n, pref)
    assert n % t == 0, (n, t)
    return t


def _pack_w_in(w_in):
    att_cols = ATT_WIDTH + 6 * KV_WIDTH
    n_gate = 3 * ATT_HEADS
    o = att_cols
    gates = w_in[:, o:o + n_gate]; o += n_gate
    mqk = w_in[:, o:o + 2 * M_WIDTH]; o += 2 * M_WIDTH
    mv = w_in[:, o:o + M_WIDTH]; o += M_WIDTH
    mif = w_in[:, o:o + 2 * M_HEADS]; o += 2 * M_HEADS
    mo = w_in[:, o:o + M_WIDTH]
    pad = jnp.zeros((w_in.shape[0], LANES - n_gate - 2 * M_HEADS), w_in.dtype)
    return jnp.concatenate([w_in[:, :att_cols], mqk, mv, mo, gates, mif, pad], axis=1).astype(MXU_DT)


def _expand_cmp_weights(pos, w1, w2):
    half = CMP_BLOCK // 2
    w1r = w1.reshape(CMP_BLOCK, ATT_DIM, ATT_DIM)
    eye = jnp.eye(ATT_KV_HEADS, dtype=w1.dtype)

    def expand(wl):
        z = wl[:, None, :, None, :] * eye[None, :, None, :, None]
        return z.reshape(half * KV_WIDTH, KV_WIDTH).astype(MXU_DT)

    def expand_pos(pl_):
        return jnp.broadcast_to(pl_[:, None, :], (half, ATT_KV_HEADS, ATT_DIM)).reshape(1, half * KV_WIDTH)

    w2e = (w2[None, :, None, :] * eye[:, None, :, None]).reshape(KV_WIDTH, KV_WIDTH).astype(MXU_DT)
    return expand_pos(pos[:half]), expand_pos(pos[half:]), expand(w1r[:half]), expand(w1r[half:]), w2e


def kernel(x, c, positions, norm1_g, norm2_g, ada_w, ada_b, w_in, cmp_pos, cmp_w1, cmp_w2, att_norm_g,
           m_conv_w, m_conv_b, m_gate_b, m_norm_g, w_out, ffn_up, ffn_conv_w, ffn_conv_b, ffn_down, final_g):
    b, s, d = x.shape
    depth = ada_w.shape[0]
    d_ff = ffn_down.shape[1]
    assert s % (CMP_STRIDE * SUBLANES) == 0 and d % LANES == 0 and b <= SUBLANES
    tm_proj = _pick_tile(s, 512)
    tm_ffn = _pick_tile(s, 512)
    tq = _pick_tile(s, 256)
    chunk = _pick_tile(s, 256)
    tf = 256
    assert d_ff % tf == 0 and WINDOW % tq == 0 and s >= WINDOW + tq
    nc = s // CMP_STRIDE
    nsel = s // SEL_BLOCK
    k_top = min(SEL_TOPK, nsel)

    c8 = jnp.zeros((SUBLANES, d), F32).at[:b].set(c)
    mod = _ada_call(c8, ada_w, ada_b.reshape(depth, 1, 6 * d), _pick_tile(6 * d, 1536))

    def mod_vec(l, k):
        return mod[l, :b, k * d:(k + 1) * d].reshape(b, 1, d)

    lane_d = jnp.arange(LANES) % ATT_DIM
    inv_freq = ROPE_THETA ** (-(2 * (lane_d % ROPE_HALF)).astype(F32) / ROPE_DIM)
    rc, rs1, rs2 = _rope_call(positions.reshape(b, s, 1), inv_freq.reshape(1, LANES), _pick_tile(s, 1024))

    c_start = jnp.arange(nc)[None, :] * CMP_STRIDE
    s_start = jnp.arange(nsel)[:, None] * SEL_BLOCK
    ovt = ((c_start <= s_start + SEL_BLOCK - 1) & (c_start + CMP_BLOCK - 1 >= s_start)
           & (jnp.arange(nc)[None, :] < nc - 1)).astype(MXU_DT)

    for l in range(depth):
        outs = _proj_call(x, mod_vec(l, 1), mod_vec(l, 0), norm1_g[l].reshape(1, d), _pack_w_in(w_in[l]),
                          rc, rs1, rs2, tm_proj)
        q, kc, vc, ks, vst, kw, vwt, mqk, mv, mo, small, smallt = outs
        kcmp, _ = _compress_call(kc.reshape(b, nc, CMP_STRIDE * KV_WIDTH),
                                 *_expand_cmp_weights(cmp_pos[l, 0], cmp_w1[l, 0], cmp_w2[l, 0]))
        _, vcmpt = _compress_call(vc.reshape(b, nc, CMP_STRIDE * KV_WIDTH),
                                  *_expand_cmp_weights(cmp_pos[l, 1], cmp_w1[l, 1], cmp_w2[l, 1]))
        ocmp, bias = _cmp_call(q, kcmp, vcmpt, ovt, tq, k_top)
        gn = jnp.broadcast_to(att_norm_g[l][:, None], (ATT_WIDTH, tq))
        attt = _slc_call(q, ks, vst, kw, vwt, bias, ocmp, smallt, gn, tq)
        gbr = jnp.zeros((1, LANES), F32).at[0, SMALL_I:SMALL_I + 2 * M_HEADS].set(m_gate_b[l])
        hm = _mlstm_call(mqk, mv, mo, small, smallt, m_conv_w[l], m_conv_b[l].reshape(1, -1),
                         gbr, gbr.reshape(LANES, 1), m_norm_g[l].reshape(1, -1), chunk)
        x = _mix_ffn_call(x, attt, hm, w_out[l].astype(MXU_DT), mod_vec(l, 2), mod_vec(l, 4), mod_vec(l, 3),
                          mod_vec(l, 5), norm2_g[l].reshape(1, d), ffn_up[l].astype(MXU_DT), ffn_conv_w[l],
                          ffn_conv_b[l].reshape(1, -1), ffn_down[l].astype(MXU_DT), final_g.reshape(1, d),
                          tm_ffn, tf, l == depth - 1)
    return x
```

```python
import functools

import jax
import jax.numpy as jnp
from jax import lax
from jax.experimental import pallas as pl
from jax.experimental.pallas import tpu as pltpu

F32 = jnp.float32
BF16 = jnp.bfloat16
MXU_DT = jnp.bfloat16

ATT_DIM = 64
ATT_HEADS = 8
ATT_KV_HEADS = 2
ATT_REP = ATT_HEADS // ATT_KV_HEADS
ATT_WIDTH = ATT_HEADS * ATT_DIM
KV_WIDTH = ATT_KV_HEADS * ATT_DIM
ROPE_DIM = ATT_DIM // 4
ROPE_HALF = ROPE_DIM // 2
ROPE_THETA = 500000.0
CMP_BLOCK = 32
CMP_STRIDE = 16
SEL_BLOCK = 64
SEL_TOPK = 16
WINDOW = 512
FORCE_BONUS = 1.0e4
M_HEADS = 4
M_DIM = 128
M_WIDTH = M_HEADS * M_DIM
M_CONV = 4
FFN_CONV = 3
RMS_EPS = 1e-6
NEG = -1e30
Q_SCALE = ATT_DIM ** -0.5 * 1.4426950408889634
BIAS_GROUP = 8

LANES = 128
SUBLANES = 8
VMEM_LIMIT = 56 * 1024 * 1024

COL_Q = 0
COL_KC = 512
COL_VC = 640
COL_KS = 768
COL_VS = 896
COL_KW = 1024
COL_VW = 1152
COL_MQK = 1280
COL_MV = 2304
COL_MO = 2816
COL_SMALL = 3328
IN_COLS_PACKED = 3456
SMALL_I = 3 * ATT_HEADS
SMALL_F = SMALL_I + M_HEADS

_NT = (((1,), (1,)), ((), ()))
_TN = (((0,), (0,)), ((), ()))


def _params(n_axes):
    return pltpu.CompilerParams(dimension_semantics=("arbitrary",) * n_axes,
                                vmem_limit_bytes=VMEM_LIMIT)


def _dot(a, b):
    return jnp.dot(a.astype(MXU_DT), b.astype(MXU_DT), preferred_element_type=F32)


def _dot_nt(a, b):
    return lax.dot_general(a.astype(MXU_DT), b.astype(MXU_DT), _NT, preferred_element_type=F32)


def _dot_tn(a, b):
    return lax.dot_general(a.astype(MXU_DT), b.astype(MXU_DT), _TN, preferred_element_type=F32)


def _sigmoid(x):
    return 1.0 / (1.0 + jnp.exp(-x))


def _silu(x):
    return x * _sigmoid(x)


def _shift_rows(x, prev8, k):
    rolled = pltpu.roll(x, k, 0)
    fix = pltpu.roll(prev8, k, 0)
    row = lax.broadcasted_iota(jnp.int32, fix.shape, 0)
    top = jnp.where(row < k, fix, rolled[:SUBLANES])
    return jnp.concatenate([top, rolled[SUBLANES:]], axis=0)


def _ada_body(c_ref, w_ref, b_ref, o_ref):
    c = c_ref[...]
    o_ref[0] = _dot(_silu(c), w_ref[0]) + b_ref[0]


def _ada_call(c8, ada_w, ada_b3, tn):
    depth, d, n = ada_w.shape
    return pl.pallas_call(
        _ada_body,
        grid=(depth, n // tn),
        in_specs=[pl.BlockSpec((SUBLANES, d), lambda l, j: (0, 0)),
                  pl.BlockSpec((1, d, tn), lambda l, j: (l, 0, j)),
                  pl.BlockSpec((1, 1, tn), lambda l, j: (l, 0, j))],
        out_specs=pl.BlockSpec((1, SUBLANES, tn), lambda l, j: (l, 0, j)),
        out_shape=jax.ShapeDtypeStruct((depth, SUBLANES, n), F32),
        compiler_params=_params(2),
        name="ada",
    )(c8, ada_w, ada_b3)


def _rope_body(pos_ref, freq_ref, rc_ref, rs1_ref, rs2_ref):
    pos = pos_ref[0].astype(F32)
    ang = pos * freq_ref[...]
    d = lax.broadcasted_iota(jnp.int32, ang.shape, 1) % ATT_DIM
    cos = jnp.cos(ang)
    sin = jnp.sin(ang)
    rc_ref[0] = jnp.where(d < ROPE_DIM, cos, 1.0)
    rs1_ref[0] = jnp.where(d < ROPE_HALF, -sin, 0.0)
    rs2_ref[0] = jnp.where((d >= ROPE_HALF) & (d < ROPE_DIM), sin, 0.0)


def _rope_call(pos3, freq, tm):
    b, s, _ = pos3.shape
    spec = pl.BlockSpec((1, tm, LANES), lambda i, j: (i, j, 0))
    shp = jax.ShapeDtypeStruct((b, s, LANES), F32)
    return pl.pallas_call(
        _rope_body,
        grid=(b, s // tm),
        in_specs=[pl.BlockSpec((1, tm, 1), lambda i, j: (i, j, 0)),
                  pl.BlockSpec((1, LANES), lambda i, j: (0, 0))],
        out_specs=[spec, spec, spec],
        out_shape=[shp, shp, shp],
        compiler_params=_params(2),
        name="rope",
    )(pos3, freq)


def _proj_body(x_ref, sc_ref, sh_ref, g_ref, w_ref, rc_ref, rs1_ref, rs2_ref,
               q_ref, kc_ref, vc_ref, ks_ref, vst_ref, kw_ref, vwt_ref,
               mqk_ref, mv_ref, mo_ref, sm_ref, smt_ref):
    x = x_ref[0]
    var = jnp.mean(x * x, axis=-1, keepdims=True)
    h = (x * lax.rsqrt(var + RMS_EPS)) * g_ref[...]
    h = h * (1.0 + sc_ref[0]) + sh_ref[0]
    hb = h.astype(MXU_DT)

    def mm(c0, n):
        return jnp.dot(hb, w_ref[:, c0:c0 + n], preferred_element_type=F32)

    rc, rs1, rs2 = rc_ref[0], rs1_ref[0], rs2_ref[0]

    def rope(y):
        return (y * rc + pltpu.roll(y, LANES - ROPE_HALF, 1) * rs1
                + pltpu.roll(y, ROPE_HALF, 1) * rs2)

    for j in range(ATT_HEADS // 4):
        y2 = mm(COL_Q + 2 * LANES * j, 2 * LANES)
        for i in range(2):
            yt = (rope(y2[:, LANES * i:LANES * (i + 1)]) * Q_SCALE).T
            q_ref[0, 4 * j + 2 * i] = yt[:ATT_DIM].astype(q_ref.dtype)
            q_ref[0, 4 * j + 2 * i + 1] = yt[ATT_DIM:].astype(q_ref.dtype)
    y2 = mm(COL_KC, 2 * KV_WIDTH)
    kc_ref[0] = rope(y2[:, :KV_WIDTH])
    vc_ref[0] = y2[:, KV_WIDTH:]
    y2 = mm(COL_KS, 2 * KV_WIDTH)
    y = rope(y2[:, :KV_WIDTH])
    lane = lax.broadcasted_iota(jnp.int32, y.shape, 1)
    t = pl.program_id(1) * y.shape[0] + lax.broadcasted_iota(jnp.int32, y.shape, 0)
    onehot = jnp.where(lane - ATT_DIM == (t // SEL_BLOCK) % BIAS_GROUP, 1.0, 0.0)
    ks_ref[0, 0] = jnp.where(lane < ATT_DIM, y, onehot).astype(ks_ref.dtype)
    ks_ref[0, 1] = jnp.where(lane < ATT_DIM, pltpu.roll(y, ATT_DIM, 1), onehot).astype(ks_ref.dtype)
    vst_ref[0] = y2[:, KV_WIDTH:].T.astype(vst_ref.dtype)
    y2 = mm(COL_KW, 2 * KV_WIDTH)
    y = rope(y2[:, :KV_WIDTH])
    kw_ref[0, 0] = y[:, :ATT_DIM].astype(kw_ref.dtype)
    kw_ref[0, 1] = y[:, ATT_DIM:].astype(kw_ref.dtype)
    vwt_ref[0] = y2[:, KV_WIDTH:].T.astype(vwt_ref.dtype)
    for j in range(2 * M_WIDTH // 256):
        mqk_ref[0, :, 256 * j:256 * (j + 1)] = mm(COL_MQK + 256 * j, 256).astype(mqk_ref.dtype)
    for j in range(M_WIDTH // 256):
        mv_ref[0, :, 256 * j:256 * (j + 1)] = mm(COL_MV + 256 * j, 256).astype(mv_ref.dtype)
        mo_ref[0, :, 256 * j:256 * (j + 1)] = mm(COL_MO + 256 * j, 256).astype(mo_ref.dtype)
    sm = mm(COL_SMALL, LANES)
    sm_ref[0] = sm
    smt_ref[0] = sm.T


def _proj_call(x, sc, sh, g, w, rc, rs1, rs2, tm):
    b, s, d = x.shape
    row = lambda n: pl.BlockSpec((1, tm, n), lambda i, j: (i, j, 0))
    vec = pl.BlockSpec((1, 1, d), lambda i, j: (i, 0, 0))
    headed = lambda n, w: pl.BlockSpec((1, n, tm, w), lambda i, j: (i, 0, j, 0))
    tposed = pl.BlockSpec((1, LANES, tm), lambda i, j: (i, 0, j))
    sds = jax.ShapeDtypeStruct
    return pl.pallas_call(
        _proj_body,
        grid=(b, s // tm),
        in_specs=[row(d), vec, vec,
                  pl.BlockSpec((1, d), lambda i, j: (0, 0)),
                  pl.BlockSpec((d, IN_COLS_PACKED), lambda i, j: (0, 0)),
                  row(LANES), row(LANES), row(LANES)],
        out_specs=[pl.BlockSpec((1, ATT_HEADS, ATT_DIM, tm), lambda i, j: (i, 0, 0, j)),
                   row(KV_WIDTH), row(KV_WIDTH),
                   headed(ATT_KV_HEADS, LANES), tposed, headed(ATT_KV_HEADS, ATT_DIM), tposed,
                   row(2 * M_WIDTH), row(M_WIDTH), row(M_WIDTH), row(LANES), tposed],
        out_shape=[sds((b, ATT_HEADS, ATT_DIM, s), MXU_DT),
                   sds((b, s, KV_WIDTH), F32), sds((b, s, KV_WIDTH), F32),
                   sds((b, ATT_KV_HEADS, s, LANES), MXU_DT), sds((b, KV_WIDTH, s), MXU_DT),
                   sds((b, ATT_KV_HEADS, s, ATT_DIM), MXU_DT), sds((b, KV_WIDTH, s), MXU_DT),
                   sds((b, s, 2 * M_WIDTH), MXU_DT), sds((b, s, M_WIDTH), MXU_DT),
                   sds((b, s, M_WIDTH), MXU_DT), sds((b, s, LANES), F32), sds((b, LANES, s), F32)],
        compiler_params=_params(2),
        name="proj",
    )(x, sc, sh, g, w, rc, rs1, rs2)


def _compress_body(kr_ref, pa_ref, pb_ref, w1a_ref, w1b_ref, w2_ref, out_ref, outt_ref):
    kr = kr_ref[0]
    nc = kr.shape[0]
    ya = _dot(kr + pa_ref[...], w1a_ref[...])
    yb = _dot(kr + pb_ref[...], w1b_ref[...])
    pre = ya + pltpu.roll(yb, nc - 1, 0)
    out = _dot(_silu(pre), w2_ref[...])
    row = lax.broadcasted_iota(jnp.int32, out.shape, 0)
    out = jnp.where(row < nc - 1, out, 0.0)
    out_ref[0, 0] = out[:, :ATT_DIM].astype(out_ref.dtype)
    out_ref[0, 1] = out[:, ATT_DIM:].astype(out_ref.dtype)
    outt_ref[0] = out.T.astype(outt_ref.dtype)


def _compress_call(kr, pa, pb, w1a, w1b, w2e):
    b, nc, width = kr.shape
    full = lambda a: pl.BlockSpec(a.shape, lambda i: (0,) * a.ndim)
    return pl.pallas_call(
        _compress_body,
        grid=(b,),
        in_specs=[pl.BlockSpec((1, nc, width), lambda i: (i, 0, 0)),
                  full(pa), full(pb), full(w1a), full(w1b), full(w2e)],
        out_specs=[pl.BlockSpec((1, ATT_KV_HEADS, nc, ATT_DIM), lambda i: (i, 0, 0, 0)),
                   pl.BlockSpec((1, KV_WIDTH, nc), lambda i: (i, 0, 0))],
        out_shape=[jax.ShapeDtypeStruct((b, ATT_KV_HEADS, nc, ATT_DIM), MXU_DT),
                   jax.ShapeDtypeStruct((b, KV_WIDTH, nc), MXU_DT)],
        compiler_params=_params(1),
        name="compress",
    )(kr, pa, pb, w1a, w1b, w2e)


def _heads_on_lanes(qt_ref):
    return jnp.concatenate([qt_ref[0, r] for r in range(ATT_REP)], axis=1)


def _tile_heads(a):
    return jnp.concatenate([a] * ATT_REP, axis=1)


def _with_ones_row(vt):
    n = vt.shape[1]
    extra = jnp.where(lax.broadcasted_iota(jnp.int32, (2 * SUBLANES, n), 0) == 0, 1.0, 0.0)
    return jnp.concatenate([vt, extra.astype(vt.dtype)], axis=0)


def _cmp_body(qt_ref, kc_ref, vct_ref, ovt_ref, ocmp_ref, bias_ref, *, tq, k_top, n_split):
    t0 = pl.program_id(2) * tq
    nc_all = kc_ref.shape[2]
    nsel_all = ovt_ref.shape[0]
    rows_per_sel = SEL_BLOCK // CMP_STRIDE

    def compute(nc):
        nsel = nc // rows_per_sel
        row = lax.broadcasted_iota(jnp.int32, (nc, tq), 0)
        tpos = t0 + lax.broadcasted_iota(jnp.int32, (nc, tq), 1)
        mbias = jnp.where(row * CMP_STRIDE + (CMP_BLOCK - 1) <= tpos, 0.0, NEG)
        s = jnp.dot(kc_ref[0, 0, :nc, :], _heads_on_lanes(qt_ref), preferred_element_type=F32)
        s = s + _tile_heads(mbias)
        m = jnp.max(s, axis=0, keepdims=True)
        m = jnp.where(m > 0.5 * NEG, m, 0.0)
        p = jnp.exp2(s - m).astype(MXU_DT)
        oa = jnp.dot(_with_ones_row(vct_ref[0, :, :nc]), p, preferred_element_type=F32)
        d = oa[ATT_DIM:ATT_DIM + 1]
        inv = 1.0 / jnp.where(d > 0.0, d, 1.0)
        o = oa[:ATT_DIM] * inv
        impa = jnp.dot(ovt_ref[:nsel, :nc], p, preferred_element_type=F32) * inv
        imp = impa[:, :tq]
        for r in range(ATT_REP):
            ocmp_ref[0, r * ATT_DIM:(r + 1) * ATT_DIM, :] = o[:, r * tq:(r + 1) * tq]
            if r:
                imp = imp + impa[:, r * tq:(r + 1) * tq]
        jidx = lax.broadcasted_iota(jnp.int32, (nsel, tq), 0)
        tq_pos = t0 + lax.broadcasted_iota(jnp.int32, (nsel, tq), 1)
        jcur = tq_pos // SEL_BLOCK
        forced = (jidx == 0) | (jidx == jcur) | (jidx == jcur - 1)
        score = jnp.where(jidx * SEL_BLOCK <= tq_pos, imp + jnp.where(forced, FORCE_BONUS, 0.0), NEG)
        valid = score > 0.5 * NEG
        for _ in range(min(k_top, nsel)):
            cur = jnp.max(score, axis=0, keepdims=True)
            first = jnp.min(jnp.where(score == cur, jidx, nsel), axis=0, keepdims=True)
            score = jnp.where(jidx == first, NEG, score)
        bias_ref[0, 0, 0, :nsel, :] = jnp.where(valid & (score < 0.5 * NEG), 0.0, NEG)
        if nsel < nsel_all:
            bias_ref[0, 0, 0, nsel:, :] = jnp.full((nsel_all - nsel, tq), NEG, F32)

    chunk_pos = nc_all // n_split * CMP_STRIDE
    need = jnp.minimum((t0 + tq - 1) // chunk_pos + 1, n_split)
    for k in range(1, n_split + 1):
        pl.when(need == k)(functools.partial(compute, nc_all // n_split * k))


def _cmp_call(qt, kcmp, vcmpt, ovt, tq, k_top):
    b, _, _, s = qt.shape
    nc = kcmp.shape[2]
    nsel = ovt.shape[0]
    nq = s // tq
    n_split = 4 if nc % (4 * LANES) == 0 and (nc // 4 * CMP_STRIDE) % tq == 0 else 1
    return pl.pallas_call(
        functools.partial(_cmp_body, tq=tq, k_top=k_top, n_split=n_split),
        grid=(b, ATT_KV_HEADS, nq),
        in_specs=[pl.BlockSpec((1, ATT_REP, ATT_DIM, tq), lambda i, g, j: (i, g, 0, j)),
                  pl.BlockSpec((1, 1, nc, ATT_DIM), lambda i, g, j: (i, g, 0, 0)),
                  pl.BlockSpec((1, ATT_DIM, nc), lambda i, g, j: (i, g, 0)),
                  pl.BlockSpec((nsel, nc), lambda i, g, j: (0, 0))],
        out_specs=[pl.BlockSpec((1, ATT_REP * ATT_DIM, tq), lambda i, g, j: (i, g, j)),
                   pl.BlockSpec((1, 1, 1, nsel, tq), lambda i, g, j: (i, g, j, 0, 0))],
        out_shape=[jax.ShapeDtypeStruct((b, ATT_WIDTH, s), F32),
                   jax.ShapeDtypeStruct((b, ATT_KV_HEADS, nq, nsel, tq), F32)],
        compiler_params=_params(3),
        name="cmp",
    )(qt, kcmp, vcmpt, ovt)


def _slc_body(qt_ref, ks_ref, vst_ref, kw_ref, vwt_ref, bias_ref, ocmp_ref, gt_ref, gn_ref,
              o_ref, qa0_scr, qa1_scr, s0_scr, s1_scr, mx0_scr, mx1_scr, m_scr, acc_scr,
              mw_scr, accw_scr, *, tq):
    qa_scr = (qa0_scr, qa1_scr)
    s_scr = (s0_scr, s1_scr)
    mx_scr = (mx0_scr, mx1_scr)
    g = pl.program_id(1)
    qi = pl.program_id(2)
    t0 = qi * tq
    blocks_per_tile = tq // SEL_BLOCK
    qt = _heads_on_lanes(qt_ref)
    for qa in qa_scr:
        qa[:ATT_DIM, :] = qt
        qa[ATT_DIM:, :] = jnp.zeros((LANES - ATT_DIM, qa.shape[1]), qa.dtype)
    for m_ref, a_ref in ((m_scr, acc_scr), (mw_scr, accw_scr)):
        m_ref[...] = jnp.full(m_ref.shape, NEG, F32)
        a_ref[...] = jnp.zeros(a_ref.shape, F32)

    def stage_a(kv, slot, dummy=None, diag=False):
        kvc = jnp.maximum(kv, 0)
        k0 = pl.multiple_of(kvc * tq, tq)
        grp = pl.multiple_of((kvc * blocks_per_tile) // BIAS_GROUP * BIAS_GROUP, BIAS_GROUP)
        b8 = bias_ref[0, 0, 0, pl.ds(grp, BIAS_GROUP), :]
        if dummy is not None:
            b8 = jnp.where(dummy, NEG, b8)
        b16 = jnp.concatenate([b8, jnp.zeros_like(b8)], axis=0)
        qa = qa_scr[slot]
        qa[ATT_DIM:ATT_DIM + 2 * SUBLANES, :] = _tile_heads(b16).astype(qa.dtype)
        s = jnp.dot(ks_ref[0, 0, pl.ds(k0, tq), :], qa[...], preferred_element_type=F32)
        if diag:
            row = lax.broadcasted_iota(jnp.int32, (tq, tq), 0)
            lane = lax.broadcasted_iota(jnp.int32, (tq, tq), 1)
            s = s + _tile_heads(jnp.where(row <= lane, 0.0, NEG))
        s_scr[slot][...] = s
        mx_scr[slot][...] = jnp.max(s, axis=0, keepdims=True)

    def softmax_pv(slot, v_ref, k0, m_ref, acc_ref):
        m_old = m_ref[...]
        m_new = jnp.maximum(m_old, mx_scr[slot][...])
        alpha = jnp.exp2(m_old - m_new)
        p = jnp.exp2(s_scr[slot][...] - m_new).astype(MXU_DT)
        vt = _with_ones_row(v_ref[0, :, pl.ds(k0, tq)])
        acc_ref[...] = alpha * acc_ref[...] + jnp.dot(vt, p, preferred_element_type=F32)
        m_ref[...] = m_new

    def stage_b(kv, slot):
        softmax_pv(slot, vst_ref, pl.multiple_of(jnp.maximum(kv, 0) * tq, tq), m_scr, acc_scr)

    n_win = WINDOW // tq + 1
    wrow = lax.broadcasted_iota(jnp.int32, (tq, tq), 0)
    wlane = lax.broadcasted_iota(jnp.int32, (tq, tq), 1)

    def win_start(wi):
        return t0 - WINDOW + wi * tq

    def stage_a_win(wi, slot):
        start = win_start(wi)
        k0 = pl.multiple_of(jnp.maximum(start, 0), tq)
        s = jnp.dot(kw_ref[0, 0, pl.ds(k0, tq), :], qt, preferred_element_type=F32)
        if wi == 0:
            visible = (wrow > wlane) & (start >= 0)
        elif wi == n_win - 1:
            visible = wrow <= wlane
        else:
            visible = jnp.broadcast_to(start >= 0, (tq, tq))
        s = s + _tile_heads(jnp.where(visible, 0.0, NEG))
        s_scr[slot][...] = s
        mx_scr[slot][...] = jnp.max(s, axis=0, keepdims=True)

    def stage_b_win(wi, slot):
        softmax_pv(slot, vwt_ref, pl.multiple_of(jnp.maximum(win_start(wi), 0), tq), mw_scr, accw_scr)

    off = 1 - qi % 2
    n_pairs = (qi + 2) // 2
    stage_a(-off, 0, dummy=off == 1)

    def pair(first):
        stage_a(first + 1, 1)
        stage_b(first, 0)
        stage_a(first + 2, 0)
        stage_b(first + 1, 1)

    def quad_body(i, carry):
        pair(4 * i - off)
        pair(4 * i + 2 - off)
        return carry

    def pair_body(pr, carry):
        pair(2 * pr - off)
        return carry

    n_quads = (n_pairs - 1) // 2
    lax.fori_loop(0, n_quads, quad_body, 0)
    lax.fori_loop(2 * n_quads, n_pairs - 1, pair_body, 0)
    stage_a(qi, 1, diag=True)
    stage_b(qi - 1, 0)
    stage_a_win(0, 0)
    stage_b(qi, 1)
    for wi in range(1, n_win):
        stage_a_win(wi, wi % 2)
        stage_b_win(wi - 1, (wi - 1) % 2)
    stage_b_win(n_win - 1, (n_win - 1) % 2)

    o_win_all = accw_scr[:ATT_DIM, :] / accw_scr[ATT_DIM:ATT_DIM + 1, :]
    o_slc_all = acc_scr[:ATT_DIM, :] / acc_scr[ATT_DIM:ATT_DIM + 1, :]
    for r in range(ATT_REP):
        o_win = o_win_all[:, r * tq:(r + 1) * tq]
        o_slc = o_slc_all[:, r * tq:(r + 1) * tq]
        o_cmp = ocmp_ref[0, r * ATT_DIM:(r + 1) * ATT_DIM, :]
        gbase = (g * ATT_REP + r) * 3
        g_cmp = _sigmoid(gt_ref[0, pl.ds(gbase, 1), :])
        g_slc = _sigmoid(gt_ref[0, pl.ds(gbase + 1, 1), :])
        g_win = _sigmoid(gt_ref[0, pl.ds(gbase + 2, 1), :])
        o = g_cmp * o_cmp + g_slc * o_slc + g_win * o_win
        o = o * lax.rsqrt(jnp.mean(o * o, axis=0, keepdims=True) + RMS_EPS)
        o = o * gn_ref[r * ATT_DIM:(r + 1) * ATT_DIM, :]
        o_ref[0, r * ATT_DIM:(r + 1) * ATT_DIM, :] = o.astype(o_ref.dtype)


def _slc_call(qt, ks, vst, kw, vwt, bias, ocmp, smallt, gn, tq):
    b, _, _, s = qt.shape
    nsel = bias.shape[3]
    gw = ATT_REP * ATT_DIM
    vres = pl.BlockSpec((1, ATT_DIM, s), lambda i, g, j: (i, g, 0))
    return pl.pallas_call(
        functools.partial(_slc_body, tq=tq),
        grid=(b, ATT_KV_HEADS, s // tq),
        in_specs=[pl.BlockSpec((1, ATT_REP, ATT_DIM, tq), lambda i, g, j: (i, g, 0, j)),
                  pl.BlockSpec((1, 1, s, LANES), lambda i, g, j: (i, g, 0, 0)), vres,
                  pl.BlockSpec((1, 1, s, ATT_DIM), lambda i, g, j: (i, g, 0, 0)), vres,
                  pl.BlockSpec((1, 1, 1, nsel, tq), lambda i, g, j: (i, g, j, 0, 0)),
                  pl.BlockSpec((1, gw, tq), lambda i, g, j: (i, g, j)),
                  pl.BlockSpec((1, LANES, tq), lambda i, g, j: (i, 0, j)),
                  pl.BlockSpec((gw, tq), lambda i, g, j: (g, 0))],
        out_specs=pl.BlockSpec((1, gw, tq), lambda i, g, j: (i, g, j)),
        out_shape=jax.ShapeDtypeStruct((b, ATT_WIDTH, s), MXU_DT),
        scratch_shapes=[pltpu.VMEM((LANES, ATT_REP * tq), MXU_DT),
                        pltpu.VMEM((LANES, ATT_REP * tq), MXU_DT),
                        pltpu.VMEM((tq, ATT_REP * tq), F32),
                        pltpu.VMEM((tq, ATT_REP * tq), F32),
                        pltpu.VMEM((1, ATT_REP * tq), F32),
                        pltpu.VMEM((1, ATT_REP * tq), F32),
                        pltpu.VMEM((1, ATT_REP * tq), F32),
                        pltpu.VMEM((ATT_DIM + 2 * SUBLANES, ATT_REP * tq), F32),
                        pltpu.VMEM((1, ATT_REP * tq), F32),
                        pltpu.VMEM((ATT_DIM + 2 * SUBLANES, ATT_REP * tq), F32)],
        compiler_params=_params(3),
        name="slc",
    )(qt, ks, vst, kw, vwt, bias, ocmp, smallt, gn)


def _mlstm_body(mqk_ref, mv_ref, mo_ref, sm_ref, smt_ref, cw_ref, cb_ref, gbr_ref, gbc_ref, ng_ref,
                o_ref, c_scr, m_scr, tail_scr, *, chunk):
    @pl.when(pl.program_id(1) == 0)
    def _():
        c_scr[...] = jnp.zeros(c_scr.shape, F32)
        m_scr[...] = jnp.full(m_scr.shape, NEG, F32)
        tail_scr[...] = jnp.zeros(tail_scr.shape, F32)

    x = mqk_ref[0].astype(F32)
    tail = tail_scr[...]
    y = cb_ref[...] + x * cw_ref[M_CONV - 1:M_CONV, :]
    for k in range(1, M_CONV):
        y = y + _shift_rows(x, tail, k) * cw_ref[M_CONV - 1 - k:M_CONV - k, :]
    tail_scr[...] = x[chunk - SUBLANES:, :]
    qk = _silu(y)
    sm = sm_ref[0] + gbr_ref[...]
    smt = smt_ref[0] + gbc_ref[...]
    row = lax.broadcasted_iota(jnp.int32, (chunk, chunk), 0)
    col = lax.broadcasted_iota(jnp.int32, (chunk, chunk), 1)
    causal = row >= col
    ones = jnp.ones((chunk, M_DIM), F32)

    def log_sigmoid(z):
        return jnp.minimum(z, 0.0) - jnp.log(1.0 + jnp.exp(-jnp.abs(z)))

    for h in range(M_HEADS):
        q = qk[:, h * M_DIM:(h + 1) * M_DIM]
        k = qk[:, M_WIDTH + h * M_DIM:M_WIDTH + (h + 1) * M_DIM] * (M_DIM ** -0.5)
        v = mv_ref[0, :, h * M_DIM:(h + 1) * M_DIM]
        i_col = sm[:, SMALL_I + h:SMALL_I + h + 1]
        lf_col = log_sigmoid(sm[:, SMALL_F + h:SMALL_F + h + 1])
        i_row = smt[SMALL_I + h:SMALL_I + h + 1, :]
        lf_row = log_sigmoid(smt[SMALL_F + h:SMALL_F + h + 1, :])
        b_col = jnp.sum(jnp.where(causal, lf_row, 0.0), axis=1, keepdims=True)
        b_row = jnp.sum(jnp.where(row <= col, lf_col, 0.0), axis=0, keepdims=True)
        m_prev = m_scr[h]
        dmat = jnp.where(causal, b_col - b_row + i_row, NEG)
        inter = b_col + m_prev
        mt = jnp.maximum(inter, jnp.max(dmat, axis=1, keepdims=True))
        w_intra = jnp.exp(dmat - mt)
        w_inter = jnp.exp(inter - mt)
        qkw = _dot_nt(q, k) * w_intra
        c_aug = c_scr[h]
        r1 = _dot(q, c_aug)
        num = w_inter * r1[:, :M_DIM] + _dot(qkw, v)
        den = w_inter * r1[:, M_DIM:M_DIM + 1] + jnp.sum(qkw, axis=1, keepdims=True)
        hh = num / jnp.maximum(jnp.abs(den), jnp.exp(-mt))
        b_last = b_col[chunk - 1:chunk, :]
        a_col = b_last - b_col + i_col
        m_new = jnp.maximum(b_last + m_prev, jnp.max(a_col, axis=0, keepdims=True))
        ws = jnp.exp(a_col - m_new)
        decay = jnp.exp(b_last + m_prev - m_new)
        v_aug = jnp.concatenate([v.astype(F32), ones], axis=1)
        c_scr[h] = decay * c_aug + _dot_tn(k * ws, v_aug)
        m_scr[h] = m_new
        hm = _sigmoid(mo_ref[0, :, h * M_DIM:(h + 1) * M_DIM].astype(F32)) * hh
        hm = hm * lax.rsqrt(jnp.mean(hm * hm, axis=1, keepdims=True) + RMS_EPS)
        o_ref[0, :, h * M_DIM:(h + 1) * M_DIM] = (hm * ng_ref[:, h * M_DIM:(h + 1) * M_DIM]).astype(o_ref.dtype)


def _mlstm_call(mqk, mv, mo, small, smallt, cw, cb, gbr, gbc, ng, chunk):
    b, s, _ = mv.shape
    row = lambda n: pl.BlockSpec((1, chunk, n), lambda i, j: (i, j, 0))
    full = lambda a: pl.BlockSpec(a.shape, lambda i, j: (0,) * a.ndim)
    return pl.pallas_call(
        functools.partial(_mlstm_body, chunk=chunk),
        grid=(b, s // chunk),
        in_specs=[row(2 * M_WIDTH), row(M_WIDTH), row(M_WIDTH), row(LANES),
                  pl.BlockSpec((1, LANES, chunk), lambda i, j: (i, 0, j)),
                  full(cw), full(cb), full(gbr), full(gbc), full(ng)],
        out_specs=row(M_WIDTH),
        out_shape=jax.ShapeDtypeStruct((b, s, M_WIDTH), MXU_DT),
        scratch_shapes=[pltpu.VMEM((M_HEADS, M_DIM, 2 * M_DIM), F32),
                        pltpu.VMEM((M_HEADS, 1, 1), F32),
                        pltpu.VMEM((SUBLANES, 2 * M_WIDTH), F32)],
        compiler_params=_params(2),
        name="mlstm",
    )(mqk, mv, mo, small, smallt, cw, cb, gbr, gbc, ng)


def _mix_ffn_body(x_ref, attt_ref, hm_ref, wo_ref, g1_ref, sc_ref, sh_ref, g2_ref, ng_ref,
                  wup_ref, cw_ref, cb_ref, wdn_ref, fg_ref, o_ref, tail_scr, g_scr, *, d_ff, tf, tiles_per_seq, final):
    @pl.when(pl.program_id(0) % tiles_per_seq == 0)
    def _():
        tail_scr[...] = jnp.zeros(tail_scr.shape, F32)

    d = x_ref.shape[-1]
    mix = _dot_tn(attt_ref[0], wo_ref[:ATT_WIDTH, :]) + _dot(hm_ref[0], wo_ref[ATT_WIDTH:, :])
    x = x_ref[0] + g1_ref[0] * mix
    var = jnp.mean(x * x, axis=-1, keepdims=True)
    h = (x * lax.rsqrt(var + RMS_EPS)) * ng_ref[...]
    hb = (h * (1.0 + sc_ref[0]) + sh_ref[0]).astype(MXU_DT)
    for c in range(d_ff // tf):
        a = jnp.dot(hb, wup_ref[:, c * tf:(c + 1) * tf], preferred_element_type=F32)
        v = jnp.dot(hb, wup_ref[:, d_ff + c * tf:d_ff + (c + 1) * tf], preferred_element_type=F32)
        tail = tail_scr[c]
        y = cb_ref[:, c * tf:(c + 1) * tf] + a * cw_ref[FFN_CONV - 1:FFN_CONV, c * tf:(c + 1) * tf]
        for k in range(1, FFN_CONV):
            y = y + _shift_rows(a, tail, k) * cw_ref[FFN_CONV - 1 - k:FFN_CONV - k, c * tf:(c + 1) * tf]
        tail_scr[c] = a[a.shape[0] - SUBLANES:, :]
        g_scr[:, c * tf:(c + 1) * tf] = (_silu(y) * v).astype(g_scr.dtype)
    out = x + g2_ref[0] * jnp.dot(g_scr[...], wdn_ref[...], preferred_element_type=F32)
    if final:
        var = jnp.mean(out * out, axis=-1, keepdims=True)
        out = (out * lax.rsqrt(var + RMS_EPS)) * fg_ref[...]
    o_ref[0] = out


def _mix_ffn_call(x, attt, hm, wo, g1, sc, sh, g2, ng, wup, cw, cb, wdn, fg, tm, tf, final):
    b, s, d = x.shape
    d_ff = wdn.shape[0]
    tiles_per_seq = s // tm
    row = lambda n: pl.BlockSpec((1, tm, n), lambda i: (i // tiles_per_seq, i % tiles_per_seq, 0))
    vec = pl.BlockSpec((1, 1, d), lambda i: (i // tiles_per_seq, 0, 0))
    full = lambda a: pl.BlockSpec(a.shape, lambda i: (0,) * a.ndim)
    return pl.pallas_call(
        functools.partial(_mix_ffn_body, d_ff=d_ff, tf=tf, tiles_per_seq=tiles_per_seq, final=final),
        grid=(b * tiles_per_seq,),
        in_specs=[row(d),
                  pl.BlockSpec((1, ATT_WIDTH, tm), lambda i: (i // tiles_per_seq, 0, i % tiles_per_seq)),
                  row(M_WIDTH), full(wo), vec, vec, vec, vec, full(ng),
                  full(wup), full(cw), full(cb), full(wdn), full(fg)],
        out_specs=row(d),
        out_shape=jax.ShapeDtypeStruct((b, s, d), F32),
        scratch_shapes=[pltpu.VMEM((d_ff // tf, SUBLANES, tf), F32),
                        pltpu.VMEM((tm, d_ff), MXU_DT)],
        compiler_params=_params(1),
        name="mix_ffn",
    )(x, attt, hm, wo, g1, sc, sh, g2, ng, wup, cw, cb, wdn, fg)


def _pick_tile(n, pref):
    t = min(n, pref)
    assert n % t == 0, (n, t)
    return t


def _pack_w_in(w_in):
    att_cols = ATT_WIDTH + 6 * KV_WIDTH
    n_gate = 3 * ATT_HEADS
    o = att_cols
    gates = w_in[:, o:o + n_gate]; o += n_gate
    mqk = w_in[:, o:o + 2 * M_WIDTH]; o += 2 * M_WIDTH
    mv = w_in[:, o:o + M_WIDTH]; o += M_WIDTH
    mif = w_in[:, o:o + 2 * M_HEADS]; o += 2 * M_HEADS
    mo = w_in[:, o:o + M_WIDTH]
    pad = jnp.zeros((w_in.shape[0], LANES - n_gate - 2 * M_HEADS), w_in.dtype)
    return jnp.concatenate([w_in[:, :att_cols], mqk, mv, mo, gates, mif, pad], axis=1).astype(MXU_DT)


def _expand_cmp_weights(pos, w1, w2):
    half = CMP_BLOCK // 2
    w1r = w1.reshape(CMP_BLOCK, ATT_DIM, ATT_DIM)
    eye = jnp.eye(ATT_KV_HEADS, dtype=w1.dtype)

    def expand(wl):
        z = wl[:, None, :, None, :] * eye[None, :, None, :, None]
        return z.reshape(half * KV_WIDTH, KV_WIDTH).astype(MXU_DT)

    def expand_pos(pl_):
        return jnp.broadcast_to(pl_[:, None, :], (half, ATT_KV_HEADS, ATT_DIM)).reshape(1, half * KV_WIDTH)

    w2e = (w2[None, :, None, :] * eye[:, None, :, None]).reshape(KV_WIDTH, KV_WIDTH).astype(MXU_DT)
    return expand_pos(pos[:half]), expand_pos(pos[half:]), expand(w1r[:half]), expand(w1r[half:]), w2e


def kernel(x, c, positions, norm1_g, norm2_g, ada_w, ada_b, w_in, cmp_pos, cmp_w1, cmp_w2, att_norm_g,
           m_conv_w, m_conv_b, m_gate_b, m_norm_g, w_out, ffn_up, ffn_conv_w, ffn_conv_b, ffn_down, final_g):
    b, s, d = x.shape
    depth = ada_w.shape[0]
    d_ff = ffn_down.shape[1]
    assert s % (CMP_STRIDE * SUBLANES) == 0 and d % LANES == 0 and b <= SUBLANES
    tm_proj = _pick_tile(s, 512)
    tm_ffn = _pick_tile(s, 512)
    tq = _pick_tile(s, 256)
    chunk = _pick_tile(s, 256)
    tf = 256
    assert d_ff % tf == 0 and WINDOW % tq == 0 and s >= WINDOW + tq
    nc = s // CMP_STRIDE
    nsel = s // SEL_BLOCK
    k_top = min(SEL_TOPK, nsel)

    c8 = jnp.zeros((SUBLANES, d), F32).at[:b].set(c)
    mod = _ada_call(c8, ada_w, ada_b.reshape(depth, 1, 6 * d), _pick_tile(6 * d, 1536))

    def mod_vec(l, k):
        return mod[l, :b, k * d:(k + 1) * d].reshape(b, 1, d)

    lane_d = jnp.arange(LANES) % ATT_DIM
    inv_freq = ROPE_THETA ** (-(2 * (lane_d % ROPE_HALF)).astype(F32) / ROPE_DIM)
    rc, rs1, rs2 = _rope_call(positions.reshape(b, s, 1), inv_freq.reshape(1, LANES), _pick_tile(s, 1024))

    c_start = jnp.arange(nc)[None, :] * CMP_STRIDE
    s_start = jnp.arange(nsel)[:, None] * SEL_BLOCK
    ovt = ((c_start <= s_start + SEL_BLOCK - 1) & (c_start + CMP_BLOCK - 1 >= s_start)
           & (jnp.arange(nc)[None, :] < nc - 1)).astype(MXU_DT)

    for l in range(depth):
        outs = _proj_call(x, mod_vec(l, 1), mod_vec(l, 0), norm1_g[l].reshape(1, d), _pack_w_in(w_in[l]),
                          rc, rs1, rs2, tm_proj)
        q, kc, vc, ks, vst, kw, vwt, mqk, mv, mo, small, smallt = outs
        kcmp, _ = _compress_call(kc.reshape(b, nc, CMP_STRIDE * KV_WIDTH),
                                 *_expand_cmp_weights(cmp_pos[l, 0], cmp_w1[l, 0], cmp_w2[l, 0]))
        _, vcmpt = _compress_call(vc.reshape(b, nc, CMP_STRIDE * KV_WIDTH),
                                  *_expand_cmp_weights(cmp_pos[l, 1], cmp_w1[l, 1], cmp_w2[l, 1]))
        ocmp, bias = _cmp_call(q, kcmp, vcmpt, ovt, tq, k_top)
        gn = jnp.broadcast_to(att_norm_g[l][:, None], (ATT_WIDTH, tq))
        attt = _slc_call(q, ks, vst, kw, vwt, bias, ocmp, smallt, gn, tq)
        gbr = jnp.zeros((1, LANES), F32).at[0, SMALL_I:SMALL_I + 2 * M_HEADS].set(m_gate_b[l])
        hm = _mlstm_call(mqk, mv, mo, small, smallt, m_conv_w[l], m_conv_b[l].reshape(1, -1),
                         gbr, gbr.reshape(LANES, 1), m_norm_g[l].reshape(1, -1), chunk)
        x = _mix_ffn_call(x, attt, hm, w_out[l].astype(MXU_DT), mod_vec(l, 2), mod_vec(l, 4), mod_vec(l, 3),
                          mod_vec(l, 5), norm2_g[l].reshape(1, d), ffn_up[l].astype(MXU_DT), ffn_conv_w[l],
                          ffn_conv_b[l].reshape(1, -1), ffn_down[l].astype(MXU_DT), final_g.reshape(1, d),
                          tm_ffn, tf, l == depth - 1)
    return x
```

```python
import functools

import jax
import jax.numpy as jnp
from jax import lax
from jax.experimental import pallas as pl
from jax.experimental.pallas import tpu as pltpu

F32 = jnp.float32
BF16 = jnp.bfloat16
MXU_DT = jnp.bfloat16

ATT_DIM = 64
ATT_HEADS = 8
ATT_KV_HEADS = 2
ATT_REP = ATT_HEADS // ATT_KV_HEADS
ATT_WIDTH = ATT_HEADS * ATT_DIM
KV_WIDTH = ATT_KV_HEADS * ATT_DIM
ROPE_DIM = ATT_DIM // 4
ROPE_HALF = ROPE_DIM // 2
ROPE_THETA = 500000.0
CMP_BLOCK = 32
CMP_STRIDE = 16
SEL_BLOCK = 64
SEL_TOPK = 16
WINDOW = 512
FORCE_BONUS = 1.0e4
M_HEADS = 4
M_DIM = 128
M_WIDTH = M_HEADS * M_DIM
M_CONV = 4
FFN_CONV = 3
RMS_EPS = 1e-6
NEG = -1e30
Q_SCALE = ATT_DIM ** -0.5 * 1.4426950408889634
BIAS_GROUP = 8

LANES = 128
SUBLANES = 8
VMEM_LIMIT = 56 * 1024 * 1024

COL_Q = 0
COL_KC = 512
COL_VC = 640
COL_KS = 768
COL_VS = 896
COL_KW = 1024
COL_VW = 1152
COL_MQK = 1280
COL_MV = 2304
COL_MO = 2816
COL_SMALL = 3328
IN_COLS_PACKED = 3456
SMALL_I = 3 * ATT_HEADS
SMALL_F = SMALL_I + M_HEADS

_NT = (((1,), (1,)), ((), ()))
_TN = (((0,), (0,)), ((), ()))


def _params(n_axes):
    return pltpu.CompilerParams(dimension_semantics=("arbitrary",) * n_axes,
                                vmem_limit_bytes=VMEM_LIMIT)


def _dot(a, b):
    return jnp.dot(a.astype(MXU_DT), b.astype(MXU_DT), preferred_element_type=F32)


def _dot_nt(a, b):
    return lax.dot_general(a.astype(MXU_DT), b.astype(MXU_DT), _NT, preferred_element_type=F32)


def _dot_tn(a, b):
    return lax.dot_general(a.astype(MXU_DT), b.astype(MXU_DT), _TN, preferred_element_type=F32)


def _sigmoid(x):
    return 1.0 / (1.0 + jnp.exp(-x))


def _silu(x):
    return x * _sigmoid(x)


def _shift_rows(x, prev8, k):
    rolled = pltpu.roll(x, k, 0)
    fix = pltpu.roll(prev8, k, 0)
    row = lax.broadcasted_iota(jnp.int32, fix.shape, 0)
    top = jnp.where(row < k, fix, rolled[:SUBLANES])
    return jnp.concatenate([top, rolled[SUBLANES:]], axis=0)


def _ada_body(c_ref, w_ref, b_ref, o_ref):
    c = c_ref[...]
    o_ref[0] = _dot(_silu(c), w_ref[0]) + b_ref[0]


def _ada_call(c8, ada_w, ada_b3, tn):
    depth, d, n = ada_w.shape
    return pl.pallas_call(
        _ada_body,
        grid=(depth, n // tn),
        in_specs=[pl.BlockSpec((SUBLANES, d), lambda l, j: (0, 0)),
                  pl.BlockSpec((1, d, tn), lambda l, j: (l, 0, j)),
                  pl.BlockSpec((1, 1, tn), lambda l, j: (l, 0, j))],
        out_specs=pl.BlockSpec((1, SUBLANES, tn), lambda l, j: (l, 0, j)),
        out_shape=jax.ShapeDtypeStruct((depth, SUBLANES, n), F32),
        compiler_params=_params(2),
        name="ada",
    )(c8, ada_w, ada_b3)


def _rope_body(pos_ref, freq_ref, rc_ref, rs1_ref, rs2_ref):
    pos = pos_ref[0].astype(F32)
    ang = pos * freq_ref[...]
    d = lax.broadcasted_iota(jnp.int32, ang.shape, 1) % ATT_DIM
    cos = jnp.cos(ang)
    sin = jnp.sin(ang)
    rc_ref[0] = jnp.where(d < ROPE_DIM, cos, 1.0)
    rs1_ref[0] = jnp.where(d < ROPE_HALF, -sin, 0.0)
    rs2_ref[0] = jnp.where((d >= ROPE_HALF) & (d < ROPE_DIM), sin, 0.0)


def _rope_call(pos3, freq, tm):
    b, s, _ = pos3.shape
    spec = pl.BlockSpec((1, tm, LANES), lambda i, j: (i, j, 0))
    shp = jax.ShapeDtypeStruct((b, s, LANES), F32)
    return pl.pallas_call(
        _rope_body,
        grid=(b, s // tm),
        in_specs=[pl.BlockSpec((1, tm, 1), lambda i, j: (i, j, 0)),
                  pl.BlockSpec((1, LANES), lambda i, j: (0, 0))],
        out_specs=[spec, spec, spec],
        out_shape=[shp, shp, shp],
        compiler_params=_params(2),
        name="rope",
    )(pos3, freq)


def _proj_body(x_ref, sc_ref, sh_ref, g_ref, w_ref, rc_ref, rs1_ref, rs2_ref,
               q_ref, kc_ref, vc_ref, ks_ref, vst_ref, kw_ref, vwt_ref,
               mqk_ref, mv_ref, mo_ref, sm_ref, smt_ref):
    x = x_ref[0]
    var = jnp.mean(x * x, axis=-1, keepdims=True)
    h = (x * lax.rsqrt(var + RMS_EPS)) * g_ref[...]
    h = h * (1.0 + sc_ref[0]) + sh_ref[0]
    hb = h.astype(MXU_DT)

    def mm(c0, n):
        return jnp.dot(hb, w_ref[:, c0:c0 + n], preferred_element_type=F32)

    rc, rs1, rs2 = rc_ref[0], rs1_ref[0], rs2_ref[0]

    def rope(y):
        return (y * rc + pltpu.roll(y, LANES - ROPE_HALF, 1) * rs1
                + pltpu.roll(y, ROPE_HALF, 1) * rs2)

    for j in range(ATT_HEADS // 4):
        y2 = mm(COL_Q + 2 * LANES * j, 2 * LANES)
        for i in range(2):
            yt = (rope(y2[:, LANES * i:LANES * (i + 1)]) * Q_SCALE).T
            q_ref[0, 4 * j + 2 * i] = yt[:ATT_DIM].astype(q_ref.dtype)
            q_ref[0, 4 * j + 2 * i + 1] = yt[ATT_DIM:].astype(q_ref.dtype)
    y2 = mm(COL_KC, 2 * KV_WIDTH)
    kc_ref[0] = rope(y2[:, :KV_WIDTH])
    vc_ref[0] = y2[:, KV_WIDTH:]
    y2 = mm(COL_KS, 2 * KV_WIDTH)
    y = rope(y2[:, :KV_WIDTH])
    lane = lax.broadcasted_iota(jnp.int32, y.shape, 1)
    t = pl.program_id(1) * y.shape[0] + lax.broadcasted_iota(jnp.int32, y.shape, 0)
    onehot = jnp.where(lane - ATT_DIM == (t // SEL_BLOCK) % BIAS_GROUP, 1.0, 0.0)
    ks_ref[0, 0] = jnp.where(lane < ATT_DIM, y, onehot).astype(ks_ref.dtype)
    ks_ref[0, 1] = jnp.where(lane < ATT_DIM, pltpu.roll(y, ATT_DIM, 1), onehot).astype(ks_ref.dtype)
    vst_ref[0] = y2[:, KV_WIDTH:].T.astype(vst_ref.dtype)
    y2 = mm(COL_KW, 2 * KV_WIDTH)
    y = rope(y2[:, :KV_WIDTH])
    kw_ref[0, 0] = y[:, :ATT_DIM].astype(kw_ref.dtype)
    kw_ref[0, 1] = y[:, ATT_DIM:].astype(kw_ref.dtype)
    vwt_ref[0] = y2[:, KV_WIDTH:].T.astype(vwt_ref.dtype)
    for j in range(2 * M_WIDTH // 256):
        mqk_ref[0, :, 256 * j:256 * (j + 1)] = mm(COL_MQK + 256 * j, 256).astype(mqk_ref.dtype)
    for j in range(M_WIDTH // 256):
        mv_ref[0, :, 256 * j:256 * (j + 1)] = mm(COL_MV + 256 * j, 256).astype(mv_ref.dtype)
        mo_ref[0, :, 256 * j:256 * (j + 1)] = mm(COL_MO + 256 * j, 256).astype(mo_ref.dtype)
    sm = mm(COL_SMALL, LANES)
    sm_ref[0] = sm
    smt_ref[0] = sm.T


def _proj_call(x, sc, sh, g, w, layer, rc, rs1, rs2, tm):
    b, s, d = x.shape
    row = lambda n: pl.BlockSpec((1, tm, n), lambda i, j: (i, j, 0))
    vec = pl.BlockSpec((1, 1, d), lambda i, j: (i, 0, 0))
    headed = lambda n, w: pl.BlockSpec((1, n, tm, w), lambda i, j: (i, 0, j, 0))
    tposed = pl.BlockSpec((1, LANES, tm), lambda i, j: (i, 0, j))
    sds = jax.ShapeDtypeStruct
    return pl.pallas_call(
        _proj_body,
        grid=(b, s // tm),
        in_specs=[row(d), vec, vec,
                  pl.BlockSpec((1, d), lambda i, j: (0, 0)),
                  pl.BlockSpec((None, d, IN_COLS_PACKED), lambda i, j: (layer, 0, 0)),
                  row(LANES), row(LANES), row(LANES)],
        out_specs=[pl.BlockSpec((1, ATT_HEADS, ATT_DIM, tm), lambda i, j: (i, 0, 0, j)),
                   row(KV_WIDTH), row(KV_WIDTH),
                   headed(ATT_KV_HEADS, LANES), tposed, headed(ATT_KV_HEADS, ATT_DIM), tposed,
                   row(2 * M_WIDTH), row(M_WIDTH), row(M_WIDTH), row(LANES), tposed],
        out_shape=[sds((b, ATT_HEADS, ATT_DIM, s), MXU_DT),
                   sds((b, s, KV_WIDTH), F32), sds((b, s, KV_WIDTH), F32),
                   sds((b, ATT_KV_HEADS, s, LANES), MXU_DT), sds((b, KV_WIDTH, s), MXU_DT),
                   sds((b, ATT_KV_HEADS, s, ATT_DIM), MXU_DT), sds((b, KV_WIDTH, s), MXU_DT),
                   sds((b, s, 2 * M_WIDTH), MXU_DT), sds((b, s, M_WIDTH), MXU_DT),
                   sds((b, s, M_WIDTH), MXU_DT), sds((b, s, LANES), F32), sds((b, LANES, s), F32)],
        compiler_params=_params(2),
        name="proj",
    )(x, sc, sh, g, w, rc, rs1, rs2)


def _compress_body(kr_ref, pa_ref, pb_ref, w1a_ref, w1b_ref, w2_ref, out_ref, outt_ref):
    kr = kr_ref[0]
    nc = kr.shape[0]
    ya = _dot(kr + pa_ref[...], w1a_ref[...])
    yb = _dot(kr + pb_ref[...], w1b_ref[...])
    pre = ya + pltpu.roll(yb, nc - 1, 0)
    out = _dot(_silu(pre), w2_ref[...])
    row = lax.broadcasted_iota(jnp.int32, out.shape, 0)
    out = jnp.where(row < nc - 1, out, 0.0)
    out_ref[0, 0] = out[:, :ATT_DIM].astype(out_ref.dtype)
    out_ref[0, 1] = out[:, ATT_DIM:].astype(out_ref.dtype)
    outt_ref[0] = out.T.astype(outt_ref.dtype)


def _compress_call(kr, pa, pb, w1a, w1b, w2e):
    b, nc, width = kr.shape
    full = lambda a: pl.BlockSpec(a.shape, lambda i: (0,) * a.ndim)
    return pl.pallas_call(
        _compress_body,
        grid=(b,),
        in_specs=[pl.BlockSpec((1, nc, width), lambda i: (i, 0, 0)),
                  full(pa), full(pb), full(w1a), full(w1b), full(w2e)],
        out_specs=[pl.BlockSpec((1, ATT_KV_HEADS, nc, ATT_DIM), lambda i: (i, 0, 0, 0)),
                   pl.BlockSpec((1, KV_WIDTH, nc), lambda i: (i, 0, 0))],
        out_shape=[jax.ShapeDtypeStruct((b, ATT_KV_HEADS, nc, ATT_DIM), MXU_DT),
                   jax.ShapeDtypeStruct((b, KV_WIDTH, nc), MXU_DT)],
        compiler_params=_params(1),
        name="compress",
    )(kr, pa, pb, w1a, w1b, w2e)


def _heads_on_lanes(qt_ref):
    return jnp.concatenate([qt_ref[0, r] for r in range(ATT_REP)], axis=1)


def _tile_heads(a):
    return jnp.concatenate([a] * ATT_REP, axis=1)


def _with_ones_row(vt):
    n = vt.shape[1]
    extra = jnp.where(lax.broadcasted_iota(jnp.int32, (2 * SUBLANES, n), 0) == 0, 1.0, 0.0)
    return jnp.concatenate([vt, extra.astype(vt.dtype)], axis=0)


def _cmp_body(qt_ref, kc_ref, vct_ref, ovt_ref, ocmp_ref, bias_ref, *, tq, k_top, n_split):
    t0 = pl.program_id(2) * tq
    nc_all = kc_ref.shape[2]
    nsel_all = ovt_ref.shape[0]
    rows_per_sel = SEL_BLOCK // CMP_STRIDE

    def compute(nc):
        nsel = nc // rows_per_sel
        row = lax.broadcasted_iota(jnp.int32, (nc, tq), 0)
        tpos = t0 + lax.broadcasted_iota(jnp.int32, (nc, tq), 1)
        mbias = jnp.where(row * CMP_STRIDE + (CMP_BLOCK - 1) <= tpos, 0.0, NEG)
        s = jnp.dot(kc_ref[0, 0, :nc, :], _heads_on_lanes(qt_ref), preferred_element_type=F32)
        s = s + _tile_heads(mbias)
        m = jnp.max(s, axis=0, keepdims=True)
        m = jnp.where(m > 0.5 * NEG, m, 0.0)
        p = jnp.exp2(s - m).astype(MXU_DT)
        oa = jnp.dot(_with_ones_row(vct_ref[0, :, :nc]), p, preferred_element_type=F32)
        d = oa[ATT_DIM:ATT_DIM + 1]
        inv = 1.0 / jnp.where(d > 0.0, d, 1.0)
        o = oa[:ATT_DIM] * inv
        impa = jnp.dot(ovt_ref[:nsel, :nc], p, preferred_element_type=F32) * inv
        imp = impa[:, :tq]
        for r in range(ATT_REP):
            ocmp_ref[0, r * ATT_DIM:(r + 1) * ATT_DIM, :] = o[:, r * tq:(r + 1) * tq]
            if r:
                imp = imp + impa[:, r * tq:(r + 1) * tq]
        jidx = lax.broadcasted_iota(jnp.int32, (nsel, tq), 0)
        tq_pos = t0 + lax.broadcasted_iota(jnp.int32, (nsel, tq), 1)
        jcur = tq_pos // SEL_BLOCK
        forced = (jidx == 0) | (jidx == jcur) | (jidx == jcur - 1)
        score = jnp.where(jidx * SEL_BLOCK <= tq_pos, imp + jnp.where(forced, FORCE_BONUS, 0.0), NEG)
        valid = score > 0.5 * NEG
        rounds = min(k_top, nsel)
        if nsel < nsel_all:
            bias_ref[0, 0, 0, nsel:, :] = jnp.full((nsel_all - nsel, tq), NEG, F32)

        fast = score
        for _ in range(rounds):
            fast = jnp.where(fast == jnp.max(fast, axis=0, keepdims=True), NEG, fast)
        picked = valid & (fast < 0.5 * NEG)
        bias_ref[0, 0, 0, :nsel, :] = jnp.where(picked, 0.0, NEG)
        n_picked = jnp.sum(jnp.where(picked, 1.0, 0.0), axis=0, keepdims=True)
        n_valid = jnp.sum(jnp.where(valid, 1.0, 0.0), axis=0, keepdims=True)
        mismatch = jnp.max(jnp.abs(n_picked - jnp.minimum(n_valid, float(rounds))))

        @pl.when(mismatch > 0.0)
        def _():
            slow = score
            for _ in range(rounds):
                cur = jnp.max(slow, axis=0, keepdims=True)
                first = jnp.min(jnp.where(slow == cur, jidx, nsel), axis=0, keepdims=True)
                slow = jnp.where(jidx == first, NEG, slow)
            bias_ref[0, 0, 0, :nsel, :] = jnp.where(valid & (slow < 0.5 * NEG), 0.0, NEG)

    chunk_pos = nc_all // n_split * CMP_STRIDE
    need = jnp.minimum((t0 + tq - 1) // chunk_pos + 1, n_split)
    for k in range(1, n_split + 1):
        pl.when(need == k)(functools.partial(compute, nc_all // n_split * k))


def _cmp_call(qt, kcmp, vcmpt, ovt, tq, k_top):
    b, _, _, s = qt.shape
    nc = kcmp.shape[2]
    nsel = ovt.shape[0]
    nq = s // tq
    n_split = 4 if nc % (4 * LANES) == 0 and (nc // 4 * CMP_STRIDE) % tq == 0 else 1
    return pl.pallas_call(
        functools.partial(_cmp_body, tq=tq, k_top=k_top, n_split=n_split),
        grid=(b, ATT_KV_HEADS, nq),
        in_specs=[pl.BlockSpec((1, ATT_REP, ATT_DIM, tq), lambda i, g, j: (i, g, 0, j)),
                  pl.BlockSpec((1, 1, nc, ATT_DIM), lambda i, g, j: (i, g, 0, 0)),
                  pl.BlockSpec((1, ATT_DIM, nc), lambda i, g, j: (i, g, 0)),
                  pl.BlockSpec((nsel, nc), lambda i, g, j: (0, 0))],
        out_specs=[pl.BlockSpec((1, ATT_REP * ATT_DIM, tq), lambda i, g, j: (i, g, j)),
                   pl.BlockSpec((1, 1, 1, nsel, tq), lambda i, g, j: (i, g, j, 0, 0))],
        out_shape=[jax.ShapeDtypeStruct((b, ATT_WIDTH, s), F32),
                   jax.ShapeDtypeStruct((b, ATT_KV_HEADS, nq, nsel, tq), F32)],
        compiler_params=_params(3),
        name="cmp",
    )(qt, kcmp, vcmpt, ovt)


def _slc_body(qt_ref, ks_ref, vst_ref, kw_ref, vwt_ref, bias_ref, ocmp_ref, gt_ref, gn_ref,
              o_ref, qa0_scr, qa1_scr, s0_scr, s1_scr, mx0_scr, mx1_scr, m_scr, acc_scr,
              mw_scr, accw_scr, *, tq):
    qa_scr = (qa0_scr, qa1_scr)
    s_scr = (s0_scr, s1_scr)
    mx_scr = (mx0_scr, mx1_scr)
    g = pl.program_id(1)
    qi = pl.program_id(2)
    t0 = qi * tq
    blocks_per_tile = tq // SEL_BLOCK
    qt = _heads_on_lanes(qt_ref)
    for qa in qa_scr:
        qa[:ATT_DIM, :] = qt
        qa[ATT_DIM:, :] = jnp.zeros((LANES - ATT_DIM, qa.shape[1]), qa.dtype)
    for m_ref, a_ref in ((m_scr, acc_scr), (mw_scr, accw_scr)):
        m_ref[...] = jnp.full(m_ref.shape, NEG, F32)
        a_ref[...] = jnp.zeros(a_ref.shape, F32)

    def stage_a(kv, slot, dummy=None, diag=False):
        kvc = jnp.maximum(kv, 0)
        k0 = pl.multiple_of(kvc * tq, tq)
        grp = pl.multiple_of((kvc * blocks_per_tile) // BIAS_GROUP * BIAS_GROUP, BIAS_GROUP)
        b8 = bias_ref[0, 0, 0, pl.ds(grp, BIAS_GROUP), :]
        if dummy is not None:
            b8 = jnp.where(dummy, NEG, b8)
        b16 = jnp.concatenate([b8, jnp.zeros_like(b8)], axis=0)
        qa = qa_scr[slot]
        qa[ATT_DIM:ATT_DIM + 2 * SUBLANES, :] = _tile_heads(b16).astype(qa.dtype)
        s = jnp.dot(ks_ref[0, 0, pl.ds(k0, tq), :], qa[...], preferred_element_type=F32)
        if diag:
            row = lax.broadcasted_iota(jnp.int32, (tq, tq), 0)
            lane = lax.broadcasted_iota(jnp.int32, (tq, tq), 1)
            s = s + _tile_heads(jnp.where(row <= lane, 0.0, NEG))
        s_scr[slot][...] = s
        mx_scr[slot][...] = jnp.max(s, axis=0, keepdims=True)

    def softmax_pv(slot, v_ref, k0, m_ref, acc_ref):
        m_old = m_ref[...]
        m_new = jnp.maximum(m_old, mx_scr[slot][...])
        alpha = jnp.exp2(m_old - m_new)
        p = jnp.exp2(s_scr[slot][...] - m_new).astype(MXU_DT)
        vt = _with_ones_row(v_ref[0, :, pl.ds(k0, tq)])
        acc_ref[...] = alpha * acc_ref[...] + jnp.dot(vt, p, preferred_element_type=F32)
        m_ref[...] = m_new

    def stage_b(kv, slot):
        softmax_pv(slot, vst_ref, pl.multiple_of(jnp.maximum(kv, 0) * tq, tq), m_scr, acc_scr)

    n_win = WINDOW // tq + 1
    wrow = lax.broadcasted_iota(jnp.int32, (tq, tq), 0)
    wlane = lax.broadcasted_iota(jnp.int32, (tq, tq), 1)

    def win_start(wi):
        return t0 - WINDOW + wi * tq

    def stage_a_win(wi, slot):
        start = win_start(wi)
        k0 = pl.multiple_of(jnp.maximum(start, 0), tq)
        s = jnp.dot(kw_ref[0, 0, pl.ds(k0, tq), :], qt, preferred_element_type=F32)
        if wi == 0:
            visible = (wrow > wlane) & (start >= 0)
        elif wi == n_win - 1:
            visible = wrow <= wlane
        else:
            visible = jnp.broadcast_to(start >= 0, (tq, tq))
        s = s + _tile_heads(jnp.where(visible, 0.0, NEG))
        s_scr[slot][...] = s
        mx_scr[slot][...] = jnp.max(s, axis=0, keepdims=True)

    def stage_b_win(wi, slot):
        softmax_pv(slot, vwt_ref, pl.multiple_of(jnp.maximum(win_start(wi), 0), tq), mw_scr, accw_scr)

    off = 1 - qi % 2
    n_pairs = (qi + 2) // 2
    stage_a(-off, 0, dummy=off == 1)

    def pair(first):
        stage_a(first + 1, 1)
        stage_b(first, 0)
        stage_a(first + 2, 0)
        stage_b(first + 1, 1)

    def quad_body(i, carry):
        pair(4 * i - off)
        pair(4 * i + 2 - off)
        return carry

    def pair_body(pr, carry):
        pair(2 * pr - off)
        return carry

    n_quads = (n_pairs - 1) // 2
    lax.fori_loop(0, n_quads, quad_body, 0)
    lax.fori_loop(2 * n_quads, n_pairs - 1, pair_body, 0)
    stage_a(qi, 1, diag=True)
    stage_b(qi - 1, 0)
    stage_a_win(0, 0)
    stage_b(qi, 1)
    for wi in range(1, n_win):
        stage_a_win(wi, wi % 2)
        stage_b_win(wi - 1, (wi - 1) % 2)
    stage_b_win(n_win - 1, (n_win - 1) % 2)

    o_win_all = accw_scr[:ATT_DIM, :] / accw_scr[ATT_DIM:ATT_DIM + 1, :]
    o_slc_all = acc_scr[:ATT_DIM, :] / acc_scr[ATT_DIM:ATT_DIM + 1, :]
    for r in range(ATT_REP):
        o_win = o_win_all[:, r * tq:(r + 1) * tq]
        o_slc = o_slc_all[:, r * tq:(r + 1) * tq]
        o_cmp = ocmp_ref[0, r * ATT_DIM:(r + 1) * ATT_DIM, :]
        gbase = (g * ATT_REP + r) * 3
        g_cmp = _sigmoid(gt_ref[0, pl.ds(gbase, 1), :])
        g_slc = _sigmoid(gt_ref[0, pl.ds(gbase + 1, 1), :])
        g_win = _sigmoid(gt_ref[0, pl.ds(gbase + 2, 1), :])
        o = g_cmp * o_cmp + g_slc * o_slc + g_win * o_win
        o = o * lax.rsqrt(jnp.mean(o * o, axis=0, keepdims=True) + RMS_EPS)
        o = o * gn_ref[r * ATT_DIM:(r + 1) * ATT_DIM, :]
        o_ref[0, r * ATT_DIM:(r + 1) * ATT_DIM, :] = o.astype(o_ref.dtype)


def _slc_call(qt, ks, vst, kw, vwt, bias, ocmp, smallt, gn, tq):
    b, _, _, s = qt.shape
    nsel = bias.shape[3]
    gw = ATT_REP * ATT_DIM
    vres = pl.BlockSpec((1, ATT_DIM, s), lambda i, g, j: (i, g, 0))
    return pl.pallas_call(
        functools.partial(_slc_body, tq=tq),
        grid=(b, ATT_KV_HEADS, s // tq),
        in_specs=[pl.BlockSpec((1, ATT_REP, ATT_DIM, tq), lambda i, g, j: (i, g, 0, j)),
                  pl.BlockSpec((1, 1, s, LANES), lambda i, g, j: (i, g, 0, 0)), vres,
                  pl.BlockSpec((1, 1, s, ATT_DIM), lambda i, g, j: (i, g, 0, 0)), vres,
                  pl.BlockSpec((1, 1, 1, nsel, tq), lambda i, g, j: (i, g, j, 0, 0)),
                  pl.BlockSpec((1, gw, tq), lambda i, g, j: (i, g, j)),
                  pl.BlockSpec((1, LANES, tq), lambda i, g, j: (i, 0, j)),
                  pl.BlockSpec((gw, tq), lambda i, g, j: (g, 0))],
        out_specs=pl.BlockSpec((1, gw, tq), lambda i, g, j: (i, g, j)),
        out_shape=jax.ShapeDtypeStruct((b, ATT_WIDTH, s), MXU_DT),
        scratch_shapes=[pltpu.VMEM((LANES, ATT_REP * tq), MXU_DT),
                        pltpu.VMEM((LANES, ATT_REP * tq), MXU_DT),
                        pltpu.VMEM((tq, ATT_REP * tq), F32),
                        pltpu.VMEM((tq, ATT_REP * tq), F32),
                        pltpu.VMEM((1, ATT_REP * tq), F32),
                        pltpu.VMEM((1, ATT_REP * tq), F32),
                        pltpu.VMEM((1, ATT_REP * tq), F32),
                        pltpu.VMEM((ATT_DIM + 2 * SUBLANES, ATT_REP * tq), F32),
                        pltpu.VMEM((1, ATT_REP * tq), F32),
                        pltpu.VMEM((ATT_DIM + 2 * SUBLANES, ATT_REP * tq), F32)],
        compiler_params=_params(3),
        name="slc",
    )(qt, ks, vst, kw, vwt, bias, ocmp, smallt, gn)


def _mlstm_body(mqk_ref, mv_ref, mo_ref, sm_ref, smt_ref, cw_ref, cb_ref, gbr_ref, gbc_ref, ng_ref,
                o_ref, c_scr, m_scr, tail_scr, *, chunk):
    @pl.when(pl.program_id(1) == 0)
    def _():
        c_scr[...] = jnp.zeros(c_scr.shape, F32)
        m_scr[...] = jnp.full(m_scr.shape, NEG, F32)
        tail_scr[...] = jnp.zeros(tail_scr.shape, F32)

    x = mqk_ref[0].astype(F32)
    tail = tail_scr[...]
    y = cb_ref[...] + x * cw_ref[M_CONV - 1:M_CONV, :]
    for k in range(1, M_CONV):
        y = y + _shift_rows(x, tail, k) * cw_ref[M_CONV - 1 - k:M_CONV - k, :]
    tail_scr[...] = x[chunk - SUBLANES:, :]
    qk = _silu(y)
    sm = sm_ref[0] + gbr_ref[...]
    smt = smt_ref[0] + gbc_ref[...]
    row = lax.broadcasted_iota(jnp.int32, (chunk, chunk), 0)
    col = lax.broadcasted_iota(jnp.int32, (chunk, chunk), 1)
    causal = row >= col
    ones = jnp.ones((chunk, M_DIM), F32)

    def log_sigmoid(z):
        return jnp.minimum(z, 0.0) - jnp.log(1.0 + jnp.exp(-jnp.abs(z)))

    for h in range(M_HEADS):
        q = qk[:, h * M_DIM:(h + 1) * M_DIM]
        k = qk[:, M_WIDTH + h * M_DIM:M_WIDTH + (h + 1) * M_DIM] * (M_DIM ** -0.5)
        v = mv_ref[0, :, h * M_DIM:(h + 1) * M_DIM]
        i_col = sm[:, SMALL_I + h:SMALL_I + h + 1]
        lf_col = log_sigmoid(sm[:, SMALL_F + h:SMALL_F + h + 1])
        i_row = smt[SMALL_I + h:SMALL_I + h + 1, :]
        lf_row = log_sigmoid(smt[SMALL_F + h:SMALL_F + h + 1, :])
        b_col = jnp.sum(jnp.where(causal, lf_row, 0.0), axis=1, keepdims=True)
        b_row = jnp.sum(jnp.where(row <= col, lf_col, 0.0), axis=0, keepdims=True)
        m_prev = m_scr[h]
        dmat = jnp.where(causal, b_col - b_row + i_row, NEG)
        inter = b_col + m_prev
        mt = jnp.maximum(inter, jnp.max(dmat, axis=1, keepdims=True))
        w_intra = jnp.exp(dmat - mt)
        w_inter = jnp.exp(inter - mt)
        qkw = _dot_nt(q, k) * w_intra
        c_aug = c_scr[h]
        r1 = _dot(q, c_aug)
        num = w_inter * r1[:, :M_DIM] + _dot(qkw, v)
        den = w_inter * r1[:, M_DIM:M_DIM + 1] + jnp.sum(qkw, axis=1, keepdims=True)
        hh = num / jnp.maximum(jnp.abs(den), jnp.exp(-mt))
        b_last = b_col[chunk - 1:chunk, :]
        a_col = b_last - b_col + i_col
        m_new = jnp.maximum(b_last + m_prev, jnp.max(a_col, axis=0, keepdims=True))
        ws = jnp.exp(a_col - m_new)
        decay = jnp.exp(b_last + m_prev - m_new)
        v_aug = jnp.concatenate([v.astype(F32), ones], axis=1)
        c_scr[h] = decay * c_aug + _dot_tn(k * ws, v_aug)
        m_scr[h] = m_new
        hm = _sigmoid(mo_ref[0, :, h * M_DIM:(h + 1) * M_DIM].astype(F32)) * hh
        hm = hm * lax.rsqrt(jnp.mean(hm * hm, axis=1, keepdims=True) + RMS_EPS)
        o_ref[0, :, h * M_DIM:(h + 1) * M_DIM] = (hm * ng_ref[:, h * M_DIM:(h + 1) * M_DIM]).astype(o_ref.dtype)


def _mlstm_call(mqk, mv, mo, small, smallt, cw, cb, gbr, gbc, ng, chunk):
    b, s, _ = mv.shape
    row = lambda n: pl.BlockSpec((1, chunk, n), lambda i, j: (i, j, 0))
    full = lambda a: pl.BlockSpec(a.shape, lambda i, j: (0,) * a.ndim)
    return pl.pallas_call(
        functools.partial(_mlstm_body, chunk=chunk),
        grid=(b, s // chunk),
        in_specs=[row(2 * M_WIDTH), row(M_WIDTH), row(M_WIDTH), row(LANES),
                  pl.BlockSpec((1, LANES, chunk), lambda i, j: (i, 0, j)),
                  full(cw), full(cb), full(gbr), full(gbc), full(ng)],
        out_specs=row(M_WIDTH),
        out_shape=jax.ShapeDtypeStruct((b, s, M_WIDTH), MXU_DT),
        scratch_shapes=[pltpu.VMEM((M_HEADS, M_DIM, 2 * M_DIM), F32),
                        pltpu.VMEM((M_HEADS, 1, 1), F32),
                        pltpu.VMEM((SUBLANES, 2 * M_WIDTH), F32)],
        compiler_params=_params(2),
        name="mlstm",
    )(mqk, mv, mo, small, smallt, cw, cb, gbr, gbc, ng)


def _mix_ffn_body(x_ref, attt_ref, hm_ref, wo_ref, g1_ref, sc_ref, sh_ref, g2_ref, ng_ref,
                  wup_ref, cw_ref, cb_ref, wdn_ref, fg_ref, o_ref, tail_scr, g_scr, *, d_ff, tf, tiles_per_seq, final):
    @pl.when(pl.program_id(0) % tiles_per_seq == 0)
    def _():
        tail_scr[...] = jnp.zeros(tail_scr.shape, F32)

    d = x_ref.shape[-1]
    mix = _dot_tn(attt_ref[0], wo_ref[:ATT_WIDTH, :]) + _dot(hm_ref[0], wo_ref[ATT_WIDTH:, :])
    x = x_ref[0] + g1_ref[0] * mix
    var = jnp.mean(x * x, axis=-1, keepdims=True)
    h = (x * lax.rsqrt(var + RMS_EPS)) * ng_ref[...]
    hb = (h * (1.0 + sc_ref[0]) + sh_ref[0]).astype(MXU_DT)
    for c in range(d_ff // tf):
        a = jnp.dot(hb, wup_ref[:, c * tf:(c + 1) * tf], preferred_element_type=F32)
        v = jnp.dot(hb, wup_ref[:, d_ff + c * tf:d_ff + (c + 1) * tf], preferred_element_type=F32)
        tail = tail_scr[c]
        y = cb_ref[:, c * tf:(c + 1) * tf] + a * cw_ref[FFN_CONV - 1:FFN_CONV, c * tf:(c + 1) * tf]
        for k in range(1, FFN_CONV):
            y = y + _shift_rows(a, tail, k) * cw_ref[FFN_CONV - 1 - k:FFN_CONV - k, c * tf:(c + 1) * tf]
        tail_scr[c] = a[a.shape[0] - SUBLANES:, :]
        g_scr[:, c * tf:(c + 1) * tf] = (_silu(y) * v).astype(g_scr.dtype)
    out = x + g2_ref[0] * jnp.dot(g_scr[...], wdn_ref[...], preferred_element_type=F32)
    if final:
        var = jnp.mean(out * out, axis=-1, keepdims=True)
        out = (out * lax.rsqrt(var + RMS_EPS)) * fg_ref[...]
    o_ref[0] = out


def _mix_ffn_call(x, attt, hm, wo, layer, g1, sc, sh, g2, ng, wup, cw, cb, wdn, fg, tm, tf, final):
    b, s, d = x.shape
    d_ff = wdn.shape[1]
    stacked = lambda a: pl.BlockSpec((None,) + a.shape[1:], lambda i: (layer,) + (0,) * (a.ndim - 1))
    tiles_per_seq = s // tm
    row = lambda n: pl.BlockSpec((1, tm, n), lambda i: (i // tiles_per_seq, i % tiles_per_seq, 0))
    vec = pl.BlockSpec((1, 1, d), lambda i: (i // tiles_per_seq, 0, 0))
    full = lambda a: pl.BlockSpec(a.shape, lambda i: (0,) * a.ndim)
    return pl.pallas_call(
        functools.partial(_mix_ffn_body, d_ff=d_ff, tf=tf, tiles_per_seq=tiles_per_seq, final=final),
        grid=(b * tiles_per_seq,),
        in_specs=[row(d),
                  pl.BlockSpec((1, ATT_WIDTH, tm), lambda i: (i // tiles_per_seq, 0, i % tiles_per_seq)),
                  row(M_WIDTH), stacked(wo), vec, vec, vec, vec, full(ng),
                  stacked(wup), full(cw), full(cb), stacked(wdn), full(fg)],
        out_specs=row(d),
        out_shape=jax.ShapeDtypeStruct((b, s, d), F32),
        scratch_shapes=[pltpu.VMEM((d_ff // tf, SUBLANES, tf), F32),
                        pltpu.VMEM((tm, d_ff), MXU_DT)],
        compiler_params=_params(1),
        name="mix_ffn",
    )(x, attt, hm, wo, g1, sc, sh, g2, ng, wup, cw, cb, wdn, fg)


def _pick_tile(n, pref):
    t = min(n, pref)
    assert n % t == 0, (n, t)
    return t


def _pack_w_in(w_in):
    att_cols = ATT_WIDTH + 6 * KV_WIDTH
    n_gate = 3 * ATT_HEADS
    o = att_cols
    gates = w_in[..., o:o + n_gate]; o += n_gate
    mqk = w_in[..., o:o + 2 * M_WIDTH]; o += 2 * M_WIDTH
    mv = w_in[..., o:o + M_WIDTH]; o += M_WIDTH
    mif = w_in[..., o:o + 2 * M_HEADS]; o += 2 * M_HEADS
    mo = w_in[..., o:o + M_WIDTH]
    pad = jnp.zeros(w_in.shape[:-1] + (LANES - n_gate - 2 * M_HEADS,), w_in.dtype)
    return jnp.concatenate([w_in[..., :att_cols], mqk, mv, mo, gates, mif, pad], axis=-1).astype(MXU_DT)


def _expand_cmp_weights(pos, w1, w2):
    half = CMP_BLOCK // 2
    lead = w1.shape[:-2]
    w1r = w1.reshape(lead + (CMP_BLOCK, ATT_DIM, ATT_DIM))
    eye = jnp.eye(ATT_KV_HEADS, dtype=w1.dtype)

    def expand(wl):
        z = wl[..., :, None, :, None, :] * eye[None, :, None, :, None]
        return z.reshape(lead + (half * KV_WIDTH, KV_WIDTH)).astype(MXU_DT)

    def expand_pos(pl_):
        z = jnp.broadcast_to(pl_[..., :, None, :], lead + (half, ATT_KV_HEADS, ATT_DIM))
        return z.reshape(lead + (1, half * KV_WIDTH))

    w2e = (w2[..., None, :, None, :] * eye[:, None, :, None]).reshape(lead + (KV_WIDTH, KV_WIDTH))
    return (expand_pos(pos[..., :half, :]), expand_pos(pos[..., half:, :]),
            expand(w1r[..., :half, :, :]), expand(w1r[..., half:, :, :]), w2e.astype(MXU_DT))


def kernel(x, c, positions, norm1_g, norm2_g, ada_w, ada_b, w_in, cmp_pos, cmp_w1, cmp_w2, att_norm_g,
           m_conv_w, m_conv_b, m_gate_b, m_norm_g, w_out, ffn_up, ffn_conv_w, ffn_conv_b, ffn_down, final_g):
    b, s, d = x.shape
    depth = ada_w.shape[0]
    d_ff = ffn_down.shape[1]
    assert s % (CMP_STRIDE * SUBLANES) == 0 and d % LANES == 0 and b <= SUBLANES
    tm_proj = _pick_tile(s, 512)
    tm_ffn = _pick_tile(s, 512)
    tq = _pick_tile(s, 256)
    chunk = _pick_tile(s, 256)
    tf = 256
    assert d_ff % tf == 0 and WINDOW % tq == 0 and s >= WINDOW + tq
    nc = s // CMP_STRIDE
    nsel = s // SEL_BLOCK
    k_top = min(SEL_TOPK, nsel)

    c8 = jnp.zeros((SUBLANES, d), F32).at[:b].set(c)
    mod = _ada_call(c8, ada_w, ada_b.reshape(depth, 1, 6 * d), _pick_tile(6 * d, 1536))

    def mod_vec(l, k):
        return mod[l, :b, k * d:(k + 1) * d].reshape(b, 1, d)

    lane_d = jnp.arange(LANES) % ATT_DIM
    inv_freq = ROPE_THETA ** (-(2 * (lane_d % ROPE_HALF)).astype(F32) / ROPE_DIM)
    rc, rs1, rs2 = _rope_call(positions.reshape(b, s, 1), inv_freq.reshape(1, LANES), _pick_tile(s, 1024))

    c_start = jnp.arange(nc)[None, :] * CMP_STRIDE
    s_start = jnp.arange(nsel)[:, None] * SEL_BLOCK
    ovt = ((c_start <= s_start + SEL_BLOCK - 1) & (c_start + CMP_BLOCK - 1 >= s_start)
           & (jnp.arange(nc)[None, :] < nc - 1)).astype(MXU_DT)

    w_in_p = _pack_w_in(w_in)
    w_out_b, ffn_up_b, ffn_down_b = (w.astype(MXU_DT) for w in (w_out, ffn_up, ffn_down))
    cmp_w = _expand_cmp_weights(cmp_pos, cmp_w1, cmp_w2)

    for l in range(depth):
        outs = _proj_call(x, mod_vec(l, 1), mod_vec(l, 0), norm1_g[l].reshape(1, d), w_in_p, l,
                          rc, rs1, rs2, tm_proj)
        q, kc, vc, ks, vst, kw, vwt, mqk, mv, mo, small, smallt = outs
        kcmp, _ = _compress_call(kc.reshape(b, nc, CMP_STRIDE * KV_WIDTH), *(w[l, 0] for w in cmp_w))
        _, vcmpt = _compress_call(vc.reshape(b, nc, CMP_STRIDE * KV_WIDTH), *(w[l, 1] for w in cmp_w))
        ocmp, bias = _cmp_call(q, kcmp, vcmpt, ovt, tq, k_top)
        gn = jnp.broadcast_to(att_norm_g[l][:, None], (ATT_WIDTH, tq))
        attt = _slc_call(q, ks, vst, kw, vwt, bias, ocmp, smallt, gn, tq)
        gbr = jnp.zeros((1, LANES), F32).at[0, SMALL_I:SMALL_I + 2 * M_HEADS].set(m_gate_b[l])
        hm = _mlstm_call(mqk, mv, mo, small, smallt, m_conv_w[l], m_conv_b[l].reshape(1, -1),
                         gbr, gbr.reshape(LANES, 1), m_norm_g[l].reshape(1, -1), chunk)
        x = _mix_ffn_call(x, attt, hm, w_out_b, l, mod_vec(l, 2), mod_vec(l, 4), mod_vec(l, 3),
                          mod_vec(l, 5), norm2_g[l].reshape(1, d), ffn_up_b, ffn_conv_w[l],
                          ffn_conv_b[l].reshape(1, -1), ffn_down_b, final_g.reshape(1, d),
                          tm_ffn, tf, l == depth - 1)
    return x
```

```python
import functools

import jax
import jax.numpy as jnp
from jax import lax
from jax.experimental import pallas as pl
from jax.experimental.pallas import tpu as pltpu

F32 = jnp.float32
BF16 = jnp.bfloat16
MXU_DT = jnp.bfloat16

ATT_DIM = 64
ATT_HEADS = 8
ATT_KV_HEADS = 2
ATT_REP = ATT_HEADS // ATT_KV_HEADS
ATT_WIDTH = ATT_HEADS * ATT_DIM
KV_WIDTH = ATT_KV_HEADS * ATT_DIM
ROPE_DIM = ATT_DIM // 4
ROPE_HALF = ROPE_DIM // 2
ROPE_THETA = 500000.0
CMP_BLOCK = 32
CMP_STRIDE = 16
SEL_BLOCK = 64
SEL_TOPK = 16
WINDOW = 512
FORCE_BONUS = 1.0e4
M_HEADS = 4
M_DIM = 128
M_WIDTH = M_HEADS * M_DIM
M_CONV = 4
FFN_CONV = 3
RMS_EPS = 1e-6
NEG = -1e30
Q_SCALE = ATT_DIM ** -0.5 * 1.4426950408889634
BIAS_GROUP = 8

LANES = 128
SUBLANES = 8
VMEM_LIMIT = 56 * 1024 * 1024

COL_Q = 0
COL_KC = 512
COL_VC = 640
COL_KS = 768
COL_VS = 896
COL_KW = 1024
COL_VW = 1152
COL_MQK = 1280
COL_MV = 2304
COL_MO = 2816
COL_SMALL = 3328
IN_COLS_PACKED = 3456
SMALL_I = 3 * ATT_HEADS
SMALL_F = SMALL_I + M_HEADS

_NT = (((1,), (1,)), ((), ()))
_TN = (((0,), (0,)), ((), ()))


def _params(n_axes):
    return pltpu.CompilerParams(dimension_semantics=("arbitrary",) * n_axes,
                                vmem_limit_bytes=VMEM_LIMIT)


def _dot(a, b):
    return jnp.dot(a.astype(MXU_DT), b.astype(MXU_DT), preferred_element_type=F32)


def _dot_nt(a, b):
    return lax.dot_general(a.astype(MXU_DT), b.astype(MXU_DT), _NT, preferred_element_type=F32)


def _dot_tn(a, b):
    return lax.dot_general(a.astype(MXU_DT), b.astype(MXU_DT), _TN, preferred_element_type=F32)


def _sigmoid(x):
    return 1.0 / (1.0 + jnp.exp(-x))


def _silu(x):
    return x * _sigmoid(x)


def _shift_rows(x, prev8, k):
    rolled = pltpu.roll(x, k, 0)
    fix = pltpu.roll(prev8, k, 0)
    row = lax.broadcasted_iota(jnp.int32, fix.shape, 0)
    top = jnp.where(row < k, fix, rolled[:SUBLANES])
    return jnp.concatenate([top, rolled[SUBLANES:]], axis=0)


def _ada_body(c_ref, w_ref, b_ref, o_ref):
    c = c_ref[...]
    o_ref[0] = _dot(_silu(c), w_ref[0]) + b_ref[0]


def _ada_call(c8, ada_w, ada_b3, tn):
    depth, d, n = ada_w.shape
    return pl.pallas_call(
        _ada_body,
        grid=(depth, n // tn),
        in_specs=[pl.BlockSpec((SUBLANES, d), lambda l, j: (0, 0)),
                  pl.BlockSpec((1, d, tn), lambda l, j: (l, 0, j)),
                  pl.BlockSpec((1, 1, tn), lambda l, j: (l, 0, j))],
        out_specs=pl.BlockSpec((1, SUBLANES, tn), lambda l, j: (l, 0, j)),
        out_shape=jax.ShapeDtypeStruct((depth, SUBLANES, n), F32),
        compiler_params=_params(2),
        name="ada",
    )(c8, ada_w, ada_b3)


def _rope_body(pos_ref, freq_ref, rc_ref, rs1_ref, rs2_ref):
    pos = pos_ref[0].astype(F32)
    ang = pos * freq_ref[...]
    d = lax.broadcasted_iota(jnp.int32, ang.shape, 1) % ATT_DIM
    cos = jnp.cos(ang)
    sin = jnp.sin(ang)
    rc_ref[0] = jnp.where(d < ROPE_DIM, cos, 1.0)
    rs1_ref[0] = jnp.where(d < ROPE_HALF, -sin, 0.0)
    rs2_ref[0] = jnp.where((d >= ROPE_HALF) & (d < ROPE_DIM), sin, 0.0)


def _rope_call(pos3, freq, tm):
    b, s, _ = pos3.shape
    spec = pl.BlockSpec((1, tm, LANES), lambda i, j: (i, j, 0))
    shp = jax.ShapeDtypeStruct((b, s, LANES), F32)
    return pl.pallas_call(
        _rope_body,
        grid=(b, s // tm),
        in_specs=[pl.BlockSpec((1, tm, 1), lambda i, j: (i, j, 0)),
                  pl.BlockSpec((1, LANES), lambda i, j: (0, 0))],
        out_specs=[spec, spec, spec],
        out_shape=[shp, shp, shp],
        compiler_params=_params(2),
        name="rope",
    )(pos3, freq)


def _proj_body(x_ref, sc_ref, sh_ref, g_ref, w_ref, rc_ref, rs1_ref, rs2_ref,
               q_ref, kc_ref, vc_ref, ks_ref, vst_ref, kw_ref, vwt_ref,
               mqk_ref, mv_ref, mo_ref, sm_ref, smt_ref):
    x = x_ref[0]
    var = jnp.mean(x * x, axis=-1, keepdims=True)
    h = (x * lax.rsqrt(var + RMS_EPS)) * g_ref[...]
    h = h * (1.0 + sc_ref[0]) + sh_ref[0]
    hb = h.astype(MXU_DT)

    def mm(c0, n):
        return jnp.dot(hb, w_ref[:, c0:c0 + n], preferred_element_type=F32)

    rc, rs1, rs2 = rc_ref[0], rs1_ref[0], rs2_ref[0]

    def rope(y):
        return (y * rc + pltpu.roll(y, LANES - ROPE_HALF, 1) * rs1
                + pltpu.roll(y, ROPE_HALF, 1) * rs2)

    for j in range(ATT_HEADS // 4):
        y2 = mm(COL_Q + 2 * LANES * j, 2 * LANES)
        for i in range(2):
            yt = (rope(y2[:, LANES * i:LANES * (i + 1)]) * Q_SCALE).T
            q_ref[0, 4 * j + 2 * i] = yt[:ATT_DIM].astype(q_ref.dtype)
            q_ref[0, 4 * j + 2 * i + 1] = yt[ATT_DIM:].astype(q_ref.dtype)
    y2 = mm(COL_KC, 2 * KV_WIDTH)
    kc_ref[0] = rope(y2[:, :KV_WIDTH])
    vc_ref[0] = y2[:, KV_WIDTH:]
    y2 = mm(COL_KS, 2 * KV_WIDTH)
    y = rope(y2[:, :KV_WIDTH])
    lane = lax.broadcasted_iota(jnp.int32, y.shape, 1)
    t = pl.program_id(1) * y.shape[0] + lax.broadcasted_iota(jnp.int32, y.shape, 0)
    onehot = jnp.where(lane - ATT_DIM == (t // SEL_BLOCK) % BIAS_GROUP, 1.0, 0.0)
    ks_ref[0, 0] = jnp.where(lane < ATT_DIM, y, onehot).astype(ks_ref.dtype)
    ks_ref[0, 1] = jnp.where(lane < ATT_DIM, pltpu.roll(y, ATT_DIM, 1), onehot).astype(ks_ref.dtype)
    vst_ref[0] = y2[:, KV_WIDTH:].T.astype(vst_ref.dtype)
    y2 = mm(COL_KW, 2 * KV_WIDTH)
    y = rope(y2[:, :KV_WIDTH])
    kw_ref[0, 0] = y[:, :ATT_DIM].astype(kw_ref.dtype)
    kw_ref[0, 1] = y[:, ATT_DIM:].astype(kw_ref.dtype)
    vwt_ref[0] = y2[:, KV_WIDTH:].T.astype(vwt_ref.dtype)
    for j in range(2 * M_WIDTH // 256):
        mqk_ref[0, :, 256 * j:256 * (j + 1)] = mm(COL_MQK + 256 * j, 256).astype(mqk_ref.dtype)
    for j in range(M_WIDTH // 256):
        mv_ref[0, :, 256 * j:256 * (j + 1)] = mm(COL_MV + 256 * j, 256).astype(mv_ref.dtype)
        mo_ref[0, :, 256 * j:256 * (j + 1)] = mm(COL_MO + 256 * j, 256).astype(mo_ref.dtype)
    sm = mm(COL_SMALL, LANES)
    sm_ref[0] = sm
    smt_ref[0] = sm.T


def _proj_call(x, sc, sh, g, w, layer, rc, rs1, rs2, tm):
    b, s, d = x.shape
    row = lambda n: pl.BlockSpec((1, tm, n), lambda i, j: (i, j, 0))
    vec = pl.BlockSpec((1, 1, d), lambda i, j: (i, 0, 0))
    headed = lambda n, w: pl.BlockSpec((1, n, tm, w), lambda i, j: (i, 0, j, 0))
    tposed = pl.BlockSpec((1, LANES, tm), lambda i, j: (i, 0, j))
    sds = jax.ShapeDtypeStruct
    return pl.pallas_call(
        _proj_body,
        grid=(b, s // tm),
        in_specs=[row(d), vec, vec,
                  pl.BlockSpec((1, d), lambda i, j: (0, 0)),
                  pl.BlockSpec((None, d, IN_COLS_PACKED), lambda i, j: (layer, 0, 0)),
                  row(LANES), row(LANES), row(LANES)],
        out_specs=[pl.BlockSpec((1, ATT_HEADS, ATT_DIM, tm), lambda i, j: (i, 0, 0, j)),
                   row(KV_WIDTH), row(KV_WIDTH),
                   headed(ATT_KV_HEADS, LANES), tposed, headed(ATT_KV_HEADS, ATT_DIM), tposed,
                   row(2 * M_WIDTH), row(M_WIDTH), row(M_WIDTH), row(LANES), tposed],
        out_shape=[sds((b, ATT_HEADS, ATT_DIM, s), MXU_DT),
                   sds((b, s, KV_WIDTH), F32), sds((b, s, KV_WIDTH), F32),
                   sds((b, ATT_KV_HEADS, s, LANES), MXU_DT), sds((b, KV_WIDTH, s), MXU_DT),
                   sds((b, ATT_KV_HEADS, s, ATT_DIM), MXU_DT), sds((b, KV_WIDTH, s), MXU_DT),
                   sds((b, s, 2 * M_WIDTH), MXU_DT), sds((b, s, M_WIDTH), MXU_DT),
                   sds((b, s, M_WIDTH), MXU_DT), sds((b, s, LANES), F32), sds((b, LANES, s), F32)],
        compiler_params=_params(2),
        name="proj",
    )(x, sc, sh, g, w, rc, rs1, rs2)


def _compress_body(kr_ref, pa_ref, pb_ref, w1a_ref, w1b_ref, w2_ref, out_ref, outt_ref):
    kr = kr_ref[0]
    nc = kr.shape[0]
    ya = _dot(kr + pa_ref[...], w1a_ref[...])
    yb = _dot(kr + pb_ref[...], w1b_ref[...])
    pre = ya + pltpu.roll(yb, nc - 1, 0)
    out = _dot(_silu(pre), w2_ref[...])
    row = lax.broadcasted_iota(jnp.int32, out.shape, 0)
    out = jnp.where(row < nc - 1, out, 0.0)
    out_ref[0, 0] = out[:, :ATT_DIM].astype(out_ref.dtype)
    out_ref[0, 1] = out[:, ATT_DIM:].astype(out_ref.dtype)
    outt_ref[0] = out.T.astype(outt_ref.dtype)


def _compress_call(kr, pa, pb, w1a, w1b, w2e):
    b, nc, width = kr.shape
    full = lambda a: pl.BlockSpec(a.shape, lambda i: (0,) * a.ndim)
    return pl.pallas_call(
        _compress_body,
        grid=(b,),
        in_specs=[pl.BlockSpec((1, nc, width), lambda i: (i, 0, 0)),
                  full(pa), full(pb), full(w1a), full(w1b), full(w2e)],
        out_specs=[pl.BlockSpec((1, ATT_KV_HEADS, nc, ATT_DIM), lambda i: (i, 0, 0, 0)),
                   pl.BlockSpec((1, KV_WIDTH, nc), lambda i: (i, 0, 0))],
        out_shape=[jax.ShapeDtypeStruct((b, ATT_KV_HEADS, nc, ATT_DIM), MXU_DT),
                   jax.ShapeDtypeStruct((b, KV_WIDTH, nc), MXU_DT)],
        compiler_params=_params(1),
        name="compress",
    )(kr, pa, pb, w1a, w1b, w2e)


def _heads_on_lanes(qt_ref):
    return jnp.concatenate([qt_ref[0, r] for r in range(ATT_REP)], axis=1)


def _tile_heads(a):
    return jnp.concatenate([a] * ATT_REP, axis=1)


def _with_ones_row(vt):
    n = vt.shape[1]
    extra = jnp.where(lax.broadcasted_iota(jnp.int32, (2 * SUBLANES, n), 0) == 0, 1.0, 0.0)
    return jnp.concatenate([vt, extra.astype(vt.dtype)], axis=0)


def _cmp_body(qt_ref, kc_ref, vct_ref, ovt_ref, ocmp_ref, bias_ref, *, tq, k_top, n_split):
    t0 = pl.program_id(2) * tq
    nc_all = kc_ref.shape[2]
    nsel_all = ovt_ref.shape[0]
    rows_per_sel = SEL_BLOCK // CMP_STRIDE

    def compute(nc):
        nsel = nc // rows_per_sel
        row = lax.broadcasted_iota(jnp.int32, (nc, tq), 0)
        tpos = t0 + lax.broadcasted_iota(jnp.int32, (nc, tq), 1)
        mbias = jnp.where(row * CMP_STRIDE + (CMP_BLOCK - 1) <= tpos, 0.0, NEG)
        s = jnp.dot(kc_ref[0, 0, :nc, :], _heads_on_lanes(qt_ref), preferred_element_type=F32)
        s = s + _tile_heads(mbias)
        m = jnp.max(s, axis=0, keepdims=True)
        m = jnp.where(m > 0.5 * NEG, m, 0.0)
        p = jnp.exp2(s - m).astype(MXU_DT)
        oa = jnp.dot(_with_ones_row(vct_ref[0, :, :nc]), p, preferred_element_type=F32)
        d = oa[ATT_DIM:ATT_DIM + 1]
        inv = 1.0 / jnp.where(d > 0.0, d, 1.0)
        o = oa[:ATT_DIM] * inv
        impa = jnp.dot(ovt_ref[:nsel, :nc], p, preferred_element_type=F32) * inv
        imp = impa[:, :tq]
        for r in range(ATT_REP):
            ocmp_ref[0, r * ATT_DIM:(r + 1) * ATT_DIM, :] = o[:, r * tq:(r + 1) * tq]
            if r:
                imp = imp + impa[:, r * tq:(r + 1) * tq]
        jidx = lax.broadcasted_iota(jnp.int32, (nsel, tq), 0)
        tq_pos = t0 + lax.broadcasted_iota(jnp.int32, (nsel, tq), 1)
        jcur = tq_pos // SEL_BLOCK
        forced = (jidx == 0) | (jidx == jcur) | (jidx == jcur - 1)
        valid = jidx * SEL_BLOCK <= tq_pos
        if nsel < nsel_all:
            bias_ref[0, 0, 0, nsel:, :] = jnp.full((nsel_all - nsel, tq), NEG, F32)
        rounds = min(k_top, nsel)
        jcur_row = (t0 + lax.broadcasted_iota(jnp.int32, (1, tq), 1)) // SEL_BLOCK
        quota = rounds - jnp.minimum(jcur_row + 1, 3)
        score = jnp.where(valid & jnp.logical_not(forced), imp, NEG)
        cand = score > 0.5 * NEG
        n_rounds = max(rounds - 1, 0)
        n_free = max(rounds - 3, 0)

        def drop_max(cur_score, r, exact):
            top = jnp.max(cur_score, axis=0, keepdims=True)
            if exact:
                first = jnp.min(jnp.where(cur_score == top, jidx, nsel), axis=0, keepdims=True)
                hit = jidx == first
            else:
                hit = cur_score == top
            if r >= n_free:
                hit = hit & (quota > r)
            return jnp.where(hit, NEG, cur_score)

        fast = score
        for r in range(n_rounds):
            fast = drop_max(fast, r, exact=False)
        picked = cand & (fast < 0.5 * NEG)
        bias_ref[0, 0, 0, :nsel, :] = jnp.where(picked | (forced & valid), 0.0, NEG)
        n_picked = jnp.sum(jnp.where(picked, 1.0, 0.0), axis=0, keepdims=True)
        n_cand = jnp.sum(jnp.where(cand, 1.0, 0.0), axis=0, keepdims=True)
        want = jnp.minimum(n_cand, jnp.maximum(quota, 0).astype(F32))
        mismatch = jnp.max(jnp.abs(n_picked - want))

        @pl.when(mismatch > 0.0)
        def _():
            slow = score
            for r in range(n_rounds):
                slow = drop_max(slow, r, exact=True)
            bias_ref[0, 0, 0, :nsel, :] = jnp.where((cand & (slow < 0.5 * NEG)) | (forced & valid), 0.0, NEG)

    chunk_pos = nc_all // n_split * CMP_STRIDE
    need = jnp.minimum((t0 + tq - 1) // chunk_pos + 1, n_split)
    for k in range(1, n_split + 1):
        pl.when(need == k)(functools.partial(compute, nc_all // n_split * k))


def _cmp_call(qt, kcmp, vcmpt, ovt, tq, k_top):
    b, _, _, s = qt.shape
    nc = kcmp.shape[2]
    nsel = ovt.shape[0]
    nq = s // tq
    n_split = 4 if nc % (4 * LANES) == 0 and (nc // 4 * CMP_STRIDE) % tq == 0 else 1
    return pl.pallas_call(
        functools.partial(_cmp_body, tq=tq, k_top=k_top, n_split=n_split),
        grid=(b, ATT_KV_HEADS, nq),
        in_specs=[pl.BlockSpec((1, ATT_REP, ATT_DIM, tq), lambda i, g, j: (i, g, 0, j)),
                  pl.BlockSpec((1, 1, nc, ATT_DIM), lambda i, g, j: (i, g, 0, 0)),
                  pl.BlockSpec((1, ATT_DIM, nc), lambda i, g, j: (i, g, 0)),
                  pl.BlockSpec((nsel, nc), lambda i, g, j: (0, 0))],
        out_specs=[pl.BlockSpec((1, ATT_REP * ATT_DIM, tq), lambda i, g, j: (i, g, j)),
                   pl.BlockSpec((1, 1, 1, nsel, tq), lambda i, g, j: (i, g, j, 0, 0))],
        out_shape=[jax.ShapeDtypeStruct((b, ATT_WIDTH, s), F32),
                   jax.ShapeDtypeStruct((b, ATT_KV_HEADS, nq, nsel, tq), F32)],
        compiler_params=_params(3),
        name="cmp",
    )(qt, kcmp, vcmpt, ovt)


def _slc_body(qt_ref, ks_ref, vst_ref, kw_ref, vwt_ref, bias_ref, ocmp_ref, gt_ref, gn_ref,
              o_ref, qa0_scr, qa1_scr, s0_scr, s1_scr, mx0_scr, mx1_scr, m_scr, acc_scr,
              mw_scr, accw_scr, *, tq):
    qa_scr = (qa0_scr, qa1_scr)
    s_scr = (s0_scr, s1_scr)
    mx_scr = (mx0_scr, mx1_scr)
    g = pl.program_id(1)
    qi = pl.program_id(2)
    t0 = qi * tq
    blocks_per_tile = tq // SEL_BLOCK
    qt = _heads_on_lanes(qt_ref)
    for qa in qa_scr:
        qa[:ATT_DIM, :] = qt
        qa[ATT_DIM:, :] = jnp.zeros((LANES - ATT_DIM, qa.shape[1]), qa.dtype)
    for m_ref, a_ref in ((m_scr, acc_scr), (mw_scr, accw_scr)):
        m_ref[...] = jnp.full(m_ref.shape, NEG, F32)
        a_ref[...] = jnp.zeros(a_ref.shape, F32)

    def stage_a(kv, slot, dummy=None, diag=False):
        kvc = jnp.maximum(kv, 0)
        k0 = pl.multiple_of(kvc * tq, tq)
        grp = pl.multiple_of((kvc * blocks_per_tile) // BIAS_GROUP * BIAS_GROUP, BIAS_GROUP)
        b8 = bias_ref[0, 0, 0, pl.ds(grp, BIAS_GROUP), :]
        if dummy is not None:
            b8 = jnp.where(dummy, NEG, b8)
        b16 = jnp.concatenate([b8, jnp.zeros_like(b8)], axis=0)
        qa = qa_scr[slot]
        qa[ATT_DIM:ATT_DIM + 2 * SUBLANES, :] = _tile_heads(b16).astype(qa.dtype)
        s = jnp.dot(ks_ref[0, 0, pl.ds(k0, tq), :], qa[...], preferred_element_type=F32)
        if diag:
            row = lax.broadcasted_iota(jnp.int32, (tq, tq), 0)
            lane = lax.broadcasted_iota(jnp.int32, (tq, tq), 1)
            s = s + _tile_heads(jnp.where(row <= lane, 0.0, NEG))
        s_scr[slot][...] = s
        mx_scr[slot][...] = jnp.max(s, axis=0, keepdims=True)

    def softmax_pv(slot, v_ref, k0, m_ref, acc_ref):
        m_old = m_ref[...]
        m_new = jnp.maximum(m_old, mx_scr[slot][...])
        alpha = jnp.exp2(m_old - m_new)
        p = jnp.exp2(s_scr[slot][...] - m_new).astype(MXU_DT)
        vt = _with_ones_row(v_ref[0, :, pl.ds(k0, tq)])
        acc_ref[...] = alpha * acc_ref[...] + jnp.dot(vt, p, preferred_element_type=F32)
        m_ref[...] = m_new

    def stage_b(kv, slot):
        softmax_pv(slot, vst_ref, pl.multiple_of(jnp.maximum(kv, 0) * tq, tq), m_scr, acc_scr)

    n_win = WINDOW // tq + 1
    wrow = lax.broadcasted_iota(jnp.int32, (tq, tq), 0)
    wlane = lax.broadcasted_iota(jnp.int32, (tq, tq), 1)

    def win_start(wi):
        return t0 - WINDOW + wi * tq

    def stage_a_win(wi, slot):
        start = win_start(wi)
        k0 = pl.multiple_of(jnp.maximum(start, 0), tq)
        s = jnp.dot(kw_ref[0, 0, pl.ds(k0, tq), :], qt, preferred_element_type=F32)
        if wi == 0:
            visible = (wrow > wlane) & (start >= 0)
        elif wi == n_win - 1:
            visible = wrow <= wlane
        else:
            visible = jnp.broadcast_to(start >= 0, (tq, tq))
        s = s + _tile_heads(jnp.where(visible, 0.0, NEG))
        s_scr[slot][...] = s
        mx_scr[slot][...] = jnp.max(s, axis=0, keepdims=True)

    def stage_b_win(wi, slot):
        softmax_pv(slot, vwt_ref, pl.multiple_of(jnp.maximum(win_start(wi), 0), tq), mw_scr, accw_scr)

    off = 1 - qi % 2
    n_pairs = (qi + 2) // 2
    stage_a(-off, 0, dummy=off == 1)

    def pair(first):
        stage_a(first + 1, 1)
        stage_b(first, 0)
        stage_a(first + 2, 0)
        stage_b(first + 1, 1)

    def quad_body(i, carry):
        pair(4 * i - off)
        pair(4 * i + 2 - off)
        return carry

    def pair_body(pr, carry):
        pair(2 * pr - off)
        return carry

    n_quads = (n_pairs - 1) // 2
    lax.fori_loop(0, n_quads, quad_body, 0)
    lax.fori_loop(2 * n_quads, n_pairs - 1, pair_body, 0)
    stage_a(qi, 1, diag=True)
    stage_b(qi - 1, 0)
    stage_a_win(0, 0)
    stage_b(qi, 1)
    for wi in range(1, n_win):
        stage_a_win(wi, wi % 2)
        stage_b_win(wi - 1, (wi - 1) % 2)
    stage_b_win(n_win - 1, (n_win - 1) % 2)

    o_win_all = accw_scr[:ATT_DIM, :] / accw_scr[ATT_DIM:ATT_DIM + 1, :]
    o_slc_all = acc_scr[:ATT_DIM, :] / acc_scr[ATT_DIM:ATT_DIM + 1, :]
    for r in range(ATT_REP):
        o_win = o_win_all[:, r * tq:(r + 1) * tq]
        o_slc = o_slc_all[:, r * tq:(r + 1) * tq]
        o_cmp = ocmp_ref[0, r * ATT_DIM:(r + 1) * ATT_DIM, :]
        gbase = (g * ATT_REP + r) * 3
        g_cmp = _sigmoid(gt_ref[0, pl.ds(gbase, 1), :])
        g_slc = _sigmoid(gt_ref[0, pl.ds(gbase + 1, 1), :])
        g_win = _sigmoid(gt_ref[0, pl.ds(gbase + 2, 1), :])
        o = g_cmp * o_cmp + g_slc * o_slc + g_win * o_win
        o = o * lax.rsqrt(jnp.mean(o * o, axis=0, keepdims=True) + RMS_EPS)
        o = o * gn_ref[r * ATT_DIM:(r + 1) * ATT_DIM, :]
        o_ref[0, r * ATT_DIM:(r + 1) * ATT_DIM, :] = o.astype(o_ref.dtype)


def _slc_call(qt, ks, vst, kw, vwt, bias, ocmp, smallt, gn, tq):
    b, _, _, s = qt.shape
    nsel = bias.shape[3]
    gw = ATT_REP * ATT_DIM
    vres = pl.BlockSpec((1, ATT_DIM, s), lambda i, g, j: (i, g, 0))
    return pl.pallas_call(
        functools.partial(_slc_body, tq=tq),
        grid=(b, ATT_KV_HEADS, s // tq),
        in_specs=[pl.BlockSpec((1, ATT_REP, ATT_DIM, tq), lambda i, g, j: (i, g, 0, j)),
                  pl.BlockSpec((1, 1, s, LANES), lambda i, g, j: (i, g, 0, 0)), vres,
                  pl.BlockSpec((1, 1, s, ATT_DIM), lambda i, g, j: (i, g, 0, 0)), vres,
                  pl.BlockSpec((1, 1, 1, nsel, tq), lambda i, g, j: (i, g, j, 0, 0)),
                  pl.BlockSpec((1, gw, tq), lambda i, g, j: (i, g, j)),
                  pl.BlockSpec((1, LANES, tq), lambda i, g, j: (i, 0, j)),
                  pl.BlockSpec((gw, tq), lambda i, g, j: (g, 0))],
        out_specs=pl.BlockSpec((1, gw, tq), lambda i, g, j: (i, g, j)),
        out_shape=jax.ShapeDtypeStruct((b, ATT_WIDTH, s), MXU_DT),
        scratch_shapes=[pltpu.VMEM((LANES, ATT_REP * tq), MXU_DT),
                        pltpu.VMEM((LANES, ATT_REP * tq), MXU_DT),
                        pltpu.VMEM((tq, ATT_REP * tq), F32),
                        pltpu.VMEM((tq, ATT_REP * tq), F32),
                        pltpu.VMEM((1, ATT_REP * tq), F32),
                        pltpu.VMEM((1, ATT_REP * tq), F32),
                        pltpu.VMEM((1, ATT_REP * tq), F32),
                        pltpu.VMEM((ATT_DIM + 2 * SUBLANES, ATT_REP * tq), F32),
                        pltpu.VMEM((1, ATT_REP * tq), F32),
                        pltpu.VMEM((ATT_DIM + 2 * SUBLANES, ATT_REP * tq), F32)],
        compiler_params=_params(3),
        name="slc",
    )(qt, ks, vst, kw, vwt, bias, ocmp, smallt, gn)


def _mlstm_body(mqk_ref, mv_ref, mo_ref, sm_ref, smt_ref, cw_ref, cb_ref, gbr_ref, gbc_ref, ng_ref,
                o_ref, c_scr, m_scr, tail_scr, *, chunk):
    @pl.when(pl.program_id(1) == 0)
    def _():
        c_scr[...] = jnp.zeros(c_scr.shape, F32)
        m_scr[...] = jnp.full(m_scr.shape, NEG, F32)
        tail_scr[...] = jnp.zeros(tail_scr.shape, F32)

    x = mqk_ref[0].astype(F32)
    tail = tail_scr[...]
    y = cb_ref[...] + x * cw_ref[M_CONV - 1:M_CONV, :]
    for k in range(1, M_CONV):
        y = y + _shift_rows(x, tail, k) * cw_ref[M_CONV - 1 - k:M_CONV - k, :]
    tail_scr[...] = x[chunk - SUBLANES:, :]
    qk = _silu(y)
    sm = sm_ref[0] + gbr_ref[...]
    smt = smt_ref[0] + gbc_ref[...]
    row = lax.broadcasted_iota(jnp.int32, (chunk, chunk), 0)
    col = lax.broadcasted_iota(jnp.int32, (chunk, chunk), 1)
    causal = row >= col
    ones = jnp.ones((chunk, M_DIM), F32)

    def log_sigmoid(z):
        return jnp.minimum(z, 0.0) - jnp.log(1.0 + jnp.exp(-jnp.abs(z)))

    for h in range(M_HEADS):
        q = qk[:, h * M_DIM:(h + 1) * M_DIM]
        k = qk[:, M_WIDTH + h * M_DIM:M_WIDTH + (h + 1) * M_DIM] * (M_DIM ** -0.5)
        v = mv_ref[0, :, h * M_DIM:(h + 1) * M_DIM]
        i_col = sm[:, SMALL_I + h:SMALL_I + h + 1]
        lf_col = log_sigmoid(sm[:, SMALL_F + h:SMALL_F + h + 1])
        i_row = smt[SMALL_I + h:SMALL_I + h + 1, :]
        lf_row = log_sigmoid(smt[SMALL_F + h:SMALL_F + h + 1, :])
        b_col = jnp.sum(jnp.where(causal, lf_row, 0.0), axis=1, keepdims=True)
        b_row = jnp.sum(jnp.where(row <= col, lf_col, 0.0), axis=0, keepdims=True)
        m_prev = m_scr[h]
        dmat = jnp.where(causal, b_col - b_row + i_row, NEG)
        inter = b_col + m_prev
        mt = jnp.maximum(inter, jnp.max(dmat, axis=1, keepdims=True))
        w_intra = jnp.exp(dmat - mt)
        w_inter = jnp.exp(inter - mt)
        qkw = _dot_nt(q, k) * w_intra
        c_aug = c_scr[h]
        r1 = _dot(q, c_aug)
        num = w_inter * r1[:, :M_DIM] + _dot(qkw, v)
        den = w_inter * r1[:, M_DIM:M_DIM + 1] + jnp.sum(qkw, axis=1, keepdims=True)
        hh = num / jnp.maximum(jnp.abs(den), jnp.exp(-mt))
        b_last = b_col[chunk - 1:chunk, :]
        a_col = b_last - b_col + i_col
        m_new = jnp.maximum(b_last + m_prev, jnp.max(a_col, axis=0, keepdims=True))
        ws = jnp.exp(a_col - m_new)
        decay = jnp.exp(b_last + m_prev - m_new)
        v_aug = jnp.concatenate([v.astype(F32), ones], axis=1)
        c_scr[h] = decay * c_aug + _dot_tn(k * ws, v_aug)
        m_scr[h] = m_new
        hm = _sigmoid(mo_ref[0, :, h * M_DIM:(h + 1) * M_DIM].astype(F32)) * hh
        hm = hm * lax.rsqrt(jnp.mean(hm * hm, axis=1, keepdims=True) + RMS_EPS)
        o_ref[0, :, h * M_DIM:(h + 1) * M_DIM] = (hm * ng_ref[:, h * M_DIM:(h + 1) * M_DIM]).astype(o_ref.dtype)


def _mlstm_call(mqk, mv, mo, small, smallt, cw, cb, gbr, gbc, ng, chunk):
    b, s, _ = mv.shape
    row = lambda n: pl.BlockSpec((1, chunk, n), lambda i, j: (i, j, 0))
    full = lambda a: pl.BlockSpec(a.shape, lambda i, j: (0,) * a.ndim)
    return pl.pallas_call(
        functools.partial(_mlstm_body, chunk=chunk),
        grid=(b, s // chunk),
        in_specs=[row(2 * M_WIDTH), row(M_WIDTH), row(M_WIDTH), row(LANES),
                  pl.BlockSpec((1, LANES, chunk), lambda i, j: (i, 0, j)),
                  full(cw), full(cb), full(gbr), full(gbc), full(ng)],
        out_specs=row(M_WIDTH),
        out_shape=jax.ShapeDtypeStruct((b, s, M_WIDTH), MXU_DT),
        scratch_shapes=[pltpu.VMEM((M_HEADS, M_DIM, 2 * M_DIM), F32),
                        pltpu.VMEM((M_HEADS, 1, 1), F32),
                        pltpu.VMEM((SUBLANES, 2 * M_WIDTH), F32)],
        compiler_params=_params(2),
        name="mlstm",
    )(mqk, mv, mo, small, smallt, cw, cb, gbr, gbc, ng)


def _mix_ffn_body(x_ref, attt_ref, hm_ref, wo_ref, g1_ref, sc_ref, sh_ref, g2_ref, ng_ref,
                  wup_ref, cw_ref, cb_ref, wdn_ref, fg_ref, o_ref, tail_scr, g_scr, *, d_ff, tf, tiles_per_seq, final):
    @pl.when(pl.program_id(0) % tiles_per_seq == 0)
    def _():
        tail_scr[...] = jnp.zeros(tail_scr.shape, F32)

    d = x_ref.shape[-1]
    mix = _dot_tn(attt_ref[0], wo_ref[:ATT_WIDTH, :]) + _dot(hm_ref[0], wo_ref[ATT_WIDTH:, :])
    x = x_ref[0] + g1_ref[0] * mix
    var = jnp.mean(x * x, axis=-1, keepdims=True)
    h = (x * lax.rsqrt(var + RMS_EPS)) * ng_ref[...]
    hb = (h * (1.0 + sc_ref[0]) + sh_ref[0]).astype(MXU_DT)
    for c in range(d_ff // tf):
        a = jnp.dot(hb, wup_ref[:, c * tf:(c + 1) * tf], preferred_element_type=F32)
        v = jnp.dot(hb, wup_ref[:, d_ff + c * tf:d_ff + (c + 1) * tf], preferred_element_type=F32)
        tail = tail_scr[c]
        y = cb_ref[:, c * tf:(c + 1) * tf] + a * cw_ref[FFN_CONV - 1:FFN_CONV, c * tf:(c + 1) * tf]
        for k in range(1, FFN_CONV):
            y = y + _shift_rows(a, tail, k) * cw_ref[FFN_CONV - 1 - k:FFN_CONV - k, c * tf:(c + 1) * tf]
        tail_scr[c] = a[a.shape[0] - SUBLANES:, :]
        g_scr[:, c * tf:(c + 1) * tf] = (_silu(y) * v).astype(g_scr.dtype)
    out = x + g2_ref[0] * jnp.dot(g_scr[...], wdn_ref[...], preferred_element_type=F32)
    if final:
        var = jnp.mean(out * out, axis=-1, keepdims=True)
        out = (out * lax.rsqrt(var + RMS_EPS)) * fg_ref[...]
    o_ref[0] = out


def _mix_ffn_call(x, attt, hm, wo, layer, g1, sc, sh, g2, ng, wup, cw, cb, wdn, fg, tm, tf, final):
    b, s, d = x.shape
    d_ff = wdn.shape[1]
    stacked = lambda a: pl.BlockSpec((None,) + a.shape[1:], lambda i: (layer,) + (0,) * (a.ndim - 1))
    tiles_per_seq = s // tm
    row = lambda n: pl.BlockSpec((1, tm, n), lambda i: (i // tiles_per_seq, i % tiles_per_seq, 0))
    vec = pl.BlockSpec((1, 1, d), lambda i: (i // tiles_per_seq, 0, 0))
    full = lambda a: pl.BlockSpec(a.shape, lambda i: (0,) * a.ndim)
    return pl.pallas_call(
        functools.partial(_mix_ffn_body, d_ff=d_ff, tf=tf, tiles_per_seq=tiles_per_seq, final=final),
        grid=(b * tiles_per_seq,),
        in_specs=[row(d),
                  pl.BlockSpec((1, ATT_WIDTH, tm), lambda i: (i // tiles_per_seq, 0, i % tiles_per_seq)),
                  row(M_WIDTH), stacked(wo), vec, vec, vec, vec, full(ng),
                  stacked(wup), full(cw), full(cb), stacked(wdn), full(fg)],
        out_specs=row(d),
        out_shape=jax.ShapeDtypeStruct((b, s, d), F32),
        scratch_shapes=[pltpu.VMEM((d_ff // tf, SUBLANES, tf), F32),
                        pltpu.VMEM((tm, d_ff), MXU_DT)],
        compiler_params=_params(1),
        name="mix_ffn",
    )(x, attt, hm, wo, g1, sc, sh, g2, ng, wup, cw, cb, wdn, fg)


def _pick_tile(n, pref):
    t = min(n, pref)
    assert n % t == 0, (n, t)
    return t


def _pack_w_in(w_in):
    att_cols = ATT_WIDTH + 6 * KV_WIDTH
    n_gate = 3 * ATT_HEADS
    o = att_cols
    gates = w_in[..., o:o + n_gate]; o += n_gate
    mqk = w_in[..., o:o + 2 * M_WIDTH]; o += 2 * M_WIDTH
    mv = w_in[..., o:o + M_WIDTH]; o += M_WIDTH
    mif = w_in[..., o:o + 2 * M_HEADS]; o += 2 * M_HEADS
    mo = w_in[..., o:o + M_WIDTH]
    pad = jnp.zeros(w_in.shape[:-1] + (LANES - n_gate - 2 * M_HEADS,), w_in.dtype)
    return jnp.concatenate([w_in[..., :att_cols], mqk, mv, mo, gates, mif, pad], axis=-1).astype(MXU_DT)


def _expand_cmp_weights(pos, w1, w2):
    half = CMP_BLOCK // 2
    lead = w1.shape[:-2]
    w1r = w1.reshape(lead + (CMP_BLOCK, ATT_DIM, ATT_DIM))
    eye = jnp.eye(ATT_KV_HEADS, dtype=w1.dtype)

    def expand(wl):
        z = wl[..., :, None, :, None, :] * eye[None, :, None, :, None]
        return z.reshape(lead + (half * KV_WIDTH, KV_WIDTH)).astype(MXU_DT)

    def expand_pos(pl_):
        z = jnp.broadcast_to(pl_[..., :, None, :], lead + (half, ATT_KV_HEADS, ATT_DIM))
        return z.reshape(lead + (1, half * KV_WIDTH))

    w2e = (w2[..., None, :, None, :] * eye[:, None, :, None]).reshape(lead + (KV_WIDTH, KV_WIDTH))
    return (expand_pos(pos[..., :half, :]), expand_pos(pos[..., half:, :]),
            expand(w1r[..., :half, :, :]), expand(w1r[..., half:, :, :]), w2e.astype(MXU_DT))


def kernel(x, c, positions, norm1_g, norm2_g, ada_w, ada_b, w_in, cmp_pos, cmp_w1, cmp_w2, att_norm_g,
           m_conv_w, m_conv_b, m_gate_b, m_norm_g, w_out, ffn_up, ffn_conv_w, ffn_conv_b, ffn_down, final_g):
    b, s, d = x.shape
    depth = ada_w.shape[0]
    d_ff = ffn_down.shape[1]
    assert s % (CMP_STRIDE * SUBLANES) == 0 and d % LANES == 0 and b <= SUBLANES
    tm_proj = _pick_tile(s, 512)
    tm_ffn = _pick_tile(s, 512)
    tq = _pick_tile(s, 256)
    chunk = _pick_tile(s, 256)
    tf = 256
    assert d_ff % tf == 0 and WINDOW % tq == 0 and s >= WINDOW + tq
    nc = s // CMP_STRIDE
    nsel = s // SEL_BLOCK
    k_top = min(SEL_TOPK, nsel)

    c8 = jnp.zeros((SUBLANES, d), F32).at[:b].set(c)
    mod = _ada_call(c8, ada_w, ada_b.reshape(depth, 1, 6 * d), _pick_tile(6 * d, 1536))

    def mod_vec(l, k):
        return mod[l, :b, k * d:(k + 1) * d].reshape(b, 1, d)

    lane_d = jnp.arange(LANES) % ATT_DIM
    inv_freq = ROPE_THETA ** (-(2 * (lane_d % ROPE_HALF)).astype(F32) / ROPE_DIM)
    rc, rs1, rs2 = _rope_call(positions.reshape(b, s, 1), inv_freq.reshape(1, LANES), _pick_tile(s, 1024))

    c_start = jnp.arange(nc)[None, :] * CMP_STRIDE
    s_start = jnp.arange(nsel)[:, None] * SEL_BLOCK
    ovt = ((c_start <= s_start + SEL_BLOCK - 1) & (c_start + CMP_BLOCK - 1 >= s_start)
           & (jnp.arange(nc)[None, :] < nc - 1)).astype(MXU_DT)

    w_in_p = _pack_w_in(w_in)
    w_out_b, ffn_up_b, ffn_down_b = (w.astype(MXU_DT) for w in (w_out, ffn_up, ffn_down))
    cmp_w = _expand_cmp_weights(cmp_pos, cmp_w1, cmp_w2)

    for l in range(depth):
        outs = _proj_call(x, mod_vec(l, 1), mod_vec(l, 0), norm1_g[l].reshape(1, d), w_in_p, l,
                          rc, rs1, rs2, tm_proj)
        q, kc, vc, ks, vst, kw, vwt, mqk, mv, mo, small, smallt = outs
        kcmp, _ = _compress_call(kc.reshape(b, nc, CMP_STRIDE * KV_WIDTH), *(w[l, 0] for w in cmp_w))
        _, vcmpt = _compress_call(vc.reshape(b, nc, CMP_STRIDE * KV_WIDTH), *(w[l, 1] for w in cmp_w))
        ocmp, bias = _cmp_call(q, kcmp, vcmpt, ovt, tq, k_top)
        gn = jnp.broadcast_to(att_norm_g[l][:, None], (ATT_WIDTH, tq))
        attt = _slc_call(q, ks, vst, kw, vwt, bias, ocmp, smallt, gn, tq)
        gbr = jnp.zeros((1, LANES), F32).at[0, SMALL_I:SMALL_I + 2 * M_HEADS].set(m_gate_b[l])
        hm = _mlstm_call(mqk, mv, mo, small, smallt, m_conv_w[l], m_conv_b[l].reshape(1, -1),
                         gbr, gbr.reshape(LANES, 1), m_norm_g[l].reshape(1, -1), chunk)
        x = _mix_ffn_call(x, attt, hm, w_out_b, l, mod_vec(l, 2), mod_vec(l, 4), mod_vec(l, 3),
                          mod_vec(l, 5), norm2_g[l].reshape(1, d), ffn_up_b, ffn_conv_w[l],
                          ffn_conv_b[l].reshape(1, -1), ffn_down_b, final_g.reshape(1, d),
                          tm_ffn, tf, l == depth - 1)
    return x
```

```python
import functools

import jax
import jax.numpy as jnp
from jax import lax
from jax.experimental import pallas as pl
from jax.experimental.pallas import tpu as pltpu

F32 = jnp.float32
BF16 = jnp.bfloat16
MXU_DT = jnp.bfloat16

ATT_DIM = 64
ATT_HEADS = 8
ATT_KV_HEADS = 2
ATT_REP = ATT_HEADS // ATT_KV_HEADS
ATT_WIDTH = ATT_HEADS * ATT_DIM
KV_WIDTH = ATT_KV_HEADS * ATT_DIM
ROPE_DIM = ATT_DIM // 4
ROPE_HALF = ROPE_DIM // 2
ROPE_THETA = 500000.0
CMP_BLOCK = 32
CMP_STRIDE = 16
SEL_BLOCK = 64
SEL_TOPK = 16
WINDOW = 512
FORCE_BONUS = 1.0e4
M_HEADS = 4
M_DIM = 128
M_WIDTH = M_HEADS * M_DIM
M_CONV = 4
FFN_CONV = 3
RMS_EPS = 1e-6
NEG = -1e30
Q_SCALE = ATT_DIM ** -0.5 * 1.4426950408889634
BIAS_GROUP = 8

LANES = 128
SUBLANES = 8
VMEM_LIMIT = 56 * 1024 * 1024

COL_Q = 0
COL_KC = 512
COL_VC = 640
COL_KS = 768
COL_VS = 896
COL_KW = 1024
COL_VW = 1152
COL_MQK = 1280
COL_MV = 2304
COL_MO = 2816
COL_SMALL = 3328
IN_COLS_PACKED = 3456
SMALL_I = 3 * ATT_HEADS
SMALL_F = SMALL_I + M_HEADS

_NT = (((1,), (1,)), ((), ()))
_TN = (((0,), (0,)), ((), ()))


def _params(n_axes):
    return pltpu.CompilerParams(dimension_semantics=("arbitrary",) * n_axes,
                                vmem_limit_bytes=VMEM_LIMIT)


def _dot(a, b):
    return jnp.dot(a.astype(MXU_DT), b.astype(MXU_DT), preferred_element_type=F32)


def _dot_nt(a, b):
    return lax.dot_general(a.astype(MXU_DT), b.astype(MXU_DT), _NT, preferred_element_type=F32)


def _dot_tn(a, b):
    return lax.dot_general(a.astype(MXU_DT), b.astype(MXU_DT), _TN, preferred_element_type=F32)


def _sigmoid(x):
    return 1.0 / (1.0 + jnp.exp(-x))


def _silu(x):
    return x * _sigmoid(x)


def _shift_rows(x, prev8, k):
    rolled = pltpu.roll(x, k, 0)
    fix = pltpu.roll(prev8, k, 0)
    row = lax.broadcasted_iota(jnp.int32, fix.shape, 0)
    top = jnp.where(row < k, fix, rolled[:SUBLANES])
    return jnp.concatenate([top, rolled[SUBLANES:]], axis=0)


def _ada_body(c_ref, w_ref, b_ref, o_ref):
    c = c_ref[...]
    o_ref[0] = _dot(_silu(c), w_ref[0]) + b_ref[0]


def _ada_call(c8, ada_w, ada_b3, tn):
    depth, d, n = ada_w.shape
    return pl.pallas_call(
        _ada_body,
        grid=(depth, n // tn),
        in_specs=[pl.BlockSpec((SUBLANES, d), lambda l, j: (0, 0)),
                  pl.BlockSpec((1, d, tn), lambda l, j: (l, 0, j)),
                  pl.BlockSpec((1, 1, tn), lambda l, j: (l, 0, j))],
        out_specs=pl.BlockSpec((1, SUBLANES, tn), lambda l, j: (l, 0, j)),
        out_shape=jax.ShapeDtypeStruct((depth, SUBLANES, n), F32),
        compiler_params=_params(2),
        name="ada",
    )(c8, ada_w, ada_b3)


def _rope_body(pos_ref, freq_ref, rc_ref, rs1_ref, rs2_ref):
    pos = pos_ref[0].astype(F32)
    ang = pos * freq_ref[...]
    d = lax.broadcasted_iota(jnp.int32, ang.shape, 1) % ATT_DIM
    cos = jnp.cos(ang)
    sin = jnp.sin(ang)
    rc_ref[0] = jnp.where(d < ROPE_DIM, cos, 1.0)
    rs1_ref[0] = jnp.where(d < ROPE_HALF, -sin, 0.0)
    rs2_ref[0] = jnp.where((d >= ROPE_HALF) & (d < ROPE_DIM), sin, 0.0)


def _rope_call(pos3, freq, tm):
    b, s, _ = pos3.shape
    spec = pl.BlockSpec((1, tm, LANES), lambda i, j: (i, j, 0))
    shp = jax.ShapeDtypeStruct((b, s, LANES), F32)
    return pl.pallas_call(
        _rope_body,
        grid=(b, s // tm),
        in_specs=[pl.BlockSpec((1, tm, 1), lambda i, j: (i, j, 0)),
                  pl.BlockSpec((1, LANES), lambda i, j: (0, 0))],
        out_specs=[spec, spec, spec],
        out_shape=[shp, shp, shp],
        compiler_params=_params(2),
        name="rope",
    )(pos3, freq)


def _proj_body(x_ref, sc_ref, sh_ref, g_ref, w_ref, rc_ref, rs1_ref, rs2_ref,
               q_ref, kc_ref, vc_ref, ks_ref, vst_ref, kw_ref, vwt_ref,
               mqk_ref, mv_ref, mo_ref, sm_ref, smt_ref):
    x = x_ref[0]
    var = jnp.mean(x * x, axis=-1, keepdims=True)
    h = (x * lax.rsqrt(var + RMS_EPS)) * g_ref[...]
    h = h * (1.0 + sc_ref[0]) + sh_ref[0]
    hb = h.astype(MXU_DT)

    def mm(c0, n):
        return jnp.dot(hb, w_ref[:, c0:c0 + n], preferred_element_type=F32)

    rc, rs1, rs2 = rc_ref[0], rs1_ref[0], rs2_ref[0]

    def rope(y):
        return (y * rc + pltpu.roll(y, LANES - ROPE_HALF, 1) * rs1
                + pltpu.roll(y, ROPE_HALF, 1) * rs2)

    for j in range(ATT_HEADS // 4):
        y2 = mm(COL_Q + 2 * LANES * j, 2 * LANES)
        for i in range(2):
            yt = (rope(y2[:, LANES * i:LANES * (i + 1)]) * Q_SCALE).T
            q_ref[0, 4 * j + 2 * i] = yt[:ATT_DIM].astype(q_ref.dtype)
            q_ref[0, 4 * j + 2 * i + 1] = yt[ATT_DIM:].astype(q_ref.dtype)
    y2 = mm(COL_KC, 2 * KV_WIDTH)
    kc_ref[0] = rope(y2[:, :KV_WIDTH])
    vc_ref[0] = y2[:, KV_WIDTH:]
    y2 = mm(COL_KS, 2 * KV_WIDTH)
    y = rope(y2[:, :KV_WIDTH])
    lane = lax.broadcasted_iota(jnp.int32, y.shape, 1)
    t = pl.program_id(1) * y.shape[0] + lax.broadcasted_iota(jnp.int32, y.shape, 0)
    onehot = jnp.where(lane - ATT_DIM == (t // SEL_BLOCK) % BIAS_GROUP, 1.0, 0.0)
    ks_ref[0, 0] = jnp.where(lane < ATT_DIM, y, onehot).astype(ks_ref.dtype)
    ks_ref[0, 1] = jnp.where(lane < ATT_DIM, pltpu.roll(y, ATT_DIM, 1), onehot).astype(ks_ref.dtype)
    vst_ref[0] = y2[:, KV_WIDTH:].T.astype(vst_ref.dtype)
    y2 = mm(COL_KW, 2 * KV_WIDTH)
    y = rope(y2[:, :KV_WIDTH])
    kw_ref[0, 0] = y[:, :ATT_DIM].astype(kw_ref.dtype)
    kw_ref[0, 1] = y[:, ATT_DIM:].astype(kw_ref.dtype)
    vwt_ref[0] = y2[:, KV_WIDTH:].T.astype(vwt_ref.dtype)
    for j in range(2 * M_WIDTH // 256):
        mqk_ref[0, :, 256 * j:256 * (j + 1)] = mm(COL_MQK + 256 * j, 256).astype(mqk_ref.dtype)
    for j in range(M_WIDTH // 256):
        mv_ref[0, :, 256 * j:256 * (j + 1)] = mm(COL_MV + 256 * j, 256).astype(mv_ref.dtype)
        mo_ref[0, :, 256 * j:256 * (j + 1)] = mm(COL_MO + 256 * j, 256).astype(mo_ref.dtype)
    sm = mm(COL_SMALL, LANES)
    sm_ref[0] = sm
    smt_ref[0] = sm.T


def _proj_call(x, sc, sh, g, w, layer, rc, rs1, rs2, tm):
    b, s, d = x.shape
    row = lambda n: pl.BlockSpec((1, tm, n), lambda i, j: (i, j, 0))
    vec = pl.BlockSpec((1, 1, d), lambda i, j: (i, 0, 0))
    headed = lambda n, w: pl.BlockSpec((1, n, tm, w), lambda i, j: (i, 0, j, 0))
    tposed = pl.BlockSpec((1, LANES, tm), lambda i, j: (i, 0, j))
    sds = jax.ShapeDtypeStruct
    return pl.pallas_call(
        _proj_body,
        grid=(b, s // tm),
        in_specs=[row(d), vec, vec,
                  pl.BlockSpec((1, d), lambda i, j: (0, 0)),
                  pl.BlockSpec((None, d, IN_COLS_PACKED), lambda i, j: (layer, 0, 0)),
                  row(LANES), row(LANES), row(LANES)],
        out_specs=[pl.BlockSpec((1, ATT_HEADS, ATT_DIM, tm), lambda i, j: (i, 0, 0, j)),
                   row(KV_WIDTH), row(KV_WIDTH),
                   headed(ATT_KV_HEADS, LANES), tposed, headed(ATT_KV_HEADS, ATT_DIM), tposed,
                   row(2 * M_WIDTH), row(M_WIDTH), row(M_WIDTH), row(LANES), tposed],
        out_shape=[sds((b, ATT_HEADS, ATT_DIM, s), MXU_DT),
                   sds((b, s, KV_WIDTH), F32), sds((b, s, KV_WIDTH), F32),
                   sds((b, ATT_KV_HEADS, s, LANES), MXU_DT), sds((b, KV_WIDTH, s), MXU_DT),
                   sds((b, ATT_KV_HEADS, s, ATT_DIM), MXU_DT), sds((b, KV_WIDTH, s), MXU_DT),
                   sds((b, s, 2 * M_WIDTH), MXU_DT), sds((b, s, M_WIDTH), MXU_DT),
                   sds((b, s, M_WIDTH), MXU_DT), sds((b, s, LANES), F32), sds((b, LANES, s), F32)],
        compiler_params=_params(2),
        name="proj",
    )(x, sc, sh, g, w, rc, rs1, rs2)


def _compress_body(kr_ref, pa_ref, pb_ref, w1a_ref, w1b_ref, w2_ref, out_ref, outt_ref):
    kr = kr_ref[0]
    nc = kr.shape[0]
    ya = _dot(kr + pa_ref[...], w1a_ref[...])
    yb = _dot(kr + pb_ref[...], w1b_ref[...])
    pre = ya + pltpu.roll(yb, nc - 1, 0)
    out = _dot(_silu(pre), w2_ref[...])
    row = lax.broadcasted_iota(jnp.int32, out.shape, 0)
    out = jnp.where(row < nc - 1, out, 0.0)
    out_ref[0, 0] = out[:, :ATT_DIM].astype(out_ref.dtype)
    out_ref[0, 1] = out[:, ATT_DIM:].astype(out_ref.dtype)
    outt_ref[0] = out.T.astype(outt_ref.dtype)


def _compress_call(kr, pa, pb, w1a, w1b, w2e):
    b, nc, width = kr.shape
    full = lambda a: pl.BlockSpec(a.shape, lambda i: (0,) * a.ndim)
    return pl.pallas_call(
        _compress_body,
        grid=(b,),
        in_specs=[pl.BlockSpec((1, nc, width), lambda i: (i, 0, 0)),
                  full(pa), full(pb), full(w1a), full(w1b), full(w2e)],
        out_specs=[pl.BlockSpec((1, ATT_KV_HEADS, nc, ATT_DIM), lambda i: (i, 0, 0, 0)),
                   pl.BlockSpec((1, KV_WIDTH, nc), lambda i: (i, 0, 0))],
        out_shape=[jax.ShapeDtypeStruct((b, ATT_KV_HEADS, nc, ATT_DIM), MXU_DT),
                   jax.ShapeDtypeStruct((b, KV_WIDTH, nc), MXU_DT)],
        compiler_params=_params(1),
        name="compress",
    )(kr, pa, pb, w1a, w1b, w2e)


def _heads_on_lanes(qt_ref):
    return jnp.concatenate([qt_ref[0, r] for r in range(ATT_REP)], axis=1)


def _tile_heads(a):
    return jnp.concatenate([a] * ATT_REP, axis=1)


def _with_ones_row(vt):
    n = vt.shape[1]
    extra = jnp.where(lax.broadcasted_iota(jnp.int32, (2 * SUBLANES, n), 0) == 0, 1.0, 0.0)
    return jnp.concatenate([vt, extra.astype(vt.dtype)], axis=0)


def _cmp_body(qt_ref, kc_ref, vct_ref, ovt_ref, ocmp_ref, bias_ref, *, tq, k_top, n_split):
    t0 = pl.program_id(2) * tq
    nc_all = kc_ref.shape[2]
    nsel_all = ovt_ref.shape[0]
    rows_per_sel = SEL_BLOCK // CMP_STRIDE

    def compute(nc):
        nsel = nc // rows_per_sel
        row = lax.broadcasted_iota(jnp.int32, (nc, tq), 0)
        tpos = t0 + lax.broadcasted_iota(jnp.int32, (nc, tq), 1)
        mbias = jnp.where(row * CMP_STRIDE + (CMP_BLOCK - 1) <= tpos, 0.0, NEG)
        s = jnp.dot(kc_ref[0, 0, :nc, :], _heads_on_lanes(qt_ref), preferred_element_type=F32)
        s = s + _tile_heads(mbias)
        m = jnp.max(s, axis=0, keepdims=True)
        m = jnp.where(m > 0.5 * NEG, m, 0.0)
        p = jnp.exp2(s - m).astype(MXU_DT)
        oa = jnp.dot(_with_ones_row(vct_ref[0, :, :nc]), p, preferred_element_type=F32)
        d = oa[ATT_DIM:ATT_DIM + 1]
        inv = 1.0 / jnp.where(d > 0.0, d, 1.0)
        o = oa[:ATT_DIM] * inv
        impa = jnp.dot(ovt_ref[:nsel, :nc], p, preferred_element_type=F32) * inv
        imp = impa[:, :tq]
        for r in range(ATT_REP):
            ocmp_ref[0, r * ATT_DIM:(r + 1) * ATT_DIM, :] = o[:, r * tq:(r + 1) * tq]
            if r:
                imp = imp + impa[:, r * tq:(r + 1) * tq]
        jidx = lax.broadcasted_iota(jnp.int32, (nsel, tq), 0)
        tq_pos = t0 + lax.broadcasted_iota(jnp.int32, (nsel, tq), 1)
        jcur = tq_pos // SEL_BLOCK
        forced = (jidx == 0) | (jidx == jcur) | (jidx == jcur - 1)
        valid = jidx * SEL_BLOCK <= tq_pos
        if nsel < nsel_all:
            bias_ref[0, 0, 0, nsel:, :] = jnp.full((nsel_all - nsel, tq), NEG, F32)
        rounds = min(k_top, nsel)
        jcur_row = (t0 + lax.broadcasted_iota(jnp.int32, (1, tq), 1)) // SEL_BLOCK
        quota = rounds - jnp.minimum(jcur_row + 1, 3)
        score = jnp.where(valid & jnp.logical_not(forced), imp, NEG)
        cand = score > 0.5 * NEG
        n_rounds = max(rounds - 1, 0)
        n_free = max(rounds - 3, 0)

        def drop_max(cur_score, r, exact):
            top = jnp.max(cur_score, axis=0, keepdims=True)
            if exact:
                first = jnp.min(jnp.where(cur_score == top, jidx, nsel), axis=0, keepdims=True)
                hit = jidx == first
            else:
                hit = cur_score == top
            if r >= n_free:
                hit = hit & (quota > r)
            return jnp.where(hit, NEG, cur_score)

        fast = score
        for r in range(n_rounds):
            fast = drop_max(fast, r, exact=False)
        picked = cand & (fast < 0.5 * NEG)
        bias_ref[0, 0, 0, :nsel, :] = jnp.where(picked | (forced & valid), 0.0, NEG)
        n_picked = jnp.sum(jnp.where(picked, 1.0, 0.0), axis=0, keepdims=True)
        n_cand = jnp.sum(jnp.where(cand, 1.0, 0.0), axis=0, keepdims=True)
        want = jnp.minimum(n_cand, jnp.maximum(quota, 0).astype(F32))
        mismatch = jnp.max(jnp.abs(n_picked - want))

        @pl.when(mismatch > 0.0)
        def _():
            slow = score
            for r in range(n_rounds):
                slow = drop_max(slow, r, exact=True)
            bias_ref[0, 0, 0, :nsel, :] = jnp.where((cand & (slow < 0.5 * NEG)) | (forced & valid), 0.0, NEG)

    chunk_pos = nc_all // n_split * CMP_STRIDE
    need = jnp.minimum((t0 + tq - 1) // chunk_pos + 1, n_split)
    for k in range(1, n_split + 1):
        pl.when(need == k)(functools.partial(compute, nc_all // n_split * k))


def _cmp_call(qt, kcmp, vcmpt, ovt, tq, k_top):
    b, _, _, s = qt.shape
    nc = kcmp.shape[2]
    nsel = ovt.shape[0]
    nq = s // tq
    n_split = 4 if nc % (4 * LANES) == 0 and (nc // 4 * CMP_STRIDE) % tq == 0 else 1
    return pl.pallas_call(
        functools.partial(_cmp_body, tq=tq, k_top=k_top, n_split=n_split),
        grid=(b, ATT_KV_HEADS, nq),
        in_specs=[pl.BlockSpec((1, ATT_REP, ATT_DIM, tq), lambda i, g, j: (i, g, 0, j)),
                  pl.BlockSpec((1, 1, nc, ATT_DIM), lambda i, g, j: (i, g, 0, 0)),
                  pl.BlockSpec((1, ATT_DIM, nc), lambda i, g, j: (i, g, 0)),
                  pl.BlockSpec((nsel, nc), lambda i, g, j: (0, 0))],
        out_specs=[pl.BlockSpec((1, ATT_REP * ATT_DIM, tq), lambda i, g, j: (i, g, j)),
                   pl.BlockSpec((1, 1, 1, nsel, tq), lambda i, g, j: (i, g, j, 0, 0))],
        out_shape=[jax.ShapeDtypeStruct((b, ATT_WIDTH, s), F32),
                   jax.ShapeDtypeStruct((b, ATT_KV_HEADS, nq, nsel, tq), F32)],
        compiler_params=_params(3),
        name="cmp",
    )(qt, kcmp, vcmpt, ovt)


def _slc_body(qt_ref, ks_ref, vst_ref, kw_ref, vwt_ref, bias_ref, ocmp_ref, gt_ref, gn_ref,
              o_ref, qa0_scr, qa1_scr, s0_scr, s1_scr, mx0_scr, mx1_scr, m_scr, acc_scr,
              mw_scr, accw_scr, *, tq):
    qa_scr = (qa0_scr, qa1_scr)
    s_scr = (s0_scr, s1_scr)
    mx_scr = (mx0_scr, mx1_scr)
    g = pl.program_id(1)
    qi = pl.program_id(2)
    t0 = qi * tq
    blocks_per_tile = tq // SEL_BLOCK
    qt = _heads_on_lanes(qt_ref)
    for qa in qa_scr:
        qa[:ATT_DIM, :] = qt
        qa[ATT_DIM:, :] = jnp.zeros((LANES - ATT_DIM, qa.shape[1]), qa.dtype)
    for m_ref, a_ref in ((m_scr, acc_scr), (mw_scr, accw_scr)):
        m_ref[...] = jnp.full(m_ref.shape, NEG, F32)
        a_ref[...] = jnp.zeros(a_ref.shape, F32)

    def stage_a(kv, slot, dummy=None, diag=False):
        kvc = jnp.maximum(kv, 0)
        k0 = pl.multiple_of(kvc * tq, tq)
        grp = pl.multiple_of((kvc * blocks_per_tile) // BIAS_GROUP * BIAS_GROUP, BIAS_GROUP)
        b8 = bias_ref[0, 0, 0, pl.ds(grp, BIAS_GROUP), :]
        if dummy is not None:
            b8 = jnp.where(dummy, NEG, b8)
        b16 = jnp.concatenate([b8, jnp.zeros_like(b8)], axis=0)
        qa = qa_scr[slot]
        qa[ATT_DIM:ATT_DIM + 2 * SUBLANES, :] = _tile_heads(b16).astype(qa.dtype)
        s = jnp.dot(ks_ref[0, 0, pl.ds(k0, tq), :], qa[...], preferred_element_type=F32)
        if diag:
            row = lax.broadcasted_iota(jnp.int32, (tq, tq), 0)
            lane = lax.broadcasted_iota(jnp.int32, (tq, tq), 1)
            s = s + _tile_heads(jnp.where(row <= lane, 0.0, NEG))
        s_scr[slot][...] = s
        mx_scr[slot][...] = jnp.max(s, axis=0, keepdims=True)

    def softmax_pv(slot, v_ref, k0, m_ref, acc_ref):
        m_old = m_ref[...]
        m_new = jnp.maximum(m_old, mx_scr[slot][...])
        alpha = jnp.exp2(m_old - m_new)
        p = jnp.exp2(s_scr[slot][...] - m_new).astype(MXU_DT)
        vt = _with_ones_row(v_ref[0, :, pl.ds(k0, tq)])
        acc_ref[...] = alpha * acc_ref[...] + jnp.dot(vt, p, preferred_element_type=F32)
        m_ref[...] = m_new

    def stage_b(kv, slot):
        softmax_pv(slot, vst_ref, pl.multiple_of(jnp.maximum(kv, 0) * tq, tq), m_scr, acc_scr)

    n_win = WINDOW // tq + 1
    wrow = lax.broadcasted_iota(jnp.int32, (tq, tq), 0)
    wlane = lax.broadcasted_iota(jnp.int32, (tq, tq), 1)

    def win_start(wi):
        return t0 - WINDOW + wi * tq

    def stage_a_win(wi, slot):
        start = win_start(wi)
        k0 = pl.multiple_of(jnp.maximum(start, 0), tq)
        s = jnp.dot(kw_ref[0, 0, pl.ds(k0, tq), :], qt, preferred_element_type=F32)
        if wi == 0:
            visible = (wrow > wlane) & (start >= 0)
        elif wi == n_win - 1:
            visible = wrow <= wlane
        else:
            visible = jnp.broadcast_to(start >= 0, (tq, tq))
        s = s + _tile_heads(jnp.where(visible, 0.0, NEG))
        s_scr[slot][...] = s
        mx_scr[slot][...] = jnp.max(s, axis=0, keepdims=True)

    def stage_b_win(wi, slot):
        softmax_pv(slot, vwt_ref, pl.multiple_of(jnp.maximum(win_start(wi), 0), tq), mw_scr, accw_scr)

    off = 1 - qi % 2
    n_pairs = (qi + 2) // 2
    stage_a(-off, 0, dummy=off == 1)

    def pair(first):
        stage_a(first + 1, 1)
        stage_b(first, 0)
        stage_a(first + 2, 0)
        stage_b(first + 1, 1)

    def quad_body(i, carry):
        pair(4 * i - off)
        pair(4 * i + 2 - off)
        return carry

    def pair_body(pr, carry):
        pair(2 * pr - off)
        return carry

    n_quads = (n_pairs - 1) // 2
    lax.fori_loop(0, n_quads, quad_body, 0)
    lax.fori_loop(2 * n_quads, n_pairs - 1, pair_body, 0)
    stage_a(qi, 1, diag=True)
    stage_b(qi - 1, 0)
    stage_a_win(0, 0)
    stage_b(qi, 1)
    for wi in range(1, n_win):
        stage_a_win(wi, wi % 2)
        stage_b_win(wi - 1, (wi - 1) % 2)
    stage_b_win(n_win - 1, (n_win - 1) % 2)

    o_win_all = accw_scr[:ATT_DIM, :] / accw_scr[ATT_DIM:ATT_DIM + 1, :]
    o_slc_all = acc_scr[:ATT_DIM, :] / acc_scr[ATT_DIM:ATT_DIM + 1, :]
    for r in range(ATT_REP):
        o_win = o_win_all[:, r * tq:(r + 1) * tq]
        o_slc = o_slc_all[:, r * tq:(r + 1) * tq]
        o_cmp = ocmp_ref[0, r * ATT_DIM:(r + 1) * ATT_DIM, :]
        gbase = (g * ATT_REP + r) * 3
        g_cmp = _sigmoid(gt_ref[0, pl.ds(gbase, 1), :])
        g_slc = _sigmoid(gt_ref[0, pl.ds(gbase + 1, 1), :])
        g_win = _sigmoid(gt_ref[0, pl.ds(gbase + 2, 1), :])
        o = g_cmp * o_cmp + g_slc * o_slc + g_win * o_win
        o = o * lax.rsqrt(jnp.mean(o * o, axis=0, keepdims=True) + RMS_EPS)
        o = o * gn_ref[r * ATT_DIM:(r + 1) * ATT_DIM, :]
        o_ref[0, r * ATT_DIM:(r + 1) * ATT_DIM, :] = o.astype(o_ref.dtype)


def _slc_call(qt, ks, vst, kw, vwt, bias, ocmp, smallt, gn, tq):
    b, _, _, s = qt.shape
    nsel = bias.shape[3]
    gw = ATT_REP * ATT_DIM
    vres = pl.BlockSpec((1, ATT_DIM, s), lambda i, g, j: (i, g, 0))
    return pl.pallas_call(
        functools.partial(_slc_body, tq=tq),
        grid=(b, ATT_KV_HEADS, s // tq),
        in_specs=[pl.BlockSpec((1, ATT_REP, ATT_DIM, tq), lambda i, g, j: (i, g, 0, j)),
                  pl.BlockSpec((1, 1, s, LANES), lambda i, g, j: (i, g, 0, 0)), vres,
                  pl.BlockSpec((1, 1, s, ATT_DIM), lambda i, g, j: (i, g, 0, 0)), vres,
                  pl.BlockSpec((1, 1, 1, nsel, tq), lambda i, g, j: (i, g, j, 0, 0)),
                  pl.BlockSpec((1, gw, tq), lambda i, g, j: (i, g, j)),
                  pl.BlockSpec((1, LANES, tq), lambda i, g, j: (i, 0, j)),
                  pl.BlockSpec((gw, tq), lambda i, g, j: (g, 0))],
        out_specs=pl.BlockSpec((1, gw, tq), lambda i, g, j: (i, g, j)),
        out_shape=jax.ShapeDtypeStruct((b, ATT_WIDTH, s), MXU_DT),
        scratch_shapes=[pltpu.VMEM((LANES, ATT_REP * tq), MXU_DT),
                        pltpu.VMEM((LANES, ATT_REP * tq), MXU_DT),
                        pltpu.VMEM((tq, ATT_REP * tq), F32),
                        pltpu.VMEM((tq, ATT_REP * tq), F32),
                        pltpu.VMEM((1, ATT_REP * tq), F32),
                        pltpu.VMEM((1, ATT_REP * tq), F32),
                        pltpu.VMEM((1, ATT_REP * tq), F32),
                        pltpu.VMEM((ATT_DIM + 2 * SUBLANES, ATT_REP * tq), F32),
                        pltpu.VMEM((1, ATT_REP * tq), F32),
                        pltpu.VMEM((ATT_DIM + 2 * SUBLANES, ATT_REP * tq), F32)],
        compiler_params=_params(3),
        name="slc",
    )(qt, ks, vst, kw, vwt, bias, ocmp, smallt, gn)


def _mlstm_body(mqk_ref, mv_ref, mo_ref, sm_ref, smt_ref, cw_ref, cb_ref, gbr_ref, gbc_ref, ng_ref,
                o_ref, c_scr, m_scr, tail_scr, *, chunk):
    @pl.when(pl.program_id(1) == 0)
    def _():
        c_scr[...] = jnp.zeros(c_scr.shape, F32)
        m_scr[...] = jnp.full(m_scr.shape, NEG, F32)
        tail_scr[...] = jnp.zeros(tail_scr.shape, F32)

    x = mqk_ref[0].astype(F32)
    tail = tail_scr[...]
    y = cb_ref[...] + x * cw_ref[M_CONV - 1:M_CONV, :]
    for k in range(1, M_CONV):
        y = y + _shift_rows(x, tail, k) * cw_ref[M_CONV - 1 - k:M_CONV - k, :]
    tail_scr[...] = x[chunk - SUBLANES:, :]
    qk = _silu(y)
    sm = sm_ref[0] + gbr_ref[...]
    smt = smt_ref[0] + gbc_ref[...]
    row = lax.broadcasted_iota(jnp.int32, (chunk, chunk), 0)
    col = lax.broadcasted_iota(jnp.int32, (chunk, chunk), 1)
    causal = row >= col
    ones = jnp.ones((chunk, M_DIM), F32)

    def log_sigmoid(z):
        return jnp.minimum(z, 0.0) - jnp.log(1.0 + jnp.exp(-jnp.abs(z)))

    for h in range(M_HEADS):
        q = qk[:, h * M_DIM:(h + 1) * M_DIM]
        k = qk[:, M_WIDTH + h * M_DIM:M_WIDTH + (h + 1) * M_DIM] * (M_DIM ** -0.5)
        v = mv_ref[0, :, h * M_DIM:(h + 1) * M_DIM]
        i_col = sm[:, SMALL_I + h:SMALL_I + h + 1]
        lf_col = log_sigmoid(sm[:, SMALL_F + h:SMALL_F + h + 1])
        i_row = smt[SMALL_I + h:SMALL_I + h + 1, :]
        lf_row = log_sigmoid(smt[SMALL_F + h:SMALL_F + h + 1, :])
        b_col = jnp.sum(jnp.where(causal, lf_row, 0.0), axis=1, keepdims=True)
        b_row = jnp.sum(jnp.where(row <= col, lf_col, 0.0), axis=0, keepdims=True)
        m_prev = m_scr[h]
        dmat = jnp.where(causal, b_col - b_row + i_row, NEG)
        inter = b_col + m_prev
        mt = jnp.maximum(inter, jnp.max(dmat, axis=1, keepdims=True))
        w_intra = jnp.exp(dmat - mt)
        w_inter = jnp.exp(inter - mt)
        qkw = _dot_nt(q, k) * w_intra
        c_aug = c_scr[h]
        r1 = _dot(q, c_aug)
        num = w_inter * r1[:, :M_DIM] + _dot(qkw, v)
        den = w_inter * r1[:, M_DIM:M_DIM + 1] + jnp.sum(qkw, axis=1, keepdims=True)
        hh = num / jnp.maximum(jnp.abs(den), jnp.exp(-mt))
        b_last = b_col[chunk - 1:chunk, :]
        a_col = b_last - b_col + i_col
        m_new = jnp.maximum(b_last + m_prev, jnp.max(a_col, axis=0, keepdims=True))
        ws = jnp.exp(a_col - m_new)
        decay = jnp.exp(b_last + m_prev - m_new)
        v_aug = jnp.concatenate([v.astype(F32), ones], axis=1)
        c_scr[h] = decay * c_aug + _dot_tn(k * ws, v_aug)
        m_scr[h] = m_new
        hm = _sigmoid(mo_ref[0, :, h * M_DIM:(h + 1) * M_DIM].astype(F32)) * hh
        hm = hm * lax.rsqrt(jnp.mean(hm * hm, axis=1, keepdims=True) + RMS_EPS)
        o_ref[0, :, h * M_DIM:(h + 1) * M_DIM] = (hm * ng_ref[:, h * M_DIM:(h + 1) * M_DIM]).astype(o_ref.dtype)


def _mlstm_call(mqk, mv, mo, small, smallt, cw, cb, gbr, gbc, ng, chunk):
    b, s, _ = mv.shape
    row = lambda n: pl.BlockSpec((1, chunk, n), lambda i, j: (i, j, 0))
    full = lambda a: pl.BlockSpec(a.shape, lambda i, j: (0,) * a.ndim)
    return pl.pallas_call(
        functools.partial(_mlstm_body, chunk=chunk),
        grid=(b, s // chunk),
        in_specs=[row(2 * M_WIDTH), row(M_WIDTH), row(M_WIDTH), row(LANES),
                  pl.BlockSpec((1, LANES, chunk), lambda i, j: (i, 0, j)),
                  full(cw), full(cb), full(gbr), full(gbc), full(ng)],
        out_specs=row(M_WIDTH),
        out_shape=jax.ShapeDtypeStruct((b, s, M_WIDTH), MXU_DT),
        scratch_shapes=[pltpu.VMEM((M_HEADS, M_DIM, 2 * M_DIM), F32),
                        pltpu.VMEM((M_HEADS, 1, 1), F32),
                        pltpu.VMEM((SUBLANES, 2 * M_WIDTH), F32)],
        compiler_params=_params(2),
        name="mlstm",
    )(mqk, mv, mo, small, smallt, cw, cb, gbr, gbc, ng)


def _mix_ffn_body(x_ref, attt_ref, hm_ref, wo_ref, g1_ref, sc_ref, sh_ref, g2_ref, ng_ref,
                  wup_ref, cw_ref, cb_ref, wdn_ref, fg_ref, o_ref, tail_scr, g_scr, *, d_ff, tf, tiles_per_seq, final):
    @pl.when(pl.program_id(0) % tiles_per_seq == 0)
    def _():
        tail_scr[...] = jnp.zeros(tail_scr.shape, F32)

    d = x_ref.shape[-1]
    mix = _dot_tn(attt_ref[0], wo_ref[:ATT_WIDTH, :]) + _dot(hm_ref[0], wo_ref[ATT_WIDTH:, :])
    x = x_ref[0] + g1_ref[0] * mix
    var = jnp.mean(x * x, axis=-1, keepdims=True)
    h = (x * lax.rsqrt(var + RMS_EPS)) * ng_ref[...]
    hb = (h * (1.0 + sc_ref[0]) + sh_ref[0]).astype(MXU_DT)
    for c in range(d_ff // tf):
        a = jnp.dot(hb, wup_ref[:, c * tf:(c + 1) * tf], preferred_element_type=F32)
        v = jnp.dot(hb, wup_ref[:, d_ff + c * tf:d_ff + (c + 1) * tf], preferred_element_type=F32)
        tail = tail_scr[c]
        y = cb_ref[:, c * tf:(c + 1) * tf] + a * cw_ref[FFN_CONV - 1:FFN_CONV, c * tf:(c + 1) * tf]
        for k in range(1, FFN_CONV):
            y = y + _shift_rows(a, tail, k) * cw_ref[FFN_CONV - 1 - k:FFN_CONV - k, c * tf:(c + 1) * tf]
        tail_scr[c] = a[a.shape[0] - SUBLANES:, :]
        g_scr[:, c * tf:(c + 1) * tf] = (_silu(y) * v).astype(g_scr.dtype)
    out = x + g2_ref[0] * jnp.dot(g_scr[...], wdn_ref[...], preferred_element_type=F32)
    if final:
        var = jnp.mean(out * out, axis=-1, keepdims=True)
        out = (out * lax.rsqrt(var + RMS_EPS)) * fg_ref[...]
    o_ref[0] = out


def _mix_ffn_call(x, attt, hm, wo, layer, g1, sc, sh, g2, ng, wup, cw, cb, wdn, fg, tm, tf, final):
    b, s, d = x.shape
    d_ff = wdn.shape[1]
    stacked = lambda a: pl.BlockSpec((None,) + a.shape[1:], lambda i: (layer,) + (0,) * (a.ndim - 1))
    tiles_per_seq = s // tm
    row = lambda n: pl.BlockSpec((1, tm, n), lambda i: (i // tiles_per_seq, i % tiles_per_seq, 0))
    vec = pl.BlockSpec((1, 1, d), lambda i: (i // tiles_per_seq, 0, 0))
    full = lambda a: pl.BlockSpec(a.shape, lambda i: (0,) * a.ndim)
    return pl.pallas_call(
        functools.partial(_mix_ffn_body, d_ff=d_ff, tf=tf, tiles_per_seq=tiles_per_seq, final=final),
        grid=(b * tiles_per_seq,),
        in_specs=[row(d),
                  pl.BlockSpec((1, ATT_WIDTH, tm), lambda i: (i // tiles_per_seq, 0, i % tiles_per_seq)),
                  row(M_WIDTH), stacked(wo), vec, vec, vec, vec, full(ng),
                  stacked(wup), full(cw), full(cb), stacked(wdn), full(fg)],
        out_specs=row(d),
        out_shape=jax.ShapeDtypeStruct((b, s, d), F32),
        scratch_shapes=[pltpu.VMEM((d_ff // tf, SUBLANES, tf), F32),
                        pltpu.VMEM((tm, d_ff), MXU_DT)],
        compiler_params=_params(1),
        name="mix_ffn",
    )(x, attt, hm, wo, g1, sc, sh, g2, ng, wup, cw, cb, wdn, fg)


def _pick_tile(n, pref):
    t = min(n, pref)
    assert n % t == 0, (n, t)
    return t


def _pack_w_in(w_in):
    att_cols = ATT_WIDTH + 6 * KV_WIDTH
    n_gate = 3 * ATT_HEADS
    o = att_cols
    gates = w_in[..., o:o + n_gate]; o += n_gate
    mqk = w_in[..., o:o + 2 * M_WIDTH]; o += 2 * M_WIDTH
    mv = w_in[..., o:o + M_WIDTH]; o += M_WIDTH
    mif = w_in[..., o:o + 2 * M_HEADS]; o += 2 * M_HEADS
    mo = w_in[..., o:o + M_WIDTH]
    pad = jnp.zeros(w_in.shape[:-1] + (LANES - n_gate - 2 * M_HEADS,), w_in.dtype)
    return jnp.concatenate([w_in[..., :att_cols], mqk, mv, mo, gates, mif, pad], axis=-1).astype(MXU_DT)


def _expand_cmp_weights(pos, w1, w2):
    half = CMP_BLOCK // 2
    lead = w1.shape[:-2]
    w1r = w1.reshape(lead + (CMP_BLOCK, ATT_DIM, ATT_DIM))
    eye = jnp.eye(ATT_KV_HEADS, dtype=w1.dtype)

    def expand(wl):
        z = wl[..., :, None, :, None, :] * eye[None, :, None, :, None]
        return z.reshape(lead + (half * KV_WIDTH, KV_WIDTH)).astype(MXU_DT)

    def expand_pos(pl_):
        z = jnp.broadcast_to(pl_[..., :, None, :], lead + (half, ATT_KV_HEADS, ATT_DIM))
        return z.reshape(lead + (1, half * KV_WIDTH))

    w2e = (w2[..., None, :, None, :] * eye[:, None, :, None]).reshape(lead + (KV_WIDTH, KV_WIDTH))
    return (expand_pos(pos[..., :half, :]), expand_pos(pos[..., half:, :]),
            expand(w1r[..., :half, :, :]), expand(w1r[..., half:, :, :]), w2e.astype(MXU_DT))


def kernel(x, c, positions, norm1_g, norm2_g, ada_w, ada_b, w_in, cmp_pos, cmp_w1, cmp_w2, att_norm_g,
           m_conv_w, m_conv_b, m_gate_b, m_norm_g, w_out, ffn_up, ffn_conv_w, ffn_conv_b, ffn_down, final_g):
    b, s, d = x.shape
    depth = ada_w.shape[0]
    d_ff = ffn_down.shape[1]
    assert s % (CMP_STRIDE * SUBLANES) == 0 and d % LANES == 0 and b <= SUBLANES
    tm_proj = _pick_tile(s, 512)
    tm_ffn = _pick_tile(s, 512)
    tq = _pick_tile(s, 256)
    chunk = _pick_tile(s, 512)
    tf = 256
    assert d_ff % tf == 0 and WINDOW % tq == 0 and s >= WINDOW + tq
    nc = s // CMP_STRIDE
    nsel = s // SEL_BLOCK
    k_top = min(SEL_TOPK, nsel)

    c8 = jnp.zeros((SUBLANES, d), F32).at[:b].set(c)
    mod = _ada_call(c8, ada_w, ada_b.reshape(depth, 1, 6 * d), _pick_tile(6 * d, 1536))

    def mod_vec(l, k):
        return mod[l, :b, k * d:(k + 1) * d].reshape(b, 1, d)

    lane_d = jnp.arange(LANES) % ATT_DIM
    inv_freq = ROPE_THETA ** (-(2 * (lane_d % ROPE_HALF)).astype(F32) / ROPE_DIM)
    rc, rs1, rs2 = _rope_call(positions.reshape(b, s, 1), inv_freq.reshape(1, LANES), _pick_tile(s, 1024))

    c_start = jnp.arange(nc)[None, :] * CMP_STRIDE
    s_start = jnp.arange(nsel)[:, None] * SEL_BLOCK
    ovt = ((c_start <= s_start + SEL_BLOCK - 1) & (c_start + CMP_BLOCK - 1 >= s_start)
           & (jnp.arange(nc)[None, :] < nc - 1)).astype(MXU_DT)

    w_in_p = _pack_w_in(w_in)
    w_out_b, ffn_up_b, ffn_down_b = (w.astype(MXU_DT) for w in (w_out, ffn_up, ffn_down))
    cmp_w = _expand_cmp_weights(cmp_pos, cmp_w1, cmp_w2)

    for l in range(depth):
        outs = _proj_call(x, mod_vec(l, 1), mod_vec(l, 0), norm1_g[l].reshape(1, d), w_in_p, l,
                          rc, rs1, rs2, tm_proj)
        q, kc, vc, ks, vst, kw, vwt, mqk, mv, mo, small, smallt = outs
        kcmp, _ = _compress_call(kc.reshape(b, nc, CMP_STRIDE * KV_WIDTH), *(w[l, 0] for w in cmp_w))
        _, vcmpt = _compress_call(vc.reshape(b, nc, CMP_STRIDE * KV_WIDTH), *(w[l, 1] for w in cmp_w))
        ocmp, bias = _cmp_call(q, kcmp, vcmpt, ovt, tq, k_top)
        gn = jnp.broadcast_to(att_norm_g[l][:, None], (ATT_WIDTH, tq))
        attt = _slc_call(q, ks, vst, kw, vwt, bias, ocmp, smallt, gn, tq)
        gbr = jnp.zeros((1, LANES), F32).at[0, SMALL_I:SMALL_I + 2 * M_HEADS].set(m_gate_b[l])
        hm = _mlstm_call(mqk, mv, mo, small, smallt, m_conv_w[l], m_conv_b[l].reshape(1, -1),
                         gbr, gbr.reshape(LANES, 1), m_norm_g[l].reshape(1, -1), chunk)
        x = _mix_ffn_call(x, attt, hm, w_out_b, l, mod_vec(l, 2), mod_vec(l, 4), mod_vec(l, 3),
                          mod_vec(l, 5), norm2_g[l].reshape(1, d), ffn_up_b, ffn_conv_w[l],
                          ffn_conv_b[l].reshape(1, -1), ffn_down_b, final_g.reshape(1, d),
                          tm_ffn, tf, l == depth - 1)
    return x
```

```python
import functools

import jax
import jax.numpy as jnp
from jax import lax
from jax.experimental import pallas as pl
from jax.experimental.pallas import tpu as pltpu

F32 = jnp.float32
BF16 = jnp.bfloat16
MXU_DT = jnp.bfloat16

ATT_DIM = 64
ATT_HEADS = 8
ATT_KV_HEADS = 2
ATT_REP = ATT_HEADS // ATT_KV_HEADS
ATT_WIDTH = ATT_HEADS * ATT_DIM
KV_WIDTH = ATT_KV_HEADS * ATT_DIM
ROPE_DIM = ATT_DIM // 4
ROPE_HALF = ROPE_DIM // 2
ROPE_THETA = 500000.0
CMP_BLOCK = 32
CMP_STRIDE = 16
SEL_BLOCK = 64
SEL_TOPK = 16
WINDOW = 512
FORCE_BONUS = 1.0e4
M_HEADS = 4
M_DIM = 128
M_WIDTH = M_HEADS * M_DIM
M_CONV = 4
FFN_CONV = 3
RMS_EPS = 1e-6
NEG = -1e30
Q_SCALE = ATT_DIM ** -0.5 * 1.4426950408889634
BIAS_GROUP = 8

LANES = 128
SUBLANES = 8
VMEM_LIMIT = 56 * 1024 * 1024

COL_Q = 0
COL_KC = 512
COL_VC = 640
COL_KS = 768
COL_VS = 896
COL_KW = 1024
COL_VW = 1152
COL_MQK = 1280
COL_MV = 2304
COL_MO = 2816
COL_SMALL = 3328
IN_COLS_PACKED = 3456
SMALL_I = 3 * ATT_HEADS
SMALL_F = SMALL_I + M_HEADS

_NT = (((1,), (1,)), ((), ()))
_TN = (((0,), (0,)), ((), ()))


def _params(n_axes):
    return pltpu.CompilerParams(dimension_semantics=("arbitrary",) * n_axes,
                                vmem_limit_bytes=VMEM_LIMIT)


def _dot(a, b):
    return jnp.dot(a.astype(MXU_DT), b.astype(MXU_DT), preferred_element_type=F32)


def _dot_nt(a, b):
    return lax.dot_general(a.astype(MXU_DT), b.astype(MXU_DT), _NT, preferred_element_type=F32)


def _dot_tn(a, b):
    return lax.dot_general(a.astype(MXU_DT), b.astype(MXU_DT), _TN, preferred_element_type=F32)


def _sigmoid(x):
    return 1.0 / (1.0 + jnp.exp(-x))


def _silu(x):
    return x * _sigmoid(x)


def _shift_rows(x, prev8, k):
    rolled = pltpu.roll(x, k, 0)
    fix = pltpu.roll(prev8, k, 0)
    row = lax.broadcasted_iota(jnp.int32, fix.shape, 0)
    top = jnp.where(row < k, fix, rolled[:SUBLANES])
    return jnp.concatenate([top, rolled[SUBLANES:]], axis=0)


def _ada_body(c_ref, w_ref, b_ref, o_ref):
    c = c_ref[...]
    o_ref[0] = _dot(_silu(c), w_ref[0]) + b_ref[0]


def _ada_call(c8, ada_w, ada_b3, tn):
    depth, d, n = ada_w.shape
    return pl.pallas_call(
        _ada_body,
        grid=(depth, n // tn),
        in_specs=[pl.BlockSpec((SUBLANES, d), lambda l, j: (0, 0)),
                  pl.BlockSpec((1, d, tn), lambda l, j: (l, 0, j)),
                  pl.BlockSpec((1, 1, tn), lambda l, j: (l, 0, j))],
        out_specs=pl.BlockSpec((1, SUBLANES, tn), lambda l, j: (l, 0, j)),
        out_shape=jax.ShapeDtypeStruct((depth, SUBLANES, n), F32),
        compiler_params=_params(2),
        name="ada",
    )(c8, ada_w, ada_b3)


def _rope_body(pos_ref, freq_ref, rc_ref, rs1_ref, rs2_ref):
    pos = pos_ref[0].astype(F32)
    ang = pos * freq_ref[...]
    d = lax.broadcasted_iota(jnp.int32, ang.shape, 1) % ATT_DIM
    cos = jnp.cos(ang)
    sin = jnp.sin(ang)
    rc_ref[0] = jnp.where(d < ROPE_DIM, cos, 1.0)
    rs1_ref[0] = jnp.where(d < ROPE_HALF, -sin, 0.0)
    rs2_ref[0] = jnp.where((d >= ROPE_HALF) & (d < ROPE_DIM), sin, 0.0)


def _rope_call(pos3, freq, tm):
    b, s, _ = pos3.shape
    spec = pl.BlockSpec((1, tm, LANES), lambda i, j: (i, j, 0))
    shp = jax.ShapeDtypeStruct((b, s, LANES), F32)
    return pl.pallas_call(
        _rope_body,
        grid=(b, s // tm),
        in_specs=[pl.BlockSpec((1, tm, 1), lambda i, j: (i, j, 0)),
                  pl.BlockSpec((1, LANES), lambda i, j: (0, 0))],
        out_specs=[spec, spec, spec],
        out_shape=[shp, shp, shp],
        compiler_params=_params(2),
        name="rope",
    )(pos3, freq)


def _proj_body(x_ref, sc_ref, sh_ref, g_ref, w_ref, rc_ref, rs1_ref, rs2_ref,
               q_ref, kc_ref, vc_ref, ks_ref, vst_ref, kw_ref, vwt_ref,
               mqk_ref, mv_ref, mo_ref, sm_ref, smt_ref):
    x = x_ref[0]
    var = jnp.mean(x * x, axis=-1, keepdims=True)
    h = (x * lax.rsqrt(var + RMS_EPS)) * g_ref[...]
    h = h * (1.0 + sc_ref[0]) + sh_ref[0]
    hb = h.astype(MXU_DT)

    def mm(c0, n):
        return jnp.dot(hb, w_ref[:, c0:c0 + n], preferred_element_type=F32)

    rc, rs1, rs2 = rc_ref[0], rs1_ref[0], rs2_ref[0]

    def rope(y):
        return (y * rc + pltpu.roll(y, LANES - ROPE_HALF, 1) * rs1
                + pltpu.roll(y, ROPE_HALF, 1) * rs2)

    for j in range(ATT_HEADS // 4):
        y2 = mm(COL_Q + 2 * LANES * j, 2 * LANES)
        for i in range(2):
            yt = (rope(y2[:, LANES * i:LANES * (i + 1)]) * Q_SCALE).T
            q_ref[0, 4 * j + 2 * i] = yt[:ATT_DIM].astype(q_ref.dtype)
            q_ref[0, 4 * j + 2 * i + 1] = yt[ATT_DIM:].astype(q_ref.dtype)
    y2 = mm(COL_KC, 2 * KV_WIDTH)
    kc_ref[0] = rope(y2[:, :KV_WIDTH])
    vc_ref[0] = y2[:, KV_WIDTH:]
    y2 = mm(COL_KS, 2 * KV_WIDTH)
    y = rope(y2[:, :KV_WIDTH])
    lane = lax.broadcasted_iota(jnp.int32, y.shape, 1)
    t = pl.program_id(1) * y.shape[0] + lax.broadcasted_iota(jnp.int32, y.shape, 0)
    onehot = jnp.where(lane - ATT_DIM == (t // SEL_BLOCK) % BIAS_GROUP, 1.0, 0.0)
    ks_ref[0, 0] = jnp.where(lane < ATT_DIM, y, onehot).astype(ks_ref.dtype)
    ks_ref[0, 1] = jnp.where(lane < ATT_DIM, pltpu.roll(y, ATT_DIM, 1), onehot).astype(ks_ref.dtype)
    vst_ref[0] = y2[:, KV_WIDTH:].T.astype(vst_ref.dtype)
    y2 = mm(COL_KW, 2 * KV_WIDTH)
    y = rope(y2[:, :KV_WIDTH])
    kw_ref[0, 0] = y[:, :ATT_DIM].astype(kw_ref.dtype)
    kw_ref[0, 1] = y[:, ATT_DIM:].astype(kw_ref.dtype)
    vwt_ref[0] = y2[:, KV_WIDTH:].T.astype(vwt_ref.dtype)
    for j in range(2 * M_WIDTH // 256):
        mqk_ref[0, :, 256 * j:256 * (j + 1)] = mm(COL_MQK + 256 * j, 256).astype(mqk_ref.dtype)
    for j in range(M_WIDTH // 256):
        mv_ref[0, :, 256 * j:256 * (j + 1)] = mm(COL_MV + 256 * j, 256).astype(mv_ref.dtype)
        mo_ref[0, :, 256 * j:256 * (j + 1)] = mm(COL_MO + 256 * j, 256).astype(mo_ref.dtype)
    sm = mm(COL_SMALL, LANES)
    sm_ref[0] = sm
    smt_ref[0] = sm.T


def _proj_call(x, sc, sh, g, w, layer, rc, rs1, rs2, tm):
    b, s, d = x.shape
    row = lambda n: pl.BlockSpec((1, tm, n), lambda i, j: (i, j, 0))
    vec = pl.BlockSpec((1, 1, d), lambda i, j: (i, 0, 0))
    headed = lambda n, w: pl.BlockSpec((1, n, tm, w), lambda i, j: (i, 0, j, 0))
    tposed = pl.BlockSpec((1, LANES, tm), lambda i, j: (i, 0, j))
    sds = jax.ShapeDtypeStruct
    return pl.pallas_call(
        _proj_body,
        grid=(b, s // tm),
        in_specs=[row(d), vec, vec,
                  pl.BlockSpec((1, d), lambda i, j: (0, 0)),
                  pl.BlockSpec((None, d, IN_COLS_PACKED), lambda i, j: (layer, 0, 0)),
                  row(LANES), row(LANES), row(LANES)],
        out_specs=[pl.BlockSpec((1, ATT_HEADS, ATT_DIM, tm), lambda i, j: (i, 0, 0, j)),
                   row(KV_WIDTH), row(KV_WIDTH),
                   headed(ATT_KV_HEADS, LANES), tposed, headed(ATT_KV_HEADS, ATT_DIM), tposed,
                   row(2 * M_WIDTH), row(M_WIDTH), row(M_WIDTH), row(LANES), tposed],
        out_shape=[sds((b, ATT_HEADS, ATT_DIM, s), MXU_DT),
                   sds((b, s, KV_WIDTH), F32), sds((b, s, KV_WIDTH), F32),
                   sds((b, ATT_KV_HEADS, s, LANES), MXU_DT), sds((b, KV_WIDTH, s), MXU_DT),
                   sds((b, ATT_KV_HEADS, s, ATT_DIM), MXU_DT), sds((b, KV_WIDTH, s), MXU_DT),
                   sds((b, s, 2 * M_WIDTH), MXU_DT), sds((b, s, M_WIDTH), MXU_DT),
                   sds((b, s, M_WIDTH), MXU_DT), sds((b, s, LANES), F32), sds((b, LANES, s), F32)],
        compiler_params=_params(2),
        name="proj",
    )(x, sc, sh, g, w, rc, rs1, rs2)


def _compress_body(kr_ref, pa_ref, pb_ref, w1a_ref, w1b_ref, w2_ref, out_ref, outt_ref):
    kr = kr_ref[0]
    nc = kr.shape[0]
    ya = _dot(kr + pa_ref[...], w1a_ref[...])
    yb = _dot(kr + pb_ref[...], w1b_ref[...])
    pre = ya + pltpu.roll(yb, nc - 1, 0)
    out = _dot(_silu(pre), w2_ref[...])
    row = lax.broadcasted_iota(jnp.int32, out.shape, 0)
    out = jnp.where(row < nc - 1, out, 0.0)
    out_ref[0, 0] = out[:, :ATT_DIM].astype(out_ref.dtype)
    out_ref[0, 1] = out[:, ATT_DIM:].astype(out_ref.dtype)
    outt_ref[0] = out.T.astype(outt_ref.dtype)


def _compress_call(kr, pa, pb, w1a, w1b, w2e):
    b, nc, width = kr.shape
    full = lambda a: pl.BlockSpec(a.shape, lambda i: (0,) * a.ndim)
    return pl.pallas_call(
        _compress_body,
        grid=(b,),
        in_specs=[pl.BlockSpec((1, nc, width), lambda i: (i, 0, 0)),
                  full(pa), full(pb), full(w1a), full(w1b), full(w2e)],
        out_specs=[pl.BlockSpec((1, ATT_KV_HEADS, nc, ATT_DIM), lambda i: (i, 0, 0, 0)),
                   pl.BlockSpec((1, KV_WIDTH, nc), lambda i: (i, 0, 0))],
        out_shape=[jax.ShapeDtypeStruct((b, ATT_KV_HEADS, nc, ATT_DIM), MXU_DT),
                   jax.ShapeDtypeStruct((b, KV_WIDTH, nc), MXU_DT)],
        compiler_params=_params(1),
        name="compress",
    )(kr, pa, pb, w1a, w1b, w2e)


def _heads_on_lanes(qt_ref):
    return jnp.concatenate([qt_ref[0, r] for r in range(ATT_REP)], axis=1)


def _tile_heads(a):
    return jnp.concatenate([a] * ATT_REP, axis=1)


def _with_ones_row(vt):
    n = vt.shape[1]
    extra = jnp.where(lax.broadcasted_iota(jnp.int32, (2 * SUBLANES, n), 0) == 0, 1.0, 0.0)
    return jnp.concatenate([vt, extra.astype(vt.dtype)], axis=0)


def _cmp_body(qt_ref, kc_ref, vct_ref, ovt_ref, ocmp_ref, bias_ref, *, tq, k_top, n_split):
    t0 = pl.program_id(2) * tq
    nc_all = kc_ref.shape[2]
    nsel_all = ovt_ref.shape[0]
    rows_per_sel = SEL_BLOCK // CMP_STRIDE

    def compute(nc):
        nsel = nc // rows_per_sel
        row = lax.broadcasted_iota(jnp.int32, (nc, tq), 0)
        tpos = t0 + lax.broadcasted_iota(jnp.int32, (nc, tq), 1)
        mbias = jnp.where(row * CMP_STRIDE + (CMP_BLOCK - 1) <= tpos, 0.0, NEG)
        s = jnp.dot(kc_ref[0, 0, :nc, :], _heads_on_lanes(qt_ref), preferred_element_type=F32)
        s = s + _tile_heads(mbias)
        m = jnp.max(s, axis=0, keepdims=True)
        m = jnp.where(m > 0.5 * NEG, m, 0.0)
        p = jnp.exp2(s - m).astype(MXU_DT)
        oa = jnp.dot(_with_ones_row(vct_ref[0, :, :nc]), p, preferred_element_type=F32)
        d = oa[ATT_DIM:ATT_DIM + 1]
        inv = 1.0 / jnp.where(d > 0.0, d, 1.0)
        o = oa[:ATT_DIM] * inv
        impa = jnp.dot(ovt_ref[:nsel, :nc], p, preferred_element_type=F32) * inv
        imp = impa[:, :tq]
        for r in range(ATT_REP):
            ocmp_ref[0, r * ATT_DIM:(r + 1) * ATT_DIM, :] = o[:, r * tq:(r + 1) * tq]
            if r:
                imp = imp + impa[:, r * tq:(r + 1) * tq]
        jidx = lax.broadcasted_iota(jnp.int32, (nsel, tq), 0)
        tq_pos = t0 + lax.broadcasted_iota(jnp.int32, (nsel, tq), 1)
        jcur = tq_pos // SEL_BLOCK
        forced = (jidx == 0) | (jidx == jcur) | (jidx == jcur - 1)
        valid = jidx * SEL_BLOCK <= tq_pos
        if nsel < nsel_all:
            bias_ref[0, 0, 0, nsel:, :] = jnp.full((nsel_all - nsel, tq), NEG, F32)
        rounds = min(k_top, nsel)
        jcur_row = (t0 + lax.broadcasted_iota(jnp.int32, (1, tq), 1)) // SEL_BLOCK
        quota = rounds - jnp.minimum(jcur_row + 1, 3)
        score = jnp.where(valid & jnp.logical_not(forced), imp, NEG)
        cand = score > 0.5 * NEG
        n_rounds = max(rounds - 1, 0)
        n_free = max(rounds - 3, 0)

        def drop_max(cur_score, r, exact):
            top = jnp.max(cur_score, axis=0, keepdims=True)
            if exact:
                first = jnp.min(jnp.where(cur_score == top, jidx, nsel), axis=0, keepdims=True)
                hit = jidx == first
            else:
                hit = cur_score == top
            if r >= n_free:
                hit = hit & (quota > r)
            return jnp.where(hit, NEG, cur_score)

        fast = score
        for r in range(n_rounds):
            fast = drop_max(fast, r, exact=False)
        picked = cand & (fast < 0.5 * NEG)
        bias_ref[0, 0, 0, :nsel, :] = jnp.where(picked | (forced & valid), 0.0, NEG)
        n_picked = jnp.sum(jnp.where(picked, 1.0, 0.0), axis=0, keepdims=True)
        n_cand = jnp.sum(jnp.where(cand, 1.0, 0.0), axis=0, keepdims=True)
        want = jnp.minimum(n_cand, jnp.maximum(quota, 0).astype(F32))
        mismatch = jnp.max(jnp.abs(n_picked - want))

        @pl.when(mismatch > 0.0)
        def _():
            slow = score
            for r in range(n_rounds):
                slow = drop_max(slow, r, exact=True)
            bias_ref[0, 0, 0, :nsel, :] = jnp.where((cand & (slow < 0.5 * NEG)) | (forced & valid), 0.0, NEG)

    chunk_pos = nc_all // n_split * CMP_STRIDE
    need = jnp.minimum((t0 + tq - 1) // chunk_pos + 1, n_split)
    for k in range(1, n_split + 1):
        pl.when(need == k)(functools.partial(compute, nc_all // n_split * k))


def _cmp_call(qt, kcmp, vcmpt, ovt, tq, k_top):
    b, _, _, s = qt.shape
    nc = kcmp.shape[2]
    nsel = ovt.shape[0]
    nq = s // tq
    n_split = 1
    for cand in (8, 4):
        if nc % (cand * LANES) == 0 and (nc // cand * CMP_STRIDE) % tq == 0:
            n_split = cand
            break
    return pl.pallas_call(
        functools.partial(_cmp_body, tq=tq, k_top=k_top, n_split=n_split),
        grid=(b, ATT_KV_HEADS, nq),
        in_specs=[pl.BlockSpec((1, ATT_REP, ATT_DIM, tq), lambda i, g, j: (i, g, 0, j)),
                  pl.BlockSpec((1, 1, nc, ATT_DIM), lambda i, g, j: (i, g, 0, 0)),
                  pl.BlockSpec((1, ATT_DIM, nc), lambda i, g, j: (i, g, 0)),
                  pl.BlockSpec((nsel, nc), lambda i, g, j: (0, 0))],
        out_specs=[pl.BlockSpec((1, ATT_REP * ATT_DIM, tq), lambda i, g, j: (i, g, j)),
                   pl.BlockSpec((1, 1, 1, nsel, tq), lambda i, g, j: (i, g, j, 0, 0))],
        out_shape=[jax.ShapeDtypeStruct((b, ATT_WIDTH, s), F32),
                   jax.ShapeDtypeStruct((b, ATT_KV_HEADS, nq, nsel, tq), F32)],
        compiler_params=_params(3),
        name="cmp",
    )(qt, kcmp, vcmpt, ovt)


def _slc_body(qt_ref, ks_ref, vst_ref, kw_ref, vwt_ref, bias_ref, ocmp_ref, gt_ref, gn_ref,
              o_ref, qa0_scr, qa1_scr, s0_scr, s1_scr, mx0_scr, mx1_scr, m_scr, acc_scr,
              mw_scr, accw_scr, *, tq):
    qa_scr = (qa0_scr, qa1_scr)
    s_scr = (s0_scr, s1_scr)
    mx_scr = (mx0_scr, mx1_scr)
    g = pl.program_id(1)
    qi = pl.program_id(2)
    t0 = qi * tq
    blocks_per_tile = tq // SEL_BLOCK
    qt = _heads_on_lanes(qt_ref)
    for qa in qa_scr:
        qa[:ATT_DIM, :] = qt
        qa[ATT_DIM:, :] = jnp.zeros((LANES - ATT_DIM, qa.shape[1]), qa.dtype)
    for m_ref, a_ref in ((m_scr, acc_scr), (mw_scr, accw_scr)):
        m_ref[...] = jnp.full(m_ref.shape, NEG, F32)
        a_ref[...] = jnp.zeros(a_ref.shape, F32)

    def stage_a(kv, slot, dummy=None, diag=False):
        kvc = jnp.maximum(kv, 0)
        k0 = pl.multiple_of(kvc * tq, tq)
        grp = pl.multiple_of((kvc * blocks_per_tile) // BIAS_GROUP * BIAS_GROUP, BIAS_GROUP)
        b8 = bias_ref[0, 0, 0, pl.ds(grp, BIAS_GROUP), :]
        if dummy is not None:
            b8 = jnp.where(dummy, NEG, b8)
        b16 = jnp.concatenate([b8, jnp.zeros_like(b8)], axis=0)
        qa = qa_scr[slot]
        qa[ATT_DIM:ATT_DIM + 2 * SUBLANES, :] = _tile_heads(b16).astype(qa.dtype)
        s = jnp.dot(ks_ref[0, 0, pl.ds(k0, tq), :], qa[...], preferred_element_type=F32)
        if diag:
            row = lax.broadcasted_iota(jnp.int32, (tq, tq), 0)
            lane = lax.broadcasted_iota(jnp.int32, (tq, tq), 1)
            s = s + _tile_heads(jnp.where(row <= lane, 0.0, NEG))
        s_scr[slot][...] = s
        mx_scr[slot][...] = jnp.max(s, axis=0, keepdims=True)

    def softmax_pv(slot, v_ref, k0, m_ref, acc_ref):
        m_old = m_ref[...]
        m_new = jnp.maximum(m_old, mx_scr[slot][...])
        alpha = jnp.exp2(m_old - m_new)
        p = jnp.exp2(s_scr[slot][...] - m_new).astype(MXU_DT)
        vt = _with_ones_row(v_ref[0, :, pl.ds(k0, tq)])
        acc_ref[...] = alpha * acc_ref[...] + jnp.dot(vt, p, preferred_element_type=F32)
        m_ref[...] = m_new

    def stage_b(kv, slot):
        softmax_pv(slot, vst_ref, pl.multiple_of(jnp.maximum(kv, 0) * tq, tq), m_scr, acc_scr)

    n_win = WINDOW // tq + 1
    wrow = lax.broadcasted_iota(jnp.int32, (tq, tq), 0)
    wlane = lax.broadcasted_iota(jnp.int32, (tq, tq), 1)

    def win_start(wi):
        return t0 - WINDOW + wi * tq

    def stage_a_win(wi, slot):
        start = win_start(wi)
        k0 = pl.multiple_of(jnp.maximum(start, 0), tq)
        s = jnp.dot(kw_ref[0, 0, pl.ds(k0, tq), :], qt, preferred_element_type=F32)
        if wi == 0:
            visible = (wrow > wlane) & (start >= 0)
        elif wi == n_win - 1:
            visible = wrow <= wlane
        else:
            visible = jnp.broadcast_to(start >= 0, (tq, tq))
        s = s + _tile_heads(jnp.where(visible, 0.0, NEG))
        s_scr[slot][...] = s
        mx_scr[slot][...] = jnp.max(s, axis=0, keepdims=True)

    def stage_b_win(wi, slot):
        softmax_pv(slot, vwt_ref, pl.multiple_of(jnp.maximum(win_start(wi), 0), tq), mw_scr, accw_scr)

    off = 1 - qi % 2
    n_pairs = (qi + 2) // 2
    stage_a(-off, 0, dummy=off == 1)

    def pair(first):
        stage_a(first + 1, 1)
        stage_b(first, 0)
        stage_a(first + 2, 0)
        stage_b(first + 1, 1)

    def quad_body(i, carry):
        pair(4 * i - off)
        pair(4 * i + 2 - off)
        return carry

    def pair_body(pr, carry):
        pair(2 * pr - off)
        return carry

    n_quads = (n_pairs - 1) // 2
    lax.fori_loop(0, n_quads, quad_body, 0)
    lax.fori_loop(2 * n_quads, n_pairs - 1, pair_body, 0)
    stage_a(qi, 1, diag=True)
    stage_b(qi - 1, 0)
    stage_a_win(0, 0)
    stage_b(qi, 1)
    for wi in range(1, n_win):
        stage_a_win(wi, wi % 2)
        stage_b_win(wi - 1, (wi - 1) % 2)
    stage_b_win(n_win - 1, (n_win - 1) % 2)

    o_win_all = accw_scr[:ATT_DIM, :] / accw_scr[ATT_DIM:ATT_DIM + 1, :]
    o_slc_all = acc_scr[:ATT_DIM, :] / acc_scr[ATT_DIM:ATT_DIM + 1, :]
    for r in range(ATT_REP):
        o_win = o_win_all[:, r * tq:(r + 1) * tq]
        o_slc = o_slc_all[:, r * tq:(r + 1) * tq]
        o_cmp = ocmp_ref[0, r * ATT_DIM:(r + 1) * ATT_DIM, :]
        gbase = (g * ATT_REP + r) * 3
        g_cmp = _sigmoid(gt_ref[0, pl.ds(gbase, 1), :])
        g_slc = _sigmoid(gt_ref[0, pl.ds(gbase + 1, 1), :])
        g_win = _sigmoid(gt_ref[0, pl.ds(gbase + 2, 1), :])
        o = g_cmp * o_cmp + g_slc * o_slc + g_win * o_win
        o = o * lax.rsqrt(jnp.mean(o * o, axis=0, keepdims=True) + RMS_EPS)
        o = o * gn_ref[r * ATT_DIM:(r + 1) * ATT_DIM, :]
        o_ref[0, r * ATT_DIM:(r + 1) * ATT_DIM, :] = o.astype(o_ref.dtype)


def _slc_call(qt, ks, vst, kw, vwt, bias, ocmp, smallt, gn, tq):
    b, _, _, s = qt.shape
    nsel = bias.shape[3]
    gw = ATT_REP * ATT_DIM
    vres = pl.BlockSpec((1, ATT_DIM, s), lambda i, g, j: (i, g, 0))
    return pl.pallas_call(
        functools.partial(_slc_body, tq=tq),
        grid=(b, ATT_KV_HEADS, s // tq),
        in_specs=[pl.BlockSpec((1, ATT_REP, ATT_DIM, tq), lambda i, g, j: (i, g, 0, j)),
                  pl.BlockSpec((1, 1, s, LANES), lambda i, g, j: (i, g, 0, 0)), vres,
                  pl.BlockSpec((1, 1, s, ATT_DIM), lambda i, g, j: (i, g, 0, 0)), vres,
                  pl.BlockSpec((1, 1, 1, nsel, tq), lambda i, g, j: (i, g, j, 0, 0)),
                  pl.BlockSpec((1, gw, tq), lambda i, g, j: (i, g, j)),
                  pl.BlockSpec((1, LANES, tq), lambda i, g, j: (i, 0, j)),
                  pl.BlockSpec((gw, tq), lambda i, g, j: (g, 0))],
        out_specs=pl.BlockSpec((1, gw, tq), lambda i, g, j: (i, g, j)),
        out_shape=jax.ShapeDtypeStruct((b, ATT_WIDTH, s), MXU_DT),
        scratch_shapes=[pltpu.VMEM((LANES, ATT_REP * tq), MXU_DT),
                        pltpu.VMEM((LANES, ATT_REP * tq), MXU_DT),
                        pltpu.VMEM((tq, ATT_REP * tq), F32),
                        pltpu.VMEM((tq, ATT_REP * tq), F32),
                        pltpu.VMEM((1, ATT_REP * tq), F32),
                        pltpu.VMEM((1, ATT_REP * tq), F32),
                        pltpu.VMEM((1, ATT_REP * tq), F32),
                        pltpu.VMEM((ATT_DIM + 2 * SUBLANES, ATT_REP * tq), F32),
                        pltpu.VMEM((1, ATT_REP * tq), F32),
                        pltpu.VMEM((ATT_DIM + 2 * SUBLANES, ATT_REP * tq), F32)],
        compiler_params=_params(3),
        name="slc",
    )(qt, ks, vst, kw, vwt, bias, ocmp, smallt, gn)


def _mlstm_body(mqk_ref, mv_ref, mo_ref, sm_ref, smt_ref, cw_ref, cb_ref, gbr_ref, gbc_ref, ng_ref,
                o_ref, c_scr, m_scr, tail_scr, *, chunk):
    @pl.when(pl.program_id(1) == 0)
    def _():
        c_scr[...] = jnp.zeros(c_scr.shape, F32)
        m_scr[...] = jnp.full(m_scr.shape, NEG, F32)
        tail_scr[...] = jnp.zeros(tail_scr.shape, F32)

    x = mqk_ref[0].astype(F32)
    tail = tail_scr[...]
    y = cb_ref[...] + x * cw_ref[M_CONV - 1:M_CONV, :]
    for k in range(1, M_CONV):
        y = y + _shift_rows(x, tail, k) * cw_ref[M_CONV - 1 - k:M_CONV - k, :]
    tail_scr[...] = x[chunk - SUBLANES:, :]
    qk = _silu(y)
    sm = sm_ref[0] + gbr_ref[...]
    smt = smt_ref[0] + gbc_ref[...]
    row = lax.broadcasted_iota(jnp.int32, (chunk, chunk), 0)
    col = lax.broadcasted_iota(jnp.int32, (chunk, chunk), 1)
    causal = row >= col
    ones = jnp.ones((chunk, M_DIM), F32)

    def log_sigmoid(z):
        return jnp.minimum(z, 0.0) - jnp.log(1.0 + jnp.exp(-jnp.abs(z)))

    for h in range(M_HEADS):
        q = qk[:, h * M_DIM:(h + 1) * M_DIM]
        k = qk[:, M_WIDTH + h * M_DIM:M_WIDTH + (h + 1) * M_DIM] * (M_DIM ** -0.5)
        v = mv_ref[0, :, h * M_DIM:(h + 1) * M_DIM]
        i_col = sm[:, SMALL_I + h:SMALL_I + h + 1]
        lf_col = log_sigmoid(sm[:, SMALL_F + h:SMALL_F + h + 1])
        i_row = smt[SMALL_I + h:SMALL_I + h + 1, :]
        lf_row = log_sigmoid(smt[SMALL_F + h:SMALL_F + h + 1, :])
        b_col = jnp.sum(jnp.where(causal, lf_row, 0.0), axis=1, keepdims=True)
        b_row = jnp.sum(jnp.where(row <= col, lf_col, 0.0), axis=0, keepdims=True)
        m_prev = m_scr[h]
        dmat = jnp.where(causal, b_col - b_row + i_row, NEG)
        inter = b_col + m_prev
        mt = jnp.maximum(inter, jnp.max(dmat, axis=1, keepdims=True))
        w_intra = jnp.exp(dmat - mt)
        w_inter = jnp.exp(inter - mt)
        qkw = _dot_nt(q, k) * w_intra
        c_aug = c_scr[h]
        r1 = _dot(q, c_aug)
        num = w_inter * r1[:, :M_DIM] + _dot(qkw, v)
        den = w_inter * r1[:, M_DIM:M_DIM + 1] + jnp.sum(qkw, axis=1, keepdims=True)
        hh = num / jnp.maximum(jnp.abs(den), jnp.exp(-mt))
        b_last = b_col[chunk - 1:chunk, :]
        a_col = b_last - b_col + i_col
        m_new = jnp.maximum(b_last + m_prev, jnp.max(a_col, axis=0, keepdims=True))
        ws = jnp.exp(a_col - m_new)
        decay = jnp.exp(b_last + m_prev - m_new)
        v_aug = jnp.concatenate([v.astype(F32), ones], axis=1)
        c_scr[h] = decay * c_aug + _dot_tn(k * ws, v_aug)
        m_scr[h] = m_new
        hm = _sigmoid(mo_ref[0, :, h * M_DIM:(h + 1) * M_DIM].astype(F32)) * hh
        hm = hm * lax.rsqrt(jnp.mean(hm * hm, axis=1, keepdims=True) + RMS_EPS)
        o_ref[0, :, h * M_DIM:(h + 1) * M_DIM] = (hm * ng_ref[:, h * M_DIM:(h + 1) * M_DIM]).astype(o_ref.dtype)


def _mlstm_call(mqk, mv, mo, small, smallt, cw, cb, gbr, gbc, ng, chunk):
    b, s, _ = mv.shape
    row = lambda n: pl.BlockSpec((1, chunk, n), lambda i, j: (i, j, 0))
    full = lambda a: pl.BlockSpec(a.shape, lambda i, j: (0,) * a.ndim)
    return pl.pallas_call(
        functools.partial(_mlstm_body, chunk=chunk),
        grid=(b, s // chunk),
        in_specs=[row(2 * M_WIDTH), row(M_WIDTH), row(M_WIDTH), row(LANES),
                  pl.BlockSpec((1, LANES, chunk), lambda i, j: (i, 0, j)),
                  full(cw), full(cb), full(gbr), full(gbc), full(ng)],
        out_specs=row(M_WIDTH),
        out_shape=jax.ShapeDtypeStruct((b, s, M_WIDTH), MXU_DT),
        scratch_shapes=[pltpu.VMEM((M_HEADS, M_DIM, 2 * M_DIM), F32),
                        pltpu.VMEM((M_HEADS, 1, 1), F32),
                        pltpu.VMEM((SUBLANES, 2 * M_WIDTH), F32)],
        compiler_params=_params(2),
        name="mlstm",
    )(mqk, mv, mo, small, smallt, cw, cb, gbr, gbc, ng)


def _mix_ffn_body(x_ref, attt_ref, hm_ref, wo_ref, g1_ref, sc_ref, sh_ref, g2_ref, ng_ref,
                  wup_ref, cw_ref, cb_ref, wdn_ref, fg_ref, o_ref, tail_scr, g_scr, *, d_ff, tf, tiles_per_seq, final):
    @pl.when(pl.program_id(0) % tiles_per_seq == 0)
    def _():
        tail_scr[...] = jnp.zeros(tail_scr.shape, F32)

    d = x_ref.shape[-1]
    mix = _dot_tn(attt_ref[0], wo_ref[:ATT_WIDTH, :]) + _dot(hm_ref[0], wo_ref[ATT_WIDTH:, :])
    x = x_ref[0] + g1_ref[0] * mix
    var = jnp.mean(x * x, axis=-1, keepdims=True)
    h = (x * lax.rsqrt(var + RMS_EPS)) * ng_ref[...]
    hb = (h * (1.0 + sc_ref[0]) + sh_ref[0]).astype(MXU_DT)
    for c in range(d_ff // tf):
        a = jnp.dot(hb, wup_ref[:, c * tf:(c + 1) * tf], preferred_element_type=F32)
        v = jnp.dot(hb, wup_ref[:, d_ff + c * tf:d_ff + (c + 1) * tf], preferred_element_type=F32)
        tail = tail_scr[c]
        y = cb_ref[:, c * tf:(c + 1) * tf] + a * cw_ref[FFN_CONV - 1:FFN_CONV, c * tf:(c + 1) * tf]
        for k in range(1, FFN_CONV):
            y = y + _shift_rows(a, tail, k) * cw_ref[FFN_CONV - 1 - k:FFN_CONV - k, c * tf:(c + 1) * tf]
        tail_scr[c] = a[a.shape[0] - SUBLANES:, :]
        g_scr[:, c * tf:(c + 1) * tf] = (_silu(y) * v).astype(g_scr.dtype)
    out = x + g2_ref[0] * jnp.dot(g_scr[...], wdn_ref[...], preferred_element_type=F32)
    if final:
        var = jnp.mean(out * out, axis=-1, keepdims=True)
        out = (out * lax.rsqrt(var + RMS_EPS)) * fg_ref[...]
    o_ref[0] = out


def _mix_ffn_call(x, attt, hm, wo, layer, g1, sc, sh, g2, ng, wup, cw, cb, wdn, fg, tm, tf, final):
    b, s, d = x.shape
    d_ff = wdn.shape[1]
    stacked = lambda a: pl.BlockSpec((None,) + a.shape[1:], lambda i: (layer,) + (0,) * (a.ndim - 1))
    tiles_per_seq = s // tm
    row = lambda n: pl.BlockSpec((1, tm, n), lambda i: (i // tiles_per_seq, i % tiles_per_seq, 0))
    vec = pl.BlockSpec((1, 1, d), lambda i: (i // tiles_per_seq, 0, 0))
    full = lambda a: pl.BlockSpec(a.shape, lambda i: (0,) * a.ndim)
    return pl.pallas_call(
        functools.partial(_mix_ffn_body, d_ff=d_ff, tf=tf, tiles_per_seq=tiles_per_seq, final=final),
        grid=(b * tiles_per_seq,),
        in_specs=[row(d),
                  pl.BlockSpec((1, ATT_WIDTH, tm), lambda i: (i // tiles_per_seq, 0, i % tiles_per_seq)),
                  row(M_WIDTH), stacked(wo), vec, vec, vec, vec, full(ng),
                  stacked(wup), full(cw), full(cb), stacked(wdn), full(fg)],
        out_specs=row(d),
        out_shape=jax.ShapeDtypeStruct((b, s, d), F32),
        scratch_shapes=[pltpu.VMEM((d_ff // tf, SUBLANES, tf), F32),
                        pltpu.VMEM((tm, d_ff), MXU_DT)],
        compiler_params=_params(1),
        name="mix_ffn",
    )(x, attt, hm, wo, g1, sc, sh, g2, ng, wup, cw, cb, wdn, fg)


def _pick_tile(n, pref):
    t = min(n, pref)
    assert n % t == 0, (n, t)
    return t


def _pack_w_in(w_in):
    att_cols = ATT_WIDTH + 6 * KV_WIDTH
    n_gate = 3 * ATT_HEADS
    o = att_cols
    gates = w_in[..., o:o + n_gate]; o += n_gate
    mqk = w_in[..., o:o + 2 * M_WIDTH]; o += 2 * M_WIDTH
    mv = w_in[..., o:o + M_WIDTH]; o += M_WIDTH
    mif = w_in[..., o:o + 2 * M_HEADS]; o += 2 * M_HEADS
    mo = w_in[..., o:o + M_WIDTH]
    pad = jnp.zeros(w_in.shape[:-1] + (LANES - n_gate - 2 * M_HEADS,), w_in.dtype)
    return jnp.concatenate([w_in[..., :att_cols], mqk, mv, mo, gates, mif, pad], axis=-1).astype(MXU_DT)


def _expand_cmp_weights(pos, w1, w2):
    half = CMP_BLOCK // 2
    lead = w1.shape[:-2]
    w1r = w1.reshape(lead + (CMP_BLOCK, ATT_DIM, ATT_DIM))
    eye = jnp.eye(ATT_KV_HEADS, dtype=w1.dtype)

    def expand(wl):
        z = wl[..., :, None, :, None, :] * eye[None, :, None, :, None]
        return z.reshape(lead + (half * KV_WIDTH, KV_WIDTH)).astype(MXU_DT)

    def expand_pos(pl_):
        z = jnp.broadcast_to(pl_[..., :, None, :], lead + (half, ATT_KV_HEADS, ATT_DIM))
        return z.reshape(lead + (1, half * KV_WIDTH))

    w2e = (w2[..., None, :, None, :] * eye[:, None, :, None]).reshape(lead + (KV_WIDTH, KV_WIDTH))
    return (expand_pos(pos[..., :half, :]), expand_pos(pos[..., half:, :]),
            expand(w1r[..., :half, :, :]), expand(w1r[..., half:, :, :]), w2e.astype(MXU_DT))


def kernel(x, c, positions, norm1_g, norm2_g, ada_w, ada_b, w_in, cmp_pos, cmp_w1, cmp_w2, att_norm_g,
           m_conv_w, m_conv_b, m_gate_b, m_norm_g, w_out, ffn_up, ffn_conv_w, ffn_conv_b, ffn_down, final_g):
    b, s, d = x.shape
    depth = ada_w.shape[0]
    d_ff = ffn_down.shape[1]
    assert s % (CMP_STRIDE * SUBLANES) == 0 and d % LANES == 0 and b <= SUBLANES
    tm_proj = _pick_tile(s, 512)
    tm_ffn = _pick_tile(s, 512)
    tq = _pick_tile(s, 256)
    chunk = _pick_tile(s, 512)
    tf = 256
    assert d_ff % tf == 0 and WINDOW % tq == 0 and s >= WINDOW + tq
    nc = s // CMP_STRIDE
    nsel = s // SEL_BLOCK
    k_top = min(SEL_TOPK, nsel)

    c8 = jnp.zeros((SUBLANES, d), F32).at[:b].set(c)
    mod = _ada_call(c8, ada_w, ada_b.reshape(depth, 1, 6 * d), _pick_tile(6 * d, 1536))

    def mod_vec(l, k):
        return mod[l, :b, k * d:(k + 1) * d].reshape(b, 1, d)

    lane_d = jnp.arange(LANES) % ATT_DIM
    inv_freq = ROPE_THETA ** (-(2 * (lane_d % ROPE_HALF)).astype(F32) / ROPE_DIM)
    rc, rs1, rs2 = _rope_call(positions.reshape(b, s, 1), inv_freq.reshape(1, LANES), _pick_tile(s, 1024))

    c_start = jnp.arange(nc)[None, :] * CMP_STRIDE
    s_start = jnp.arange(nsel)[:, None] * SEL_BLOCK
    ovt = ((c_start <= s_start + SEL_BLOCK - 1) & (c_start + CMP_BLOCK - 1 >= s_start)
           & (jnp.arange(nc)[None, :] < nc - 1)).astype(MXU_DT)

    w_in_p = _pack_w_in(w_in)
    w_out_b, ffn_up_b, ffn_down_b = (w.astype(MXU_DT) for w in (w_out, ffn_up, ffn_down))
    cmp_w = _expand_cmp_weights(cmp_pos, cmp_w1, cmp_w2)

    for l in range(depth):
        outs = _proj_call(x, mod_vec(l, 1), mod_vec(l, 0), norm1_g[l].reshape(1, d), w_in_p, l,
                          rc, rs1, rs2, tm_proj)
        q, kc, vc, ks, vst, kw, vwt, mqk, mv, mo, small, smallt = outs
        kcmp, _ = _compress_call(kc.reshape(b, nc, CMP_STRIDE * KV_WIDTH), *(w[l, 0] for w in cmp_w))
        _, vcmpt = _compress_call(vc.reshape(b, nc, CMP_STRIDE * KV_WIDTH), *(w[l, 1] for w in cmp_w))
        ocmp, bias = _cmp_call(q, kcmp, vcmpt, ovt, tq, k_top)
        gn = jnp.broadcast_to(att_norm_g[l][:, None], (ATT_WIDTH, tq))
        attt = _slc_call(q, ks, vst, kw, vwt, bias, ocmp, smallt, gn, tq)
        gbr = jnp.zeros((1, LANES), F32).at[0, SMALL_I:SMALL_I + 2 * M_HEADS].set(m_gate_b[l])
        hm = _mlstm_call(mqk, mv, mo, small, smallt, m_conv_w[l], m_conv_b[l].reshape(1, -1),
                         gbr, gbr.reshape(LANES, 1), m_norm_g[l].reshape(1, -1), chunk)
        x = _mix_ffn_call(x, attt, hm, w_out_b, l, mod_vec(l, 2), mod_vec(l, 4), mod_vec(l, 3),
                          mod_vec(l, 5), norm2_g[l].reshape(1, d), ffn_up_b, ffn_conv_w[l],
                          ffn_conv_b[l].reshape(1, -1), ffn_down_b, final_g.reshape(1, d),
                          tm_ffn, tf, l == depth - 1)
    return x
```

```python
import functools

import jax
import jax.numpy as jnp
from jax import lax
from jax.experimental import pallas as pl
from jax.experimental.pallas import tpu as pltpu

F32 = jnp.float32
BF16 = jnp.bfloat16
MXU_DT = jnp.bfloat16

ATT_DIM = 64
ATT_HEADS = 8
ATT_KV_HEADS = 2
ATT_REP = ATT_HEADS // ATT_KV_HEADS
ATT_WIDTH = ATT_HEADS * ATT_DIM
KV_WIDTH = ATT_KV_HEADS * ATT_DIM
ROPE_DIM = ATT_DIM // 4
ROPE_HALF = ROPE_DIM // 2
ROPE_THETA = 500000.0
CMP_BLOCK = 32
CMP_STRIDE = 16
SEL_BLOCK = 64
SEL_TOPK = 16
WINDOW = 512
FORCE_BONUS = 1.0e4
M_HEADS = 4
M_DIM = 128
M_WIDTH = M_HEADS * M_DIM
M_CONV = 4
FFN_CONV = 3
RMS_EPS = 1e-6
NEG = -1e30
Q_SCALE = ATT_DIM ** -0.5 * 1.4426950408889634
BIAS_GROUP = 8

LANES = 128
SUBLANES = 8
VMEM_LIMIT = 56 * 1024 * 1024

COL_Q = 0
COL_KC = 512
COL_VC = 640
COL_KS = 768
COL_VS = 896
COL_KW = 1024
COL_VW = 1152
COL_MQK = 1280
COL_MV = 2304
COL_MO = 2816
COL_SMALL = 3328
IN_COLS_PACKED = 3456
SMALL_I = 3 * ATT_HEADS
SMALL_F = SMALL_I + M_HEADS

_NT = (((1,), (1,)), ((), ()))
_TN = (((0,), (0,)), ((), ()))


def _params(n_axes):
    return pltpu.CompilerParams(dimension_semantics=("arbitrary",) * n_axes,
                                vmem_limit_bytes=VMEM_LIMIT)


def _dot(a, b):
    return jnp.dot(a.astype(MXU_DT), b.astype(MXU_DT), preferred_element_type=F32)


def _dot_nt(a, b):
    return lax.dot_general(a.astype(MXU_DT), b.astype(MXU_DT), _NT, preferred_element_type=F32)


def _dot_tn(a, b):
    return lax.dot_general(a.astype(MXU_DT), b.astype(MXU_DT), _TN, preferred_element_type=F32)


def _sigmoid(x):
    return 1.0 / (1.0 + jnp.exp(-x))


def _silu(x):
    return x * _sigmoid(x)


def _shift_rows(x, prev8, k):
    rolled = pltpu.roll(x, k, 0)
    fix = pltpu.roll(prev8, k, 0)
    row = lax.broadcasted_iota(jnp.int32, fix.shape, 0)
    top = jnp.where(row < k, fix, rolled[:SUBLANES])
    return jnp.concatenate([top, rolled[SUBLANES:]], axis=0)


def _ada_body(c_ref, w_ref, b_ref, o_ref):
    c = c_ref[...]
    o_ref[0] = _dot(_silu(c), w_ref[0]) + b_ref[0]


def _ada_call(c8, ada_w, ada_b3, tn):
    depth, d, n = ada_w.shape
    return pl.pallas_call(
        _ada_body,
        grid=(depth, n // tn),
        in_specs=[pl.BlockSpec((SUBLANES, d), lambda l, j: (0, 0)),
                  pl.BlockSpec((1, d, tn), lambda l, j: (l, 0, j)),
                  pl.BlockSpec((1, 1, tn), lambda l, j: (l, 0, j))],
        out_specs=pl.BlockSpec((1, SUBLANES, tn), lambda l, j: (l, 0, j)),
        out_shape=jax.ShapeDtypeStruct((depth, SUBLANES, n), F32),
        compiler_params=_params(2),
        name="ada",
    )(c8, ada_w, ada_b3)


def _rope_body(pos_ref, freq_ref, rc_ref, rs1_ref, rs2_ref):
    pos = pos_ref[0].astype(F32)
    ang = pos * freq_ref[...]
    d = lax.broadcasted_iota(jnp.int32, ang.shape, 1) % ATT_DIM
    cos = jnp.cos(ang)
    sin = jnp.sin(ang)
    rc_ref[0] = jnp.where(d < ROPE_DIM, cos, 1.0)
    rs1_ref[0] = jnp.where(d < ROPE_HALF, -sin, 0.0)
    rs2_ref[0] = jnp.where((d >= ROPE_HALF) & (d < ROPE_DIM), sin, 0.0)


def _rope_call(pos3, freq, tm):
    b, s, _ = pos3.shape
    spec = pl.BlockSpec((1, tm, LANES), lambda i, j: (i, j, 0))
    shp = jax.ShapeDtypeStruct((b, s, LANES), F32)
    return pl.pallas_call(
        _rope_body,
        grid=(b, s // tm),
        in_specs=[pl.BlockSpec((1, tm, 1), lambda i, j: (i, j, 0)),
                  pl.BlockSpec((1, LANES), lambda i, j: (0, 0))],
        out_specs=[spec, spec, spec],
        out_shape=[shp, shp, shp],
        compiler_params=_params(2),
        name="rope",
    )(pos3, freq)


def _proj_body(x_ref, sc_ref, sh_ref, g_ref, w_ref, rc_ref, rs1_ref, rs2_ref,
               q_ref, kc_ref, vc_ref, ks_ref, vst_ref, kw_ref, vwt_ref,
               mqk_ref, mv_ref, mo_ref, sm_ref, smt_ref):
    x = x_ref[0]
    var = jnp.mean(x * x, axis=-1, keepdims=True)
    h = (x * lax.rsqrt(var + RMS_EPS)) * g_ref[...]
    h = h * (1.0 + sc_ref[0]) + sh_ref[0]
    hb = h.astype(MXU_DT)

    def mm(c0, n):
        return jnp.dot(hb, w_ref[:, c0:c0 + n], preferred_element_type=F32)

    rc, rs1, rs2 = rc_ref[0], rs1_ref[0], rs2_ref[0]

    def rope(y):
        return (y * rc + pltpu.roll(y, LANES - ROPE_HALF, 1) * rs1
                + pltpu.roll(y, ROPE_HALF, 1) * rs2)

    for j in range(ATT_HEADS // 4):
        y2 = mm(COL_Q + 2 * LANES * j, 2 * LANES)
        for i in range(2):
            yt = (rope(y2[:, LANES * i:LANES * (i + 1)]) * Q_SCALE).T
            q_ref[0, 4 * j + 2 * i] = yt[:ATT_DIM].astype(q_ref.dtype)
            q_ref[0, 4 * j + 2 * i + 1] = yt[ATT_DIM:].astype(q_ref.dtype)
    y2 = mm(COL_KC, 2 * KV_WIDTH)
    kc_ref[0] = rope(y2[:, :KV_WIDTH])
    vc_ref[0] = y2[:, KV_WIDTH:]
    y2 = mm(COL_KS, 2 * KV_WIDTH)
    y = rope(y2[:, :KV_WIDTH])
    lane = lax.broadcasted_iota(jnp.int32, y.shape, 1)
    t = pl.program_id(1) * y.shape[0] + lax.broadcasted_iota(jnp.int32, y.shape, 0)
    onehot = jnp.where(lane - ATT_DIM == (t // SEL_BLOCK) % BIAS_GROUP, 1.0, 0.0)
    ks_ref[0, 0] = jnp.where(lane < ATT_DIM, y, onehot).astype(ks_ref.dtype)
    ks_ref[0, 1] = jnp.where(lane < ATT_DIM, pltpu.roll(y, ATT_DIM, 1), onehot).astype(ks_ref.dtype)
    vst_ref[0] = y2[:, KV_WIDTH:].T.astype(vst_ref.dtype)
    y2 = mm(COL_KW, 2 * KV_WIDTH)
    y = rope(y2[:, :KV_WIDTH])
    kw_ref[0, 0] = y[:, :ATT_DIM].astype(kw_ref.dtype)
    kw_ref[0, 1] = y[:, ATT_DIM:].astype(kw_ref.dtype)
    vwt_ref[0] = y2[:, KV_WIDTH:].T.astype(vwt_ref.dtype)
    for j in range(2 * M_WIDTH // 256):
        mqk_ref[0, :, 256 * j:256 * (j + 1)] = mm(COL_MQK + 256 * j, 256).astype(mqk_ref.dtype)
    for j in range(M_WIDTH // 256):
        mv_ref[0, :, 256 * j:256 * (j + 1)] = mm(COL_MV + 256 * j, 256).astype(mv_ref.dtype)
        mo_ref[0, :, 256 * j:256 * (j + 1)] = mm(COL_MO + 256 * j, 256).astype(mo_ref.dtype)
    sm = mm(COL_SMALL, LANES)
    sm_ref[0] = sm
    smt_ref[0] = sm.T


def _proj_call(x, sc, sh, g, w, layer, rc, rs1, rs2, tm):
    b, s, d = x.shape
    row = lambda n: pl.BlockSpec((1, tm, n), lambda i, j: (i, j, 0))
    vec = pl.BlockSpec((1, 1, d), lambda i, j: (i, 0, 0))
    headed = lambda n, w: pl.BlockSpec((1, n, tm, w), lambda i, j: (i, 0, j, 0))
    tposed = pl.BlockSpec((1, LANES, tm), lambda i, j: (i, 0, j))
    sds = jax.ShapeDtypeStruct
    return pl.pallas_call(
        _proj_body,
        grid=(b, s // tm),
        in_specs=[row(d), vec, vec,
                  pl.BlockSpec((1, d), lambda i, j: (0, 0)),
                  pl.BlockSpec((None, d, IN_COLS_PACKED), lambda i, j: (layer, 0, 0)),
                  row(LANES), row(LANES), row(LANES)],
        out_specs=[pl.BlockSpec((1, ATT_HEADS, ATT_DIM, tm), lambda i, j: (i, 0, 0, j)),
                   row(KV_WIDTH), row(KV_WIDTH),
                   headed(ATT_KV_HEADS, LANES), tposed, headed(ATT_KV_HEADS, ATT_DIM), tposed,
                   row(2 * M_WIDTH), row(M_WIDTH), row(M_WIDTH), row(LANES), tposed],
        out_shape=[sds((b, ATT_HEADS, ATT_DIM, s), MXU_DT),
                   sds((b, s, KV_WIDTH), F32), sds((b, s, KV_WIDTH), F32),
                   sds((b, ATT_KV_HEADS, s, LANES), MXU_DT), sds((b, KV_WIDTH, s), MXU_DT),
                   sds((b, ATT_KV_HEADS, s, ATT_DIM), MXU_DT), sds((b, KV_WIDTH, s), MXU_DT),
                   sds((b, s, 2 * M_WIDTH), MXU_DT), sds((b, s, M_WIDTH), MXU_DT),
                   sds((b, s, M_WIDTH), MXU_DT), sds((b, s, LANES), F32), sds((b, LANES, s), F32)],
        compiler_params=_params(2),
        name="proj",
    )(x, sc, sh, g, w, rc, rs1, rs2)


def _compress_body(kr_ref, pa_ref, pb_ref, w1a_ref, w1b_ref, w2_ref, out_ref, outt_ref):
    kr = kr_ref[0]
    nc = kr.shape[0]
    ya = _dot(kr + pa_ref[...], w1a_ref[...])
    yb = _dot(kr + pb_ref[...], w1b_ref[...])
    pre = ya + pltpu.roll(yb, nc - 1, 0)
    out = _dot(_silu(pre), w2_ref[...])
    row = lax.broadcasted_iota(jnp.int32, out.shape, 0)
    out = jnp.where(row < nc - 1, out, 0.0)
    out_ref[0, 0] = out[:, :ATT_DIM].astype(out_ref.dtype)
    out_ref[0, 1] = out[:, ATT_DIM:].astype(out_ref.dtype)
    outt_ref[0] = out.T.astype(outt_ref.dtype)


def _compress_call(kr, pa, pb, w1a, w1b, w2e):
    b, nc, width = kr.shape
    full = lambda a: pl.BlockSpec(a.shape, lambda i: (0,) * a.ndim)
    return pl.pallas_call(
        _compress_body,
        grid=(b,),
        in_specs=[pl.BlockSpec((1, nc, width), lambda i: (i, 0, 0)),
                  full(pa), full(pb), full(w1a), full(w1b), full(w2e)],
        out_specs=[pl.BlockSpec((1, ATT_KV_HEADS, nc, ATT_DIM), lambda i: (i, 0, 0, 0)),
                   pl.BlockSpec((1, KV_WIDTH, nc), lambda i: (i, 0, 0))],
        out_shape=[jax.ShapeDtypeStruct((b, ATT_KV_HEADS, nc, ATT_DIM), MXU_DT),
                   jax.ShapeDtypeStruct((b, KV_WIDTH, nc), MXU_DT)],
        compiler_params=_params(1),
        name="compress",
    )(kr, pa, pb, w1a, w1b, w2e)


def _heads_on_lanes(qt_ref):
    return jnp.concatenate([qt_ref[0, r] for r in range(ATT_REP)], axis=1)


def _tile_heads(a):
    return jnp.concatenate([a] * ATT_REP, axis=1)


def _with_ones_row(vt):
    n = vt.shape[1]
    extra = jnp.where(lax.broadcasted_iota(jnp.int32, (2 * SUBLANES, n), 0) == 0, 1.0, 0.0)
    return jnp.concatenate([vt, extra.astype(vt.dtype)], axis=0)


def _cmp_body(qt_ref, kc_ref, vct_ref, ovt_ref, ocmp_ref, bias_ref, *, tq, k_top, n_split):
    t0 = pl.program_id(2) * tq
    nc_all = kc_ref.shape[2]
    nsel_all = ovt_ref.shape[0]
    rows_per_sel = SEL_BLOCK // CMP_STRIDE

    def compute(nc):
        nsel = nc // rows_per_sel
        row = lax.broadcasted_iota(jnp.int32, (nc, tq), 0)
        tpos = t0 + lax.broadcasted_iota(jnp.int32, (nc, tq), 1)
        mbias = jnp.where(row * CMP_STRIDE + (CMP_BLOCK - 1) <= tpos, 0.0, NEG)
        s = jnp.dot(kc_ref[0, 0, :nc, :], _heads_on_lanes(qt_ref), preferred_element_type=F32)
        s = s + _tile_heads(mbias)
        m = jnp.max(s, axis=0, keepdims=True)
        m = jnp.where(m > 0.5 * NEG, m, 0.0)
        p = jnp.exp2(s - m).astype(MXU_DT)
        oa = jnp.dot(_with_ones_row(vct_ref[0, :, :nc]), p, preferred_element_type=F32)
        d = oa[ATT_DIM:ATT_DIM + 1]
        inv = 1.0 / jnp.where(d > 0.0, d, 1.0)
        o = oa[:ATT_DIM] * inv
        impa = jnp.dot(ovt_ref[:nsel, :nc], p, preferred_element_type=F32) * inv
        imp = impa[:, :tq]
        for r in range(ATT_REP):
            ocmp_ref[0, r * ATT_DIM:(r + 1) * ATT_DIM, :] = o[:, r * tq:(r + 1) * tq]
            if r:
                imp = imp + impa[:, r * tq:(r + 1) * tq]
        jidx = lax.broadcasted_iota(jnp.int32, (nsel, tq), 0)
        tq_pos = t0 + lax.broadcasted_iota(jnp.int32, (nsel, tq), 1)
        jcur = tq_pos // SEL_BLOCK
        forced = (jidx == 0) | (jidx == jcur) | (jidx == jcur - 1)
        valid = jidx * SEL_BLOCK <= tq_pos
        if nsel < nsel_all:
            bias_ref[0, 0, 0, nsel:, :] = jnp.full((nsel_all - nsel, tq), NEG, F32)
        rounds = min(k_top, nsel)
        jcur_row = (t0 + lax.broadcasted_iota(jnp.int32, (1, tq), 1)) // SEL_BLOCK
        quota = rounds - jnp.minimum(jcur_row + 1, 3)
        score = jnp.where(valid & jnp.logical_not(forced), imp, NEG)
        cand = score > 0.5 * NEG
        n_rounds = max(rounds - 1, 0)
        n_free = max(rounds - 3, 0)

        def drop_max(cur_score, r, exact):
            top = jnp.max(cur_score, axis=0, keepdims=True)
            if exact:
                first = jnp.min(jnp.where(cur_score == top, jidx, nsel), axis=0, keepdims=True)
                hit = jidx == first
            else:
                hit = cur_score == top
            if r >= n_free:
                hit = hit & (quota > r)
            return jnp.where(hit, NEG, cur_score)

        fast = score
        for r in range(n_rounds):
            fast = drop_max(fast, r, exact=False)
        picked = cand & (fast < 0.5 * NEG)
        bias_ref[0, 0, 0, :nsel, :] = jnp.where(picked | (forced & valid), 0.0, NEG)
        n_picked = jnp.sum(jnp.where(picked, 1.0, 0.0), axis=0, keepdims=True)
        n_cand = jnp.sum(jnp.where(cand, 1.0, 0.0), axis=0, keepdims=True)
        want = jnp.minimum(n_cand, jnp.maximum(quota, 0).astype(F32))
        mismatch = jnp.max(jnp.abs(n_picked - want))

        @pl.when(mismatch > 0.0)
        def _():
            slow = score
            for r in range(n_rounds):
                slow = drop_max(slow, r, exact=True)
            bias_ref[0, 0, 0, :nsel, :] = jnp.where((cand & (slow < 0.5 * NEG)) | (forced & valid), 0.0, NEG)

    chunk_pos = nc_all // n_split * CMP_STRIDE
    need = jnp.minimum((t0 + tq - 1) // chunk_pos + 1, n_split)
    for k in range(1, n_split + 1):
        pl.when(need == k)(functools.partial(compute, nc_all // n_split * k))


def _cmp_call(qt, kcmp, vcmpt, ovt, tq, k_top):
    b, _, _, s = qt.shape
    nc = kcmp.shape[2]
    nsel = ovt.shape[0]
    nq = s // tq
    n_split = 1
    for cand in (8, 4):
        if nc % (cand * LANES) == 0 and (nc // cand * CMP_STRIDE) % tq == 0:
            n_split = cand
            break
    return pl.pallas_call(
        functools.partial(_cmp_body, tq=tq, k_top=k_top, n_split=n_split),
        grid=(b, ATT_KV_HEADS, nq),
        in_specs=[pl.BlockSpec((1, ATT_REP, ATT_DIM, tq), lambda i, g, j: (i, g, 0, j)),
                  pl.BlockSpec((1, 1, nc, ATT_DIM), lambda i, g, j: (i, g, 0, 0)),
                  pl.BlockSpec((1, ATT_DIM, nc), lambda i, g, j: (i, g, 0)),
                  pl.BlockSpec((nsel, nc), lambda i, g, j: (0, 0))],
        out_specs=[pl.BlockSpec((1, ATT_REP * ATT_DIM, tq), lambda i, g, j: (i, g, j)),
                   pl.BlockSpec((1, 1, 1, nsel, tq), lambda i, g, j: (i, g, j, 0, 0))],
        out_shape=[jax.ShapeDtypeStruct((b, ATT_WIDTH, s), F32),
                   jax.ShapeDtypeStruct((b, ATT_KV_HEADS, nq, nsel, tq), F32)],
        compiler_params=_params(3),
        name="cmp",
    )(qt, kcmp, vcmpt, ovt)


def _slc_body(qt_ref, ks_ref, vst_ref, kw_ref, vwt_ref, bias_ref, ocmp_ref, gt_ref, gn_ref,
              o_ref, qa0_scr, qa1_scr, s0_scr, s1_scr, mx0_scr, mx1_scr, m_scr, acc_scr,
              mw_scr, accw_scr, *, tq):
    qa_scr = (qa0_scr, qa1_scr)
    s_scr = (s0_scr, s1_scr)
    mx_scr = (mx0_scr, mx1_scr)
    g = pl.program_id(1)
    qi = pl.program_id(2)
    t0 = qi * tq
    blocks_per_tile = tq // SEL_BLOCK
    qt = _heads_on_lanes(qt_ref)
    for qa in qa_scr:
        qa[:ATT_DIM, :] = qt
        qa[ATT_DIM:, :] = jnp.zeros((LANES - ATT_DIM, qa.shape[1]), qa.dtype)
    for m_ref, a_ref in ((m_scr, acc_scr), (mw_scr, accw_scr)):
        m_ref[...] = jnp.full(m_ref.shape, NEG, F32)
        a_ref[...] = jnp.zeros(a_ref.shape, F32)

    def stage_a(kv, slot, dummy=None, diag=False):
        kvc = jnp.maximum(kv, 0)
        k0 = pl.multiple_of(kvc * tq, tq)
        grp = pl.multiple_of((kvc * blocks_per_tile) // BIAS_GROUP * BIAS_GROUP, BIAS_GROUP)
        b8 = bias_ref[0, 0, 0, pl.ds(grp, BIAS_GROUP), :]
        if dummy is not None:
            b8 = jnp.where(dummy, NEG, b8)
        b16 = jnp.concatenate([b8, jnp.zeros_like(b8)], axis=0)
        qa = qa_scr[slot]
        qa[ATT_DIM:ATT_DIM + 2 * SUBLANES, :] = _tile_heads(b16).astype(qa.dtype)
        s = jnp.dot(ks_ref[0, 0, pl.ds(k0, tq), :], qa[...], preferred_element_type=F32)
        if diag:
            row = lax.broadcasted_iota(jnp.int32, (tq, tq), 0)
            lane = lax.broadcasted_iota(jnp.int32, (tq, tq), 1)
            s = s + _tile_heads(jnp.where(row <= lane, 0.0, NEG))
        s_scr[slot][...] = s
        mx_scr[slot][...] = jnp.max(s, axis=0, keepdims=True)

    def softmax_pv(slot, v_ref, k0, m_ref, acc_ref):
        m_old = m_ref[...]
        m_new = jnp.maximum(m_old, mx_scr[slot][...])
        alpha = jnp.exp2(m_old - m_new)
        p = jnp.exp2(s_scr[slot][...] - m_new).astype(MXU_DT)
        vt = _with_ones_row(v_ref[0, :, pl.ds(k0, tq)])
        acc_ref[...] = alpha * acc_ref[...] + jnp.dot(vt, p, preferred_element_type=F32)
        m_ref[...] = m_new

    def stage_b(kv, slot):
        softmax_pv(slot, vst_ref, pl.multiple_of(jnp.maximum(kv, 0) * tq, tq), m_scr, acc_scr)

    n_win = WINDOW // tq + 1
    wrow = lax.broadcasted_iota(jnp.int32, (tq, tq), 0)
    wlane = lax.broadcasted_iota(jnp.int32, (tq, tq), 1)

    def win_start(wi):
        return t0 - WINDOW + wi * tq

    def stage_a_win(wi, slot):
        start = win_start(wi)
        k0 = pl.multiple_of(jnp.maximum(start, 0), tq)
        s = jnp.dot(kw_ref[0, 0, pl.ds(k0, tq), :], qt, preferred_element_type=F32)
        if wi == 0:
            visible = (wrow > wlane) & (start >= 0)
        elif wi == n_win - 1:
            visible = wrow <= wlane
        else:
            visible = jnp.broadcast_to(start >= 0, (tq, tq))
        s = s + _tile_heads(jnp.where(visible, 0.0, NEG))
        s_scr[slot][...] = s
        mx_scr[slot][...] = jnp.max(s, axis=0, keepdims=True)

    def stage_b_win(wi, slot):
        softmax_pv(slot, vwt_ref, pl.multiple_of(jnp.maximum(win_start(wi), 0), tq), mw_scr, accw_scr)

    off = 1 - qi % 2
    n_pairs = (qi + 2) // 2
    stage_a(-off, 0, dummy=off == 1)

    def pair(first):
        stage_a(first + 1, 1)
        stage_b(first, 0)
        stage_a(first + 2, 0)
        stage_b(first + 1, 1)

    def quad_body(i, carry):
        pair(4 * i - off)
        pair(4 * i + 2 - off)
        return carry

    def pair_body(pr, carry):
        pair(2 * pr - off)
        return carry

    n_quads = (n_pairs - 1) // 2
    lax.fori_loop(0, n_quads, quad_body, 0)
    lax.fori_loop(2 * n_quads, n_pairs - 1, pair_body, 0)
    stage_a(qi, 1, diag=True)
    stage_b(qi - 1, 0)
    stage_a_win(0, 0)
    stage_b(qi, 1)
    for wi in range(1, n_win):
        stage_a_win(wi, wi % 2)
        stage_b_win(wi - 1, (wi - 1) % 2)
    stage_b_win(n_win - 1, (n_win - 1) % 2)

    o_win_all = accw_scr[:ATT_DIM, :] / accw_scr[ATT_DIM:ATT_DIM + 1, :]
    o_slc_all = acc_scr[:ATT_DIM, :] / acc_scr[ATT_DIM:ATT_DIM + 1, :]
    for r in range(ATT_REP):
        o_win = o_win_all[:, r * tq:(r + 1) * tq]
        o_slc = o_slc_all[:, r * tq:(r + 1) * tq]
        o_cmp = ocmp_ref[0, r * ATT_DIM:(r + 1) * ATT_DIM, :]
        gbase = (g * ATT_REP + r) * 3
        g_cmp = _sigmoid(gt_ref[0, pl.ds(gbase, 1), :])
        g_slc = _sigmoid(gt_ref[0, pl.ds(gbase + 1, 1), :])
        g_win = _sigmoid(gt_ref[0, pl.ds(gbase + 2, 1), :])
        o = g_cmp * o_cmp + g_slc * o_slc + g_win * o_win
        o = o * lax.rsqrt(jnp.mean(o * o, axis=0, keepdims=True) + RMS_EPS)
        o = o * gn_ref[r * ATT_DIM:(r + 1) * ATT_DIM, :]
        o_ref[0, r * ATT_DIM:(r + 1) * ATT_DIM, :] = o.astype(o_ref.dtype)


def _slc_call(qt, ks, vst, kw, vwt, bias, ocmp, smallt, gn, tq):
    b, _, _, s = qt.shape
    nsel = bias.shape[3]
    gw = ATT_REP * ATT_DIM
    vres = pl.BlockSpec((1, ATT_DIM, s), lambda i, g, j: (i, g, 0))
    return pl.pallas_call(
        functools.partial(_slc_body, tq=tq),
        grid=(b, ATT_KV_HEADS, s // tq),
        in_specs=[pl.BlockSpec((1, ATT_REP, ATT_DIM, tq), lambda i, g, j: (i, g, 0, j)),
                  pl.BlockSpec((1, 1, s, LANES), lambda i, g, j: (i, g, 0, 0)), vres,
                  pl.BlockSpec((1, 1, s, ATT_DIM), lambda i, g, j: (i, g, 0, 0)), vres,
                  pl.BlockSpec((1, 1, 1, nsel, tq), lambda i, g, j: (i, g, j, 0, 0)),
                  pl.BlockSpec((1, gw, tq), lambda i, g, j: (i, g, j)),
                  pl.BlockSpec((1, LANES, tq), lambda i, g, j: (i, 0, j)),
                  pl.BlockSpec((gw, tq), lambda i, g, j: (g, 0))],
        out_specs=pl.BlockSpec((1, gw, tq), lambda i, g, j: (i, g, j)),
        out_shape=jax.ShapeDtypeStruct((b, ATT_WIDTH, s), MXU_DT),
        scratch_shapes=[pltpu.VMEM((LANES, ATT_REP * tq), MXU_DT),
                        pltpu.VMEM((LANES, ATT_REP * tq), MXU_DT),
                        pltpu.VMEM((tq, ATT_REP * tq), F32),
                        pltpu.VMEM((tq, ATT_REP * tq), F32),
                        pltpu.VMEM((1, ATT_REP * tq), F32),
                        pltpu.VMEM((1, ATT_REP * tq), F32),
                        pltpu.VMEM((1, ATT_REP * tq), F32),
                        pltpu.VMEM((ATT_DIM + 2 * SUBLANES, ATT_REP * tq), F32),
                        pltpu.VMEM((1, ATT_REP * tq), F32),
                        pltpu.VMEM((ATT_DIM + 2 * SUBLANES, ATT_REP * tq), F32)],
        compiler_params=_params(3),
        name="slc",
    )(qt, ks, vst, kw, vwt, bias, ocmp, smallt, gn)


def _mlstm_body(mqk_ref, mv_ref, mo_ref, sm_ref, smt_ref, cw_ref, cb_ref, gbr_ref, gbc_ref, ng_ref,
                o_ref, c_scr, m_scr, tail_scr, *, chunk):
    @pl.when(pl.program_id(1) == 0)
    def _():
        c_scr[...] = jnp.zeros(c_scr.shape, F32)
        m_scr[...] = jnp.full(m_scr.shape, NEG, F32)
        tail_scr[...] = jnp.zeros(tail_scr.shape, F32)

    x = mqk_ref[0].astype(F32)
    tail = tail_scr[...]
    y = cb_ref[...] + x * cw_ref[M_CONV - 1:M_CONV, :]
    for k in range(1, M_CONV):
        y = y + _shift_rows(x, tail, k) * cw_ref[M_CONV - 1 - k:M_CONV - k, :]
    tail_scr[...] = x[chunk - SUBLANES:, :]
    qk = _silu(y)
    sm = sm_ref[0] + gbr_ref[...]
    smt = smt_ref[0] + gbc_ref[...]
    row = lax.broadcasted_iota(jnp.int32, (chunk, chunk), 0)
    col = lax.broadcasted_iota(jnp.int32, (chunk, chunk), 1)
    causal = row >= col
    ones = jnp.ones((chunk, M_DIM), F32)

    def log_sigmoid(z):
        return jnp.minimum(z, 0.0) - jnp.log(1.0 + jnp.exp(-jnp.abs(z)))

    for h in range(M_HEADS):
        q = qk[:, h * M_DIM:(h + 1) * M_DIM]
        k = qk[:, M_WIDTH + h * M_DIM:M_WIDTH + (h + 1) * M_DIM] * (M_DIM ** -0.5)
        v = mv_ref[0, :, h * M_DIM:(h + 1) * M_DIM]
        i_col = sm[:, SMALL_I + h:SMALL_I + h + 1]
        lf_col = log_sigmoid(sm[:, SMALL_F + h:SMALL_F + h + 1])
        i_row = smt[SMALL_I + h:SMALL_I + h + 1, :]
        lf_row = log_sigmoid(smt[SMALL_F + h:SMALL_F + h + 1, :])
        b_col = jnp.sum(jnp.where(causal, lf_row, 0.0), axis=1, keepdims=True)
        b_row = jnp.sum(jnp.where(row <= col, lf_col, 0.0), axis=0, keepdims=True)
        m_prev = m_scr[h]
        dmat = jnp.where(causal, b_col - b_row + i_row, NEG)
        inter = b_col + m_prev
        mt = jnp.maximum(inter, jnp.max(dmat, axis=1, keepdims=True))
        w_intra = jnp.exp(dmat - mt)
        w_inter = jnp.exp(inter - mt)
        qkw = _dot_nt(q, k) * w_intra
        c_aug = c_scr[h]
        r1 = _dot(q, c_aug)
        num = w_inter * r1[:, :M_DIM] + _dot(qkw, v)
        den = w_inter * r1[:, M_DIM:M_DIM + 1] + jnp.sum(qkw, axis=1, keepdims=True)
        hh = num / jnp.maximum(jnp.abs(den), jnp.exp(-mt))
        b_last = b_col[chunk - 1:chunk, :]
        a_col = b_last - b_col + i_col
        m_new = jnp.maximum(b_last + m_prev, jnp.max(a_col, axis=0, keepdims=True))
        ws = jnp.exp(a_col - m_new)
        decay = jnp.exp(b_last + m_prev - m_new)
        v_aug = jnp.concatenate([v.astype(F32), ones], axis=1)
        c_scr[h] = decay * c_aug + _dot_tn(k * ws, v_aug)
        m_scr[h] = m_new
        hm = _sigmoid(mo_ref[0, :, h * M_DIM:(h + 1) * M_DIM].astype(F32)) * hh
        hm = hm * lax.rsqrt(jnp.mean(hm * hm, axis=1, keepdims=True) + RMS_EPS)
        o_ref[0, :, h * M_DIM:(h + 1) * M_DIM] = (hm * ng_ref[:, h * M_DIM:(h + 1) * M_DIM]).astype(o_ref.dtype)


def _mlstm_call(mqk, mv, mo, small, smallt, cw, cb, gbr, gbc, ng, chunk):
    b, s, _ = mv.shape
    row = lambda n: pl.BlockSpec((1, chunk, n), lambda i, j: (i, j, 0))
    full = lambda a: pl.BlockSpec(a.shape, lambda i, j: (0,) * a.ndim)
    return pl.pallas_call(
        functools.partial(_mlstm_body, chunk=chunk),
        grid=(b, s // chunk),
        in_specs=[row(2 * M_WIDTH), row(M_WIDTH), row(M_WIDTH), row(LANES),
                  pl.BlockSpec((1, LANES, chunk), lambda i, j: (i, 0, j)),
                  full(cw), full(cb), full(gbr), full(gbc), full(ng)],
        out_specs=row(M_WIDTH),
        out_shape=jax.ShapeDtypeStruct((b, s, M_WIDTH), MXU_DT),
        scratch_shapes=[pltpu.VMEM((M_HEADS, M_DIM, 2 * M_DIM), F32),
                        pltpu.VMEM((M_HEADS, 1, 1), F32),
                        pltpu.VMEM((SUBLANES, 2 * M_WIDTH), F32)],
        compiler_params=_params(2),
        name="mlstm",
    )(mqk, mv, mo, small, smallt, cw, cb, gbr, gbc, ng)


def _mix_ffn_body(x_ref, attt_ref, hm_ref, wo_ref, g1_ref, sc_ref, sh_ref, g2_ref, ng_ref,
                  wup_ref, cw_ref, cb_ref, wdn_ref, fg_ref, o_ref, tail_scr, g_scr, *, d_ff, tf, tiles_per_seq, final):
    @pl.when(pl.program_id(0) % tiles_per_seq == 0)
    def _():
        tail_scr[...] = jnp.zeros(tail_scr.shape, F32)

    d = x_ref.shape[-1]
    mix = _dot_tn(attt_ref[0], wo_ref[:ATT_WIDTH, :]) + _dot(hm_ref[0], wo_ref[ATT_WIDTH:, :])
    x = x_ref[0] + g1_ref[0] * mix
    var = jnp.mean(x * x, axis=-1, keepdims=True)
    h = (x * lax.rsqrt(var + RMS_EPS)) * ng_ref[...]
    hb = (h * (1.0 + sc_ref[0]) + sh_ref[0]).astype(MXU_DT)
    for c in range(d_ff // tf):
        a = jnp.dot(hb, wup_ref[:, c * tf:(c + 1) * tf], preferred_element_type=F32)
        v = jnp.dot(hb, wup_ref[:, d_ff + c * tf:d_ff + (c + 1) * tf], preferred_element_type=F32)
        tail = tail_scr[c]
        y = cb_ref[:, c * tf:(c + 1) * tf] + a * cw_ref[FFN_CONV - 1:FFN_CONV, c * tf:(c + 1) * tf]
        for k in range(1, FFN_CONV):
            y = y + _shift_rows(a, tail, k) * cw_ref[FFN_CONV - 1 - k:FFN_CONV - k, c * tf:(c + 1) * tf]
        tail_scr[c] = a[a.shape[0] - SUBLANES:, :]
        g_scr[:, c * tf:(c + 1) * tf] = (_silu(y) * v).astype(g_scr.dtype)
    out = x + g2_ref[0] * jnp.dot(g_scr[...], wdn_ref[...], preferred_element_type=F32)
    if final:
        var = jnp.mean(out * out, axis=-1, keepdims=True)
        out = (out * lax.rsqrt(var + RMS_EPS)) * fg_ref[...]
    o_ref[0] = out


def _mix_ffn_call(x, attt, hm, wo, layer, g1, sc, sh, g2, ng, wup, cw, cb, wdn, fg, tm, tf, final):
    b, s, d = x.shape
    d_ff = wdn.shape[1]
    stacked = lambda a: pl.BlockSpec((None,) + a.shape[1:], lambda i: (layer,) + (0,) * (a.ndim - 1),
                                     pipeline_mode=pl.Buffered(1))
    tiles_per_seq = s // tm
    row = lambda n: pl.BlockSpec((1, tm, n), lambda i: (i // tiles_per_seq, i % tiles_per_seq, 0))
    vec = pl.BlockSpec((1, 1, d), lambda i: (i // tiles_per_seq, 0, 0))
    full = lambda a: pl.BlockSpec(a.shape, lambda i: (0,) * a.ndim)
    return pl.pallas_call(
        functools.partial(_mix_ffn_body, d_ff=d_ff, tf=tf, tiles_per_seq=tiles_per_seq, final=final),
        grid=(b * tiles_per_seq,),
        in_specs=[row(d),
                  pl.BlockSpec((1, ATT_WIDTH, tm), lambda i: (i // tiles_per_seq, 0, i % tiles_per_seq)),
                  row(M_WIDTH), stacked(wo), vec, vec, vec, vec, full(ng),
                  stacked(wup), full(cw), full(cb), stacked(wdn), full(fg)],
        out_specs=row(d),
        out_shape=jax.ShapeDtypeStruct((b, s, d), F32),
        scratch_shapes=[pltpu.VMEM((d_ff // tf, SUBLANES, tf), F32),
                        pltpu.VMEM((tm, d_ff), MXU_DT)],
        compiler_params=_params(1),
        name="mix_ffn",
    )(x, attt, hm, wo, g1, sc, sh, g2, ng, wup, cw, cb, wdn, fg)


def _pick_tile(n, pref):
    t = min(n, pref)
    assert n % t == 0, (n, t)
    return t


def _pack_w_in(w_in):
    att_cols = ATT_WIDTH + 6 * KV_WIDTH
    n_gate = 3 * ATT_HEADS
    o = att_cols
    gates = w_in[..., o:o + n_gate]; o += n_gate
    mqk = w_in[..., o:o + 2 * M_WIDTH]; o += 2 * M_WIDTH
    mv = w_in[..., o:o + M_WIDTH]; o += M_WIDTH
    mif = w_in[..., o:o + 2 * M_HEADS]; o += 2 * M_HEADS
    mo = w_in[..., o:o + M_WIDTH]
    pad = jnp.zeros(w_in.shape[:-1] + (LANES - n_gate - 2 * M_HEADS,), w_in.dtype)
    return jnp.concatenate([w_in[..., :att_cols], mqk, mv, mo, gates, mif, pad], axis=-1).astype(MXU_DT)


def _expand_cmp_weights(pos, w1, w2):
    half = CMP_BLOCK // 2
    lead = w1.shape[:-2]
    w1r = w1.reshape(lead + (CMP_BLOCK, ATT_DIM, ATT_DIM))
    eye = jnp.eye(ATT_KV_HEADS, dtype=w1.dtype)

    def expand(wl):
        z = wl[..., :, None, :, None, :] * eye[None, :, None, :, None]
        return z.reshape(lead + (half * KV_WIDTH, KV_WIDTH)).astype(MXU_DT)

    def expand_pos(pl_):
        z = jnp.broadcast_to(pl_[..., :, None, :], lead + (half, ATT_KV_HEADS, ATT_DIM))
        return z.reshape(lead + (1, half * KV_WIDTH))

    w2e = (w2[..., None, :, None, :] * eye[:, None, :, None]).reshape(lead + (KV_WIDTH, KV_WIDTH))
    return (expand_pos(pos[..., :half, :]), expand_pos(pos[..., half:, :]),
            expand(w1r[..., :half, :, :]), expand(w1r[..., half:, :, :]), w2e.astype(MXU_DT))


def kernel(x, c, positions, norm1_g, norm2_g, ada_w, ada_b, w_in, cmp_pos, cmp_w1, cmp_w2, att_norm_g,
           m_conv_w, m_conv_b, m_gate_b, m_norm_g, w_out, ffn_up, ffn_conv_w, ffn_conv_b, ffn_down, final_g):
    b, s, d = x.shape
    depth = ada_w.shape[0]
    d_ff = ffn_down.shape[1]
    assert s % (CMP_STRIDE * SUBLANES) == 0 and d % LANES == 0 and b <= SUBLANES
    tm_proj = _pick_tile(s, 512)
    tm_ffn = _pick_tile(s, 1024)
    tq = _pick_tile(s, 256)
    chunk = _pick_tile(s, 512)
    tf = 256
    assert d_ff % tf == 0 and WINDOW % tq == 0 and s >= WINDOW + tq
    nc = s // CMP_STRIDE
    nsel = s // SEL_BLOCK
    k_top = min(SEL_TOPK, nsel)

    c8 = jnp.zeros((SUBLANES, d), F32).at[:b].set(c)
    mod = _ada_call(c8, ada_w, ada_b.reshape(depth, 1, 6 * d), _pick_tile(6 * d, 1536))

    def mod_vec(l, k):
        return mod[l, :b, k * d:(k + 1) * d].reshape(b, 1, d)

    lane_d = jnp.arange(LANES) % ATT_DIM
    inv_freq = ROPE_THETA ** (-(2 * (lane_d % ROPE_HALF)).astype(F32) / ROPE_DIM)
    rc, rs1, rs2 = _rope_call(positions.reshape(b, s, 1), inv_freq.reshape(1, LANES), _pick_tile(s, 1024))

    c_start = jnp.arange(nc)[None, :] * CMP_STRIDE
    s_start = jnp.arange(nsel)[:, None] * SEL_BLOCK
    ovt = ((c_start <= s_start + SEL_BLOCK - 1) & (c_start + CMP_BLOCK - 1 >= s_start)
           & (jnp.arange(nc)[None, :] < nc - 1)).astype(MXU_DT)

    w_in_p = _pack_w_in(w_in)
    w_out_b, ffn_up_b, ffn_down_b = (w.astype(MXU_DT) for w in (w_out, ffn_up, ffn_down))
    cmp_w = _expand_cmp_weights(cmp_pos, cmp_w1, cmp_w2)

    for l in range(depth):
        outs = _proj_call(x, mod_vec(l, 1), mod_vec(l, 0), norm1_g[l].reshape(1, d), w_in_p, l,
                          rc, rs1, rs2, tm_proj)
        q, kc, vc, ks, vst, kw, vwt, mqk, mv, mo, small, smallt = outs
        kcmp, _ = _compress_call(kc.reshape(b, nc, CMP_STRIDE * KV_WIDTH), *(w[l, 0] for w in cmp_w))
        _, vcmpt = _compress_call(vc.reshape(b, nc, CMP_STRIDE * KV_WIDTH), *(w[l, 1] for w in cmp_w))
        ocmp, bias = _cmp_call(q, kcmp, vcmpt, ovt, tq, k_top)
        gn = jnp.broadcast_to(att_norm_g[l][:, None], (ATT_WIDTH, tq))
        attt = _slc_call(q, ks, vst, kw, vwt, bias, ocmp, smallt, gn, tq)
        gbr = jnp.zeros((1, LANES), F32).at[0, SMALL_I:SMALL_I + 2 * M_HEADS].set(m_gate_b[l])
        hm = _mlstm_call(mqk, mv, mo, small, smallt, m_conv_w[l], m_conv_b[l].reshape(1, -1),
                         gbr, gbr.reshape(LANES, 1), m_norm_g[l].reshape(1, -1), chunk)
        x = _mix_ffn_call(x, attt, hm, w_out_b, l, mod_vec(l, 2), mod_vec(l, 4), mod_vec(l, 3),
                          mod_vec(l, 5), norm2_g[l].reshape(1, d), ffn_up_b, ffn_conv_w[l],
                          ffn_conv_b[l].reshape(1, -1), ffn_down_b, final_g.reshape(1, d),
                          tm_ffn, tf, l == depth - 1)
    return x
```
